```python
import math
import jax
import jax.numpy as jnp
from jax import lax
import numpy as np

D_MODEL = 1024
BATCH = 16
SEQ = 4096
DEPTH = 2

N_BRANCH = 4
BRANCH_WIDTH = D_MODEL // N_BRANCH
HEAD_DIM = 64
EPS = 1e-6
ROPE_DIM = HEAD_DIM // 4
ROPE_THETA = 500000.0
CONV_K = 4

A_GROUPS = ((128, 1), (512, 4), (2048, 16))
A_N_GROUPS = len(A_GROUPS)
A_HEADS = BRANCH_WIDTH // HEAD_DIM
A_QKV = A_N_GROUPS * A_HEADS * HEAD_DIM
A_BLOCK = 128

B_HEADS = BRANCH_WIDTH // HEAD_DIM
B_DK = HEAD_DIM
B_DV = HEAD_DIM
B_KEY = B_HEADS * B_DK
B_CHUNK = 64

C_HEADDIM = 64
C_HEADS = BRANCH_WIDTH // C_HEADDIM
C_GROUPS = 2
C_STATE = 128
C_XBC = BRANCH_WIDTH + 2 * C_GROUPS * C_STATE
C_CHUNK = 64

D_HEADS = 4
D_DV = BRANCH_WIDTH // D_HEADS
D_KEY = BRANCH_WIDTH // 2
D_DK = D_KEY // D_HEADS
D_GATE_RANK = 16
D_GATE_NORM = 16.0
D_CHUNK = 32

IN_SIZES = (A_QKV, A_QKV, A_QKV, BRANCH_WIDTH,
            2 * B_KEY + B_HEADS * B_DV, BRANCH_WIDTH, B_HEADS, B_HEADS,
            C_XBC, BRANCH_WIDTH, C_HEADS,
            D_KEY, D_KEY, D_HEADS * D_DV, BRANCH_WIDTH, D_GATE_RANK)
D_IN = sum(IN_SIZES)

kernel_name = "hybrid_dilated_delta_ssd_gla_block"


def rmsnorm(x, g):
    xf = x.astype(jnp.float32)
    xf = xf * lax.rsqrt(jnp.mean(jnp.square(xf), axis=-1, keepdims=True) + EPS)
    return (xf * g.astype(jnp.float32)).astype(x.dtype)


def l2norm(x):
    xf = x.astype(jnp.float32)
    return (xf * lax.rsqrt(jnp.sum(jnp.square(xf), axis=-1, keepdims=True) + EPS)).astype(x.dtype)


def causal_dwconv(x, w):
    k, c = w.shape
    return lax.conv_general_dilated(x, w[:, None, :].astype(x.dtype), (1,), ((k - 1, 0),),
                                    dimension_numbers=("NWC", "WIO", "NWC"), feature_group_count=c)


def partial_rope(x, positions):
    half = ROPE_DIM // 2
    inv_freq = ROPE_THETA ** (-jnp.arange(half, dtype=jnp.float32) / half)
    ang = positions.astype(jnp.float32)[..., None] * inv_freq
    ang = ang.reshape(ang.shape[:2] + (1,) * (x.ndim - 3) + (half,))
    cos, sin = jnp.cos(ang), jnp.sin(ang)
    xr = x[..., :ROPE_DIM].astype(jnp.float32)
    x1, x2 = xr[..., :half], xr[..., half:]
    rot = jnp.concatenate([x1 * cos - x2 * sin, x2 * cos + x1 * sin], axis=-1).astype(x.dtype)
    return jnp.concatenate([rot, x[..., ROPE_DIM:]], axis=-1)


def to_chunks(t, c):
    b, s = t.shape[:2]
    t = t.reshape((b, s // c, c) + t.shape[2:])
    return jnp.moveaxis(t, 3, 1)


def from_chunks(t):
    b, h, n, c, e = t.shape
    return jnp.moveaxis(t, 1, 3).reshape(b, n * c, h * e)


def scan_chunk_states(decay, delta):
    def step(state, inp):
        d_n, ds_n = inp
        return d_n * state + ds_n, state
    s0 = jnp.zeros(delta.shape[:2] + delta.shape[3:], delta.dtype)
    _, s_prev = lax.scan(step, s0, (jnp.moveaxis(decay, 2, 0), jnp.moveaxis(delta, 2, 0)))
    return jnp.moveaxis(s_prev, 0, 2)


def dilated_window_attention(q, k, v, win_sub, dilation):
    bsz, s, h, hd = q.shape
    n = s // dilation
    nb = -(-n // A_BLOCK)
    pad = nb * A_BLOCK - n

    def split(t):
        t = t.reshape(bsz, n, dilation, h, hd).transpose(0, 2, 3, 1, 4)
        t = jnp.pad(t, ((0, 0), (0, 0), (0, 0), (0, pad), (0, 0)))
        return t.reshape(bsz, dilation, h, nb, A_BLOCK, hd)

    def with_prev(t):
        prev = jnp.pad(t, ((0, 0), (0, 0), (0, 0), (1, 0), (0, 0), (0, 0)))[:, :, :, :-1]
        return jnp.concatenate([prev, t], axis=4)

    qb = split(q)
    kw, vw = with_prev(split(k)), with_prev(split(v))
    scores = jnp.einsum("bdhnqe,bdhnke->bdhnqk", qb, kw).astype(jnp.float32) * (hd ** -0.5)
    qpos = jnp.arange(A_BLOCK)[:, None] + A_BLOCK
    kpos = jnp.arange(2 * A_BLOCK)[None, :]
    dist = qpos - kpos
    band = (dist >= 0) & (dist <= win_sub)
    has_key = (jnp.arange(nb) > 0)[:, None, None] | (kpos >= A_BLOCK)[None]
    mask = band[None] & has_key
    scores = jnp.where(mask, scores, -jnp.inf)
    lse = jax.nn.logsumexp(scores, axis=-1)
    p = jnp.exp(scores - lse[..., None]).astype(v.dtype)
    o = jnp.einsum("bdhnqk,bdhnke->bdhnqe", p, vw)
    o = o.reshape(bsz, dilation, h, nb * A_BLOCK, hd)[:, :, :, :n].transpose(0, 3, 1, 2, 4).reshape(bsz, s, h, hd)
    lse = lse.reshape(bsz, dilation, h, nb * A_BLOCK)[..., :n].transpose(0, 3, 1, 2).reshape(bsz, s, h)
    return o, lse


def mixer_dilated(q, k, v, z, positions, q_norm, k_norm):
    bsz, s, _ = q.shape
    shp = (bsz, s, A_N_GROUPS, A_HEADS, HEAD_DIM)
    q = partial_rope(rmsnorm(q.reshape(shp), q_norm), positions)
    k = partial_rope(rmsnorm(k.reshape(shp), k_norm), positions)
    v = v.reshape(shp)
    outs, lses = [], []
    for gi, (window, dilation) in enumerate(A_GROUPS):
        o_g, lse_g = dilated_window_attention(q[:, :, gi], k[:, :, gi], v[:, :, gi], window // dilation, dilation)
        outs.append(o_g)
        lses.append(lse_g)
    wts = jax.nn.softmax(jnp.stack(lses, axis=2), axis=2)
    o = jnp.sum(jnp.stack(outs, axis=2).astype(jnp.float32) * wts[..., None], axis=2)
    o = o.reshape(bsz, s, BRANCH_WIDTH)
    return (o * jax.nn.silu(z.astype(jnp.float32))).astype(z.dtype)


def mixer_gated_delta(qkv, z, beta_raw, a_raw, conv_w, a_log, dt_bias, out_g):
    bsz, s, _ = qkv.shape
    dtype, f = qkv.dtype, jnp.float32
    qkv = jax.nn.silu(causal_dwconv(qkv, conv_w)).astype(f)
    q, k, v = jnp.split(qkv, [B_KEY, 2 * B_KEY], axis=-1)
    q = l2norm(q.reshape(bsz, s, B_HEADS, B_DK)) * (B_DK ** -0.5)
    k = l2norm(k.reshape(bsz, s, B_HEADS, B_DK))
    v = v.reshape(bsz, s, B_HEADS, B_DV)
    beta = jax.nn.sigmoid(beta_raw.astype(f))
    g = -jnp.exp(a_log.astype(f)) * jax.nn.softplus(a_raw.astype(f) + dt_bias.astype(f))
    qc, kc, vc = to_chunks(q, B_CHUNK), to_chunks(k, B_CHUNK), to_chunks(v, B_CHUNK)
    bc, gc = to_chunks(beta, B_CHUNK), to_chunks(g, B_CHUNK)
    cum = jnp.cumsum(gc, axis=-1)
    causal = jnp.tril(jnp.ones((B_CHUNK, B_CHUNK), bool))
    strict = jnp.tril(jnp.ones((B_CHUNK, B_CHUNK), bool), k=-1)
    gamma = jnp.exp(jnp.where(causal, cum[..., :, None] - cum[..., None, :], -jnp.inf))
    kb = kc * bc[..., None]
    a_kk = jnp.where(strict, jnp.einsum("bhncd,bhnkd->bhnck", kb, kc) * gamma, 0.0)
    rhs = jnp.concatenate([vc * bc[..., None], kb * jnp.exp(cum)[..., None]], axis=-1)
    sol = lax.linalg.triangular_solve(a_kk + jnp.eye(B_CHUNK, dtype=f), rhs,
                                      left_side=True, lower=True, unit_diagonal=True)
    u, w = sol[..., :B_DV], sol[..., B_DV:]
    a_qk = jnp.einsum("bhncd,bhnkd->bhnck", qc, kc) * gamma
    q_eff = qc * jnp.exp(cum)[..., None] - a_qk @ w
    o_intra = a_qk @ u
    k_dec = kc * jnp.exp(cum[..., -1:] - cum)[..., None]
    dec = jnp.exp(cum[..., -1])

    def step(state, inp):
        d_n, kd_n, u_n, w_n = inp
        v_new = u_n - jnp.einsum("bhck,bhkv->bhcv", w_n, state)
        return d_n[..., None, None] * state + jnp.einsum("bhck,bhcv->bhkv", kd_n, v_new), state

    s0 = jnp.zeros((bsz, B_HEADS, B_DK, B_DV), f)
    xs = tuple(jnp.moveaxis(t, 2, 0) for t in (dec, k_dec, u, w))
    _, s_prev = lax.scan(step, s0, xs)
    s_prev = jnp.moveaxis(s_prev, 0, 2)
    o = jnp.einsum("bhnck,bhnkv->bhncv", q_eff, s_prev) + o_intra
    o = from_chunks(rmsnorm(o, out_g))
    return (o * jax.nn.silu(z.astype(f))).astype(dtype)


def mixer_ssd(xbc, z, dt_raw, conv_w, conv_b, a_log, dt_bias, d_skip, out_g):
    bsz, s, _ = xbc.shape
    dtype, f = xbc.dtype, jnp.float32
    xbc = jax.nn.silu(causal_dwconv(xbc, conv_w) + conv_b).astype(f)
    xs, bm, cm = jnp.split(xbc, [BRANCH_WIDTH, BRANCH_WIDTH + C_GROUPS * C_STATE], axis=-1)
    xs = xs.reshape(bsz, s, C_HEADS, C_HEADDIM)
    rep = C_HEADS // C_GROUPS
    bm = jnp.repeat(bm.reshape(bsz, s, C_GROUPS, C_STATE), rep, axis=2)
    cm = jnp.repeat(cm.reshape(bsz, s, C_GROUPS, C_STATE), rep, axis=2)
    dt = jax.nn.softplus(dt_raw.astype(f) + dt_bias.astype(f))
    a = -jnp.exp(a_log.astype(f)) * dt
    xc, bc, cc = to_chunks(xs, C_CHUNK), to_chunks(bm, C_CHUNK), to_chunks(cm, C_CHUNK)
    dtc, ac = to_chunks(dt, C_CHUNK), to_chunks(a, C_CHUNK)
    acum = jnp.cumsum(ac, axis=-1)
    causal = jnp.tril(jnp.ones((C_CHUNK, C_CHUNK), bool))
    seg = jnp.exp(jnp.where(causal, acum[..., :, None] - acum[..., None, :], -jnp.inf))
    xdt = xc * dtc[..., None]
    y_diag = jnp.einsum("bhnqk,bhnkp->bhnqp", jnp.einsum("bhnqe,bhnke->bhnqk", cc, bc) * seg, xdt)
    states = jnp.einsum("bhnke,bhnkp->bhnpe", bc, xdt * jnp.exp(acum[..., -1:] - acum)[..., None])
    s_prev = scan_chunk_states(jnp.exp(acum[..., -1])[..., None, None], states)
    y_off = jnp.einsum("bhnqe,bhnpe->bhnqp", cc * jnp.exp(acum)[..., None], s_prev)
    y = y_diag + y_off + d_skip.astype(f)[None, :, None, None, None] * xc
    y = from_chunks(y) * jax.nn.silu(z.astype(f))
    y = rmsnorm(y.reshape(bsz, s, C_GROUPS, BRANCH_WIDTH // C_GROUPS),
                out_g.reshape(C_GROUPS, BRANCH_WIDTH // C_GROUPS)).reshape(bsz, s, BRANCH_WIDTH)
    return y.astype(dtype)


def mixer_gla(q, k, v, z, g_low, w_g2, b_g, out_g):
    bsz, s, _ = q.shape
    dtype, f = q.dtype, jnp.float32
    q = q.astype(f).reshape(bsz, s, D_HEADS, D_DK) * (D_DK ** -0.5)
    k = k.astype(f).reshape(bsz, s, D_HEADS, D_DK)
    v = v.astype(f).reshape(bsz, s, D_HEADS, D_DV)
    glog = jax.nn.log_sigmoid(g_low.astype(f) @ w_g2.astype(f) + b_g.astype(f)) / D_GATE_NORM
    glog = glog.reshape(bsz, s, D_HEADS, D_DK)
    qc, kc, vc, gc = (to_chunks(t, D_CHUNK) for t in (q, k, v, glog))
    cum = jnp.cumsum(gc, axis=-2)
    qg = qc * jnp.exp(cum)
    kg = kc * jnp.exp(-cum)
    causal = jnp.tril(jnp.ones((D_CHUNK, D_CHUNK), bool))
    a_qk = jnp.where(causal, jnp.einsum("bhncd,bhnkd->bhnck", qg, kg), 0.0)
    o_intra = a_qk @ vc
    ds = jnp.einsum("bhncd,bhncv->bhndv", kc * jnp.exp(cum[..., -1:, :] - cum), vc)
    s_prev = scan_chunk_states(jnp.exp(cum[..., -1, :])[..., None], ds)
    o = jnp.einsum("bhncd,bhndv->bhncv", qg, s_prev) + o_intra
    o = from_chunks(rmsnorm(o, out_g))
    return (o * jax.nn.silu(z.astype(f))).astype(dtype)


def hybrid_layer(x, positions, norm_g, w_in, a_q_norm, a_k_norm, b_conv_w, b_a_log, b_dt_bias, b_out_norm,
                 c_conv_w, c_conv_b, c_a_log, c_dt_bias, c_d_skip, c_out_norm, d_gate_w2, d_gate_b, d_out_norm,
                 w_gate, b_gate, w_branch, w_out):
    h = rmsnorm(x, norm_g)
    u = h @ w_in
    split_points = np.cumsum(IN_SIZES)[:-1].tolist()
    (a_q, a_k, a_v, a_z, b_qkv, b_z, b_beta, b_a, c_xbc, c_z, c_dt,
     d_q, d_k, d_v, d_z, d_g) = jnp.split(u, split_points, axis=-1)
    branches = (
        mixer_dilated(a_q, a_k, a_v, a_z, positions, a_q_norm, a_k_norm),
        mixer_gated_delta(b_qkv, b_z, b_beta, b_a, b_conv_w, b_a_log, b_dt_bias, b_out_norm),
        mixer_ssd(c_xbc, c_z, c_dt, c_conv_w, c_conv_b, c_a_log, c_dt_bias, c_d_skip, c_out_norm),
        mixer_gla(d_q, d_k, d_v, d_z, d_g, d_gate_w2, d_gate_b, d_out_norm),
    )
    merged = sum(jax.nn.sigmoid(h @ w_gate[i] + b_gate[i]) * (o @ w_branch[i]) for i, o in enumerate(branches))
    return x + merged @ w_out


def _inv_softplus_dt(key, shape):
    dt = jnp.exp(jax.random.uniform(key, shape, jnp.float32, math.log(1e-3), math.log(1e-1)))
    return dt + jnp.log(-jnp.expm1(-dt))


def setup_inputs(seed: int = 0) -> dict:
    key = jax.random.key(seed)
    ks = jax.random.split(key, 24)

    def nrm(k, shape, scale):
        return scale * jax.random.normal(k, shape, jnp.float32)

    x = nrm(ks[0], (BATCH, SEQ, D_MODEL), 1.0)
    positions = (jax.random.randint(ks[1], (BATCH, 1), 0, 1024) + jnp.arange(SEQ)[None, :]).astype(jnp.int32)
    return {
        "x": x,
        "positions": positions,
        "norm_g": 1.0 + nrm(ks[2], (DEPTH, D_MODEL), 0.02),
        "w_in": nrm(ks[3], (DEPTH, D_MODEL, D_IN), D_MODEL ** -0.5),
        "a_q_norm": 1.0 + nrm(ks[4], (DEPTH, HEAD_DIM), 0.02),
        "a_k_norm": 1.0 + nrm(ks[5], (DEPTH, HEAD_DIM), 0.02),
        "b_conv_w": nrm(ks[6], (DEPTH, CONV_K, 2 * B_KEY + B_HEADS * B_DV), CONV_K ** -0.5),
        "b_a_log": jnp.log(jax.random.uniform(ks[7], (DEPTH, B_HEADS), jnp.float32, 1.0, 16.0)),
        "b_dt_bias": _inv_softplus_dt(ks[8], (DEPTH, B_HEADS)),
        "b_out_norm": 1.0 + nrm(ks[9], (DEPTH, B_DV), 0.02),
        "c_conv_w": nrm(ks[10], (DEPTH, CONV_K, C_XBC), CONV_K ** -0.5),
        "c_conv_b": nrm(ks[11], (DEPTH, C_XBC), 0.02),
        "c_a_log": jnp.log(jax.random.uniform(ks[12], (DEPTH, C_HEADS), jnp.float32, 1.0, 16.0)),
        "c_dt_bias": _inv_softplus_dt(ks[13], (DEPTH, C_HEADS)),
        "c_d_skip": 1.0 + nrm(ks[14], (DEPTH, C_HEADS), 0.02),
        "c_out_norm": 1.0 + nrm(ks[15], (DEPTH, BRANCH_WIDTH), 0.02),
        "d_gate_w2": nrm(ks[16], (DEPTH, D_GATE_RANK, D_KEY), D_GATE_RANK ** -0.5),
        "d_gate_b": nrm(ks[17], (DEPTH, D_KEY), 0.01),
        "d_out_norm": 1.0 + nrm(ks[18], (DEPTH, D_DV), 0.02),
        "w_gate": nrm(ks[19], (DEPTH, N_BRANCH, D_MODEL, D_MODEL), D_MODEL ** -0.5),
        "b_gate": nrm(ks[20], (DEPTH, N_BRANCH, D_MODEL), 0.01),
        "w_branch": nrm(ks[21], (DEPTH, N_BRANCH, BRANCH_WIDTH, D_MODEL), BRANCH_WIDTH ** -0.5),
        "w_out": nrm(ks[22], (DEPTH, D_MODEL, D_MODEL), D_MODEL ** -0.5),
    }


def reference(x, positions, norm_g, w_in, a_q_norm, a_k_norm, b_conv_w, b_a_log, b_dt_bias, b_out_norm,
              c_conv_w, c_conv_b, c_a_log, c_dt_bias, c_d_skip, c_out_norm, d_gate_w2, d_gate_b, d_out_norm,
              w_gate, b_gate, w_branch, w_out):
    for l in range(DEPTH):
        x = hybrid_layer(x, positions, norm_g[l], w_in[l], a_q_norm[l], a_k_norm[l], b_conv_w[l], b_a_log[l],
                         b_dt_bias[l], b_out_norm[l], c_conv_w[l], c_conv_b[l], c_a_log[l], c_dt_bias[l],
                         c_d_skip[l], c_out_norm[l], d_gate_w2[l], d_gate_b[l], d_out_norm[l],
                         w_gate[l], b_gate[l], w_branch[l], w_out[l])
    return x
```

```python
import functools
import math

import numpy as np
import jax
import jax.numpy as jnp
from jax import lax
from jax.experimental import pallas as pl
from jax.experimental.pallas import tpu as pltpu

F32 = jnp.float32
BF16 = jnp.bfloat16

D_MODEL = 1024
N_BRANCH = 4
BRANCH_WIDTH = D_MODEL // N_BRANCH
HEAD_DIM = 64
N_HEADS = BRANCH_WIDTH // HEAD_DIM
EPS = 1e-6
ROPE_DIM = HEAD_DIM // 4
ROPE_HALF = ROPE_DIM // 2
ROPE_THETA = 500000.0
CONV_K = 4

A_GROUPS = ((128, 1), (512, 4), (2048, 16))
A_N_GROUPS = len(A_GROUPS)
A_QKV = A_N_GROUPS * BRANCH_WIDTH
A_BLOCK = 128

B_CHUNK = 64
C_GROUPS = 2
C_STATE = 128
C_CHUNK = 64
D_HEADS = 4
D_KEY = BRANCH_WIDTH // 2
D_DK = D_KEY // D_HEADS
D_DV = BRANCH_WIDTH // D_HEADS
D_GATE_RANK = 16
D_GATE_NORM = 16.0
D_CHUNK = 32
D_GROUP = 128

LANES = 128
SUBLANES = 8
SMALL_W = LANES
SM_BETA, SM_BDEC, SM_CDT, SM_DG = 0, 4, 8, 12

W_AQ, W_AK, W_AV = 0, A_QKV, 2 * A_QKV
W_Z = 3 * A_QKV
W_BQKV = W_Z + D_MODEL
W_CXBC = W_BQKV + 3 * BRANCH_WIDTH
W_DQKV = W_CXBC + 3 * BRANCH_WIDTH
W_SMALL = W_DQKV + 2 * BRANCH_WIDTH
W_TOTAL = W_SMALL + SMALL_W

VMEM_LIMIT = 56 * 1024 * 1024


def _bf(x):
    return x.astype(BF16)


def _dot(a, b):
    return jnp.dot(a, b, preferred_element_type=F32)


def _dot_nt(a, b):
    return lax.dot_general(a, b, (((1,), (1,)), ((), ())), preferred_element_type=F32)


def _dot_tn(a, b):
    return lax.dot_general(a, b, (((0,), (0,)), ((), ())), preferred_element_type=F32)


def _split3(x):
    x1 = _bf(x)
    r1 = x - x1.astype(F32)
    x2 = _bf(r1)
    x3 = _bf(r1 - x2.astype(F32))
    return x1, x2, x3


def _select_dot(x, sel):
    x1, x2, x3 = _split3(x)
    return _dot(x1, sel) + _dot(x2, sel) + _dot(x3, sel)


def _seg_sum(x, ones_bd):
    hi = _bf(x)
    lo = _bf(x - hi.astype(F32))
    return _dot(hi, ones_bd) + _dot(lo, ones_bd)


def _seg_cumsum(x, seg):
    pos = lax.broadcasted_iota(jnp.int32, x.shape, 0) & (seg - 1)
    step = 1
    while step < seg:
        x = x + jnp.where(pos >= step, pltpu.roll(x, step, axis=0), 0.0)
        step *= 2
    return x


def _silu(x):
    return x * jax.nn.sigmoid(x)


def _softplus(x):
    return jnp.maximum(x, 0.0) + jnp.log1p(jnp.exp(-jnp.abs(x)))


def _shift_rows(x, tail, s):
    r = pltpu.roll(x, s, axis=0)
    t = pltpu.roll(tail, s, axis=0)
    row = lax.broadcasted_iota(jnp.int32, tail.shape, 0)
    first = jnp.where(row < s, t, r[:SUBLANES])
    return jnp.concatenate([first, r[SUBLANES:]], axis=0)


def _causal_conv(x, tail_ref, w_ref):
    tail = tail_ref[...]
    y = x * w_ref[CONV_K - 1:CONV_K, :]
    for s in range(1, CONV_K):
        y = y + _shift_rows(x, tail, s) * w_ref[CONV_K - 1 - s:CONV_K - s, :]
    tail_ref[...] = x[x.shape[0] - SUBLANES:]
    return y


def _stack(y, bdmask):
    return _bf(jnp.concatenate([y] * N_HEADS, axis=0) * bdmask)


def _np_consts():
    r = np.arange(BRANCH_WIDTH)
    bd = (r[:, None] // HEAD_DIM == r[None, :] // HEAD_DIM).astype(np.float32)
    i = np.arange(B_CHUNK)[:, None]
    j = (r % HEAD_DIM)[None, :]
    xor = np.maximum(i ^ j, 1)
    lv = np.where(i > j, np.floor(np.log2(xor)).astype(np.int32), -1).astype(np.int32)
    eye_cat = (i == j).astype(np.float32)
    gmask = ((r[None, :] // C_STATE) == (r[:, None] // HEAD_DIM) // (N_HEADS // C_GROUPS)).astype(np.float32)
    dmask = ((r[:, None] // D_DV) == (np.arange(D_KEY)[None, :] // D_DK)).astype(np.float32)
    t = np.arange(D_GROUP)
    cmask = ((t[:, None] // D_CHUNK == t[None, :] // D_CHUNK) & (t[:, None] >= t[None, :])).astype(np.float32)
    def sel(offset, heads, width):
        m = np.zeros((SMALL_W, heads * width), np.float32)
        for h in range(heads):
            m[offset + h, h * width:(h + 1) * width] = 1.0
        return m
    return dict(bd=bd, lv=lv, eye_cat=eye_cat, gmask=gmask, dmask=dmask, cmask=cmask,
                sel_beta=sel(SM_BETA, N_HEADS, HEAD_DIM), sel_bdec=sel(SM_BDEC, N_HEADS, HEAD_DIM),
                sel_cdt=sel(SM_CDT, N_HEADS, HEAD_DIM))


_NPC = _np_consts()


def _full_spec(shape):
    nd = len(shape)
    return pl.BlockSpec(shape, lambda *_: (0,) * nd)


def _const_spec(shape):
    nd = len(shape)
    return pl.BlockSpec(shape, lambda *_: (0,) * nd, pipeline_mode=pl.Buffered(1))


def _inproj_kernel(x_ref, pos_ref, ng_ref, w_ref, qn_ref, kn_ref, invf_ref, sgn_ref, ones_ref,
                   aq_ref, ak_ref, av_ref, z_ref, bqkv_ref, cxbc_ref, dqkv_ref, small_ref):
    x = x_ref[...]
    ms = jnp.mean(x * x, axis=-1, keepdims=True)
    hb = _bf(x * lax.rsqrt(ms + EPS) * ng_ref[...])

    def proj(lo, width):
        return _dot(hb, w_ref[:, lo:lo + width])

    ang = pos_ref[...].astype(F32) * invf_ref[...]
    cos = jnp.cos(ang)
    sin = jnp.sin(ang) * sgn_ref[...]
    lane = lax.broadcasted_iota(jnp.int32, (1, LANES), 1) & (HEAD_DIM - 1)
    first_half = lane < ROPE_HALF
    ones_bd = ones_ref[...]

    def qk_norm_rope(lo, gain_ref, out_ref, scale):
        for g in range(A_N_GROUPS):
            u = proj(lo + g * BRANCH_WIDTH, BRANCH_WIDTH)
            ss = _seg_sum(u * u, ones_bd)
            gain = gain_ref[:, g * BRANCH_WIDTH:(g + 1) * BRANCH_WIDTH]
            un = u * lax.rsqrt(ss * (1.0 / HEAD_DIM) + EPS) * gain
            for s in range(BRANCH_WIDTH // LANES):
                v = un[:, s * LANES:(s + 1) * LANES]
                partner = jnp.where(first_half, pltpu.roll(v, LANES - ROPE_HALF, axis=1),
                                    pltpu.roll(v, ROPE_HALF, axis=1))
                rot = v * cos + partner * sin
                c0 = g * BRANCH_WIDTH + s * LANES
                out_ref[:, c0:c0 + LANES] = _bf(rot * scale)

    qk_norm_rope(W_AQ, qn_ref, aq_ref, HEAD_DIM ** -0.5)
    qk_norm_rope(W_AK, kn_ref, ak_ref, 1.0)
    av_ref[...] = _bf(proj(W_AV, A_QKV))
    z_ref[...] = proj(W_Z, D_MODEL)
    bqkv_ref[...] = proj(W_BQKV, 3 * BRANCH_WIDTH)
    cxbc_ref[...] = proj(W_CXBC, 3 * BRANCH_WIDTH)
    dqkv_ref[...] = proj(W_DQKV, 2 * BRANCH_WIDTH)
    small_ref[...] = proj(W_SMALL, SMALL_W)


def _inproj(xf, pos, norm_g, w_all, qn, kn, invf, sgn, ones_bd, tm):
    t = xf.shape[0]
    row = lambda w: pl.BlockSpec((tm, w), lambda i: (i, 0))
    outs = [(A_QKV, BF16), (A_QKV, BF16), (A_QKV, BF16), (D_MODEL, F32), (3 * BRANCH_WIDTH, F32),
            (3 * BRANCH_WIDTH, F32), (2 * BRANCH_WIDTH, F32), (SMALL_W, F32)]
    return pl.pallas_call(
        _inproj_kernel,
        grid=(t // tm,),
        in_specs=[row(D_MODEL), row(1), _full_spec((1, D_MODEL)), _const_spec((D_MODEL, W_TOTAL)),
                  _full_spec((1, A_QKV)), _full_spec((1, A_QKV)), _full_spec((1, LANES)),
                  _full_spec((1, LANES)), _full_spec((BRANCH_WIDTH, BRANCH_WIDTH))],
        out_specs=[row(w) for w, _ in outs],
        out_shape=[jax.ShapeDtypeStruct((t, w), dt) for w, dt in outs],
        compiler_params=pltpu.CompilerParams(dimension_semantics=("arbitrary",),
                                             vmem_limit_bytes=VMEM_LIMIT),
        name="inproj",
    )(xf, pos, norm_g, w_all, qn, kn, invf, sgn, ones_bd)


def _attn_kernel(q_ref, k_ref, v_ref, o_ref, lse_ref, *, n, win):
    nb = n // A_BLOCK
    lane = lax.broadcasted_iota(jnp.int32, (1, BRANCH_WIDTH), 1) // HEAD_DIM
    qi = lax.broadcasted_iota(jnp.int32, (A_BLOCK, 2 * A_BLOCK), 0)
    kj = lax.broadcasted_iota(jnp.int32, (A_BLOCK, 2 * A_BLOCK), 1)
    rel = qi - kj

    def block(j, carry):
        q0 = pl.multiple_of(j * A_BLOCK, A_BLOCK)
        k0 = pl.multiple_of(jnp.maximum(j - 1, 0) * A_BLOCK, A_BLOCK)
        q = q_ref[0, pl.ds(q0, A_BLOCK), :]
        kw = k_ref[0, pl.ds(k0, 2 * A_BLOCK), :]
        vw = v_ref[0, pl.ds(k0, 2 * A_BLOCK), :]
        dist = rel + (q0 - k0)
        valid = (dist >= 0) & (dist <= win)
        o_acc = jnp.zeros((A_BLOCK, BRANCH_WIDTH), F32)
        lse_acc = jnp.zeros((A_BLOCK, BRANCH_WIDTH), F32)
        for h in range(N_HEADS):
            hm = lane == h
            s = _dot_nt(jnp.where(hm, q, jnp.zeros_like(q)), kw)
            s = jnp.where(valid, s, -jnp.inf)
            m = jnp.max(s, axis=-1, keepdims=True)
            p = jnp.exp(s - m)
            l = jnp.sum(p, axis=-1, keepdims=True)
            pv = _dot(_bf(p), vw)
            o_acc = jnp.where(hm, pv / l, o_acc)
            lse_acc = jnp.where(hm, m + jnp.log(l), lse_acc)
        o_ref[0, pl.ds(q0, A_BLOCK), :] = o_acc
        lse_ref[0, pl.ds(q0, A_BLOCK), :] = lse_acc
        return carry

    lax.fori_loop(0, nb, block, 0)


def _attn_group(aq, ak, av, bsz, seq, gi):
    window, dil = A_GROUPS[gi]
    n = seq // dil
    assert n % (2 * A_BLOCK) == 0
    view = lambda a: a.reshape(bsz, n, dil * A_QKV)
    in_spec = pl.BlockSpec((1, n, BRANCH_WIDTH), lambda b, r: (b, 0, r * A_N_GROUPS + gi))
    out_spec = pl.BlockSpec((1, n, BRANCH_WIDTH), lambda b, r: (b, 0, r))
    o, lse = pl.pallas_call(
        functools.partial(_attn_kernel, n=n, win=window // dil),
        grid=(bsz, dil),
        in_specs=[in_spec, in_spec, in_spec],
        out_specs=[out_spec, out_spec],
        out_shape=[jax.ShapeDtypeStruct((bsz, n, dil * BRANCH_WIDTH), F32)] * 2,
        compiler_params=pltpu.CompilerParams(dimension_semantics=("arbitrary", "arbitrary"),
                                             vmem_limit_bytes=VMEM_LIMIT),
        name=f"attn_d{dil}",
    )(view(aq), view(ak), view(av))
    return o.reshape(bsz * seq, BRANCH_WIDTH), lse.reshape(bsz * seq, BRANCH_WIDTH)


def _attn_combine_kernel(o0, o1, o2, l0, l1, l2, z_ref, out_ref):
    la, lb, lc = l0[...], l1[...], l2[...]
    m = jnp.maximum(jnp.maximum(la, lb), lc)
    ea, eb, ec = jnp.exp(la - m), jnp.exp(lb - m), jnp.exp(lc - m)
    o = (o0[...] * ea + o1[...] * eb + o2[...] * ec) / (ea + eb + ec)
    out_ref[...] = _bf(o * _silu(z_ref[...]))


def _attn_combine(os_, lses, z_all, tm):
    t = z_all.shape[0]
    row = pl.BlockSpec((tm, BRANCH_WIDTH), lambda i: (i, 0))
    return pl.pallas_call(
        _attn_combine_kernel,
        grid=(t // tm,),
        in_specs=[row] * 7,
        out_specs=row,
        out_shape=jax.ShapeDtypeStruct((t, BRANCH_WIDTH), BF16),
        compiler_params=pltpu.CompilerParams(dimension_semantics=("arbitrary",)),
        name="attn_combine",
    )(*os_, *lses, z_all)


def _gdn_kernel(qkv_ref, z_ref, small_ref, cw_ref, alog_ref, dtb_ref, og_ref,
                ones_ref, bd_ref, lv_ref, eye_ref, selb_ref, seld_ref,
                out_ref,
                tail_ref, st_ref, q_s, k_s, vb_s, kb_s, ce_s, u_s, w_s, qe_s, oi_s, *, tb):
    nchunk = tb // B_CHUNK

    @pl.when(pl.program_id(1) == 0)
    def _():
        tail_ref[...] = jnp.zeros_like(tail_ref)
        st_ref[...] = jnp.zeros_like(st_ref)

    ones_bd = ones_ref[...]
    bdmask = bd_ref[...]
    lv = lv_ref[...]
    eye_cat = eye_ref[...]

    y = _silu(_causal_conv(qkv_ref[0], tail_ref, cw_ref))
    q = y[:, :BRANCH_WIDTH]
    k = y[:, BRANCH_WIDTH:2 * BRANCH_WIDTH]
    v = y[:, 2 * BRANCH_WIDTH:]
    q = q * lax.rsqrt(_seg_sum(q * q, ones_bd) + EPS) * (HEAD_DIM ** -0.5)
    k = k * lax.rsqrt(_seg_sum(k * k, ones_bd) + EPS)
    sm = small_ref[0]
    beta = _select_dot(jax.nn.sigmoid(sm), selb_ref[...])
    g = -jnp.exp(alog_ref[...]) * _softplus(sm + dtb_ref[...])
    ce = _select_dot(_seg_cumsum(g, B_CHUNK), seld_ref[...])
    q_s[...] = q
    k_s[...] = k
    vb_s[...] = v * beta
    kb_s[...] = k * beta
    ce_s[...] = ce

    il = lax.broadcasted_iota(jnp.int32, (B_CHUNK, BRANCH_WIDTH), 0)
    jl = lax.broadcasted_iota(jnp.int32, (B_CHUNK, BRANCH_WIDTH), 1) & (HEAD_DIM - 1)
    causal = il >= jl

    def intra(c):
        r0 = pl.multiple_of(c * B_CHUNK, B_CHUNK)
        rows = pl.ds(r0, B_CHUNK)
        qc, kc, vb, kb, cc = q_s[rows, :], k_s[rows, :], vb_s[rows, :], kb_s[rows, :], ce_s[rows, :]
        crow = jnp.sum(cc * eye_cat, axis=0, keepdims=True)
        gamma = jnp.exp(jnp.where(causal, cc - crow, -jnp.inf))
        kst = _stack(kc, bdmask)
        prod = _dot_nt(_bf(jnp.concatenate([kb, qc], axis=0)), kst)
        a_kk = jnp.where(il > jl, prod[:B_CHUNK] * gamma, 0.0)
        a_qk = prod[B_CHUNK:] * gamma
        t = eye_cat - jnp.where(lv == 0, a_kk, 0.0)
        for lb in range(1, int(math.log2(B_CHUNK))):
            nb_ = jnp.where(lv == lb, a_kk, 0.0)
            x = _dot(_bf(t), _stack(nb_, bdmask))
            t = t - _dot(_bf(x), _stack(t, bdmask))
        ecum = jnp.exp(cc)
        tb16 = _bf(t)
        u = _dot(tb16, _stack(vb, bdmask))
        w = _dot(tb16, _stack(kb * ecum, bdmask))
        aq16 = _bf(a_qk)
        u_s[rows, :] = u
        w_s[rows, :] = w
        qe_s[rows, :] = qc * ecum - _dot(aq16, _stack(w, bdmask))
        oi_s[rows, :] = _dot(aq16, _stack(u, bdmask))

    def intra_pair(i, carry):
        intra(2 * i)
        intra(2 * i + 1)
        return carry

    lax.fori_loop(0, nchunk // 2, intra_pair, 0)

    og = og_ref[...]

    def scan(c, carry):
        r0 = pl.multiple_of(c * B_CHUNK, B_CHUNK)
        rows = pl.ds(r0, B_CHUNK)
        st = st_ref[...]
        st16 = _bf(st)
        cc = ce_s[rows, :]
        clast = cc[B_CHUNK - 1:B_CHUNK, :]
        both = _dot_nt(_bf(jnp.concatenate([w_s[rows, :], qe_s[rows, :]], axis=0)), st16)
        v_new = u_s[rows, :] - both[:B_CHUNK]
        o = both[B_CHUNK:] + oi_s[rows, :]
        k_dec = k_s[rows, :] * jnp.exp(clast - cc)
        st_ref[...] = jnp.exp(clast) * st + bdmask * _dot_tn(_bf(v_new), _bf(k_dec))
        ms = _seg_sum(o * o, ones_bd) * (1.0 / HEAD_DIM)
        on = o * lax.rsqrt(ms + EPS) * og
        out_ref[0, rows, :] = _bf(on * _silu(z_ref[0, rows, :]))
        return carry

    lax.fori_loop(0, nchunk, scan, 0)


def _gdn(bqkv, z_all, small, conv_w, alog_row, dtb_row, og_row, bsz, seq, tb):
    c = _NPC
    seq_blk = lambda w, col: pl.BlockSpec((1, tb, w), lambda b, j: (b, j, col))
    consts = [jnp.asarray(c["bd"], BF16), jnp.asarray(c["bd"]), jnp.asarray(c["lv"]), jnp.asarray(c["eye_cat"]),
              jnp.asarray(c["sel_beta"], BF16), jnp.asarray(c["sel_bdec"], BF16)]
    act = pltpu.VMEM((tb, BRANCH_WIDTH), F32)
    return pl.pallas_call(
        functools.partial(_gdn_kernel, tb=tb),
        grid=(bsz, seq // tb),
        in_specs=[seq_blk(3 * BRANCH_WIDTH, 0), seq_blk(BRANCH_WIDTH, 1), seq_blk(SMALL_W, 0),
                  _full_spec(conv_w.shape), _full_spec((1, SMALL_W)), _full_spec((1, SMALL_W)),
                  _full_spec((1, BRANCH_WIDTH))] + [_full_spec(a.shape) for a in consts],
        out_specs=seq_blk(BRANCH_WIDTH, 0),
        out_shape=jax.ShapeDtypeStruct((bsz, seq, BRANCH_WIDTH), BF16),
        scratch_shapes=[pltpu.VMEM((SUBLANES, 3 * BRANCH_WIDTH), F32),
                        pltpu.VMEM((BRANCH_WIDTH, BRANCH_WIDTH), F32)] + [act] * 9,
        compiler_params=pltpu.CompilerParams(dimension_semantics=("arbitrary", "arbitrary"),
                                             vmem_limit_bytes=VMEM_LIMIT),
        name="gdn",
    )(bqkv.reshape(bsz, seq, -1), z_all.reshape(bsz, seq, -1), small.reshape(bsz, seq, -1),
      conv_w, alog_row, dtb_row, og_row, *consts)


def _ssd_kernel(xbc_ref, z_ref, small_ref, cw_ref, cb_ref, alog_ref, dtb_ref, dskip_ref, og_ref,
                bd_ref, eye_ref, gm_ref, gmt_ref, sel_ref,
                out_ref,
                tail_ref, st_ref, xs_s, bm_s, cm_s, dt_s, ac_s, *, tb):
    nchunk = tb // C_CHUNK

    @pl.when(pl.program_id(1) == 0)
    def _():
        tail_ref[...] = jnp.zeros_like(tail_ref)
        st_ref[...] = jnp.zeros_like(st_ref)

    bdmask = bd_ref[...]
    eye_cat = eye_ref[...]
    gmask = gm_ref[...]
    gmask_t = gmt_ref[...]

    y = _silu(_causal_conv(xbc_ref[0], tail_ref, cw_ref) + cb_ref[...])
    xs_s[...] = y[:, :BRANCH_WIDTH]
    bm_s[...] = y[:, BRANCH_WIDTH:2 * BRANCH_WIDTH]
    cm_s[...] = y[:, 2 * BRANCH_WIDTH:]
    dt = _softplus(small_ref[0] + dtb_ref[...])
    a = -jnp.exp(alog_ref[...]) * dt
    dt_s[...] = _select_dot(dt, sel_ref[...])
    ac_s[...] = _select_dot(_seg_cumsum(a, C_CHUNK), sel_ref[...])

    il = lax.broadcasted_iota(jnp.int32, (C_CHUNK, BRANCH_WIDTH), 0)
    jl = lax.broadcasted_iota(jnp.int32, (C_CHUNK, BRANCH_WIDTH), 1) & (HEAD_DIM - 1)
    causal = il >= jl
    og = og_ref[...]
    dskip = dskip_ref[...]

    def chunk(c, carry):
        r0 = pl.multiple_of(c * C_CHUNK, C_CHUNK)
        rows = pl.ds(r0, C_CHUNK)
        xc, bc, cc, dtc, ac = xs_s[rows, :], bm_s[rows, :], cm_s[rows, :], dt_s[rows, :], ac_s[rows, :]
        st = st_ref[...]
        arow = jnp.sum(ac * eye_cat, axis=0, keepdims=True)
        seg = jnp.exp(jnp.where(causal, ac - arow, -jnp.inf))
        cc16 = _bf(cc)
        rt = _bf(jnp.concatenate([bc] * N_HEADS, axis=0) * gmask)
        scores = _dot_nt(cc16, rt) * seg
        xdt = xc * dtc
        y_diag = _dot(_bf(scores), _stack(xdt, bdmask))
        y_off = _dot(cc16, _bf(st)) * jnp.exp(ac)
        alast = ac[C_CHUNK - 1:C_CHUNK, :]
        xdd = xdt * jnp.exp(alast - ac)
        st_ref[...] = jnp.exp(alast) * st + gmask_t * _dot_tn(_bf(bc), _bf(xdd))
        yv = (y_diag + y_off + dskip * xc) * _silu(z_ref[0, rows, :])
        half = BRANCH_WIDTH // C_GROUPS
        for gi in range(C_GROUPS):
            yg = yv[:, gi * half:(gi + 1) * half]
            ms = jnp.mean(yg * yg, axis=-1, keepdims=True)
            out_ref[0, rows, gi * half:(gi + 1) * half] = _bf(
                yg * lax.rsqrt(ms + EPS) * og[:, gi * half:(gi + 1) * half])
        return carry

    lax.fori_loop(0, nchunk, chunk, 0)


def _ssd(cxbc, z_all, small, conv_w, conv_b, alog_row, dtb_row, dskip_row, og_row, bsz, seq, tb):
    c = _NPC
    seq_blk = lambda w, col: pl.BlockSpec((1, tb, w), lambda b, j: (b, j, col))
    consts = [jnp.asarray(c["bd"]), jnp.asarray(c["eye_cat"]), jnp.asarray(c["gmask"]),
              jnp.asarray(c["gmask"].T.copy()), jnp.asarray(c["sel_cdt"], BF16)]
    act = pltpu.VMEM((tb, BRANCH_WIDTH), F32)
    return pl.pallas_call(
        functools.partial(_ssd_kernel, tb=tb),
        grid=(bsz, seq // tb),
        in_specs=[seq_blk(3 * BRANCH_WIDTH, 0), seq_blk(BRANCH_WIDTH, 2), seq_blk(SMALL_W, 0),
                  _full_spec(conv_w.shape), _full_spec((1, 3 * BRANCH_WIDTH)), _full_spec((1, SMALL_W)),
                  _full_spec((1, SMALL_W)), _full_spec((1, BRANCH_WIDTH)), _full_spec((1, BRANCH_WIDTH))]
                 + [_full_spec(a.shape) for a in consts],
        out_specs=seq_blk(BRANCH_WIDTH, 0),
        out_shape=jax.ShapeDtypeStruct((bsz, seq, BRANCH_WIDTH), BF16),
        scratch_shapes=[pltpu.VMEM((SUBLANES, 3 * BRANCH_WIDTH), F32),
                        pltpu.VMEM((BRANCH_WIDTH, BRANCH_WIDTH), F32)] + [act] * 5,
        compiler_params=pltpu.CompilerParams(dimension_semantics=("arbitrary", "arbitrary"),
                                             vmem_limit_bytes=VMEM_LIMIT),
        name="ssd",
    )(cxbc.reshape(bsz, seq, -1), z_all.reshape(bsz, seq, -1), small.reshape(bsz, seq, -1),
      conv_w, conv_b, alog_row, dtb_row, dskip_row, og_row, *consts)


def _gla_kernel(qkv_ref, z_ref, small_ref, w2_ref, bg_ref, og_ref, ones_ref, dm_ref, cm_ref,
                out_ref,
                st_ref, qg_s, kg_s, k_s, cum_s, oi_s, *, tb):
    @pl.when(pl.program_id(1) == 0)
    def _():
        st_ref[...] = jnp.zeros_like(st_ref)

    qkv = qkv_ref[0]
    q = qkv[:, :D_KEY] * (D_DK ** -0.5)
    k = qkv[:, D_KEY:2 * D_KEY]
    sm = small_ref[0]
    s_hi = _bf(sm)
    s_lo = _bf(sm - s_hi.astype(F32))
    w2 = w2_ref[...]
    w_hi = _bf(w2)
    w_lo = _bf(w2 - w_hi.astype(F32))
    gl = _dot(s_hi, w_hi) + _dot(s_lo, w_hi) + _dot(s_hi, w_lo) + bg_ref[...]
    glog = jax.nn.log_sigmoid(gl) * (1.0 / D_GATE_NORM)
    cum = _seg_cumsum(glog, D_CHUNK)
    qg = q * jnp.exp(cum)
    kg = k * jnp.exp(-cum)
    qg_s[...] = qg
    kg_s[...] = kg
    k_s[...] = k
    cum_s[...] = cum

    cmask = cm_ref[...] > 0.0
    klane = lax.broadcasted_iota(jnp.int32, (1, D_KEY), 1) // D_DK
    vlane = lax.broadcasted_iota(jnp.int32, (1, BRANCH_WIDTH), 1) // D_DV

    def intra(gidx, carry):
        r0 = pl.multiple_of(gidx * D_GROUP, D_GROUP)
        rows = pl.ds(r0, D_GROUP)
        qgg, kgg = qg_s[rows, :], kg_s[rows, :]
        vg = _bf(qkv_ref[0, rows, 2 * D_KEY:])
        kg16 = _bf(kgg)
        acc = jnp.zeros((D_GROUP, BRANCH_WIDTH), F32)
        for h in range(D_HEADS):
            a = _dot_nt(_bf(jnp.where(klane == h, qgg, 0.0)), kg16)
            a = jnp.where(cmask, a, 0.0)
            acc = jnp.where(vlane == h, _dot(_bf(a), vg), acc)
        oi_s[rows, :] = acc
        return carry

    lax.fori_loop(0, tb // D_GROUP, intra, 0)

    dmask = dm_ref[...]
    ones_bd = ones_ref[...]
    og = og_ref[...]

    def scan(c, carry):
        r0 = pl.multiple_of(c * D_CHUNK, D_CHUNK)
        rows = pl.ds(r0, D_CHUNK)
        st = st_ref[...]
        cc = cum_s[rows, :]
        clast = cc[D_CHUNK - 1:D_CHUNK, :]
        o = _dot_nt(_bf(qg_s[rows, :]), _bf(st)) + oi_s[rows, :]
        kd = k_s[rows, :] * jnp.exp(clast - cc)
        vc = qkv_ref[0, rows, 2 * D_KEY:]
        st_ref[...] = jnp.exp(clast) * st + dmask * _dot_tn(_bf(vc), _bf(kd))
        ms = _seg_sum(o * o, ones_bd) * (1.0 / D_DV)
        on = o * lax.rsqrt(ms + EPS) * og
        out_ref[0, rows, :] = _bf(on * _silu(z_ref[0, rows, :]))
        return carry

    lax.fori_loop(0, tb // D_CHUNK, scan, 0)


def _gla(dqkv, z_all, small, w2pad, bg_row, og_row, bsz, seq, tb):
    c = _NPC
    seq_blk = lambda w, col: pl.BlockSpec((1, tb, w), lambda b, j: (b, j, col))
    consts = [jnp.asarray(c["bd"], BF16), jnp.asarray(c["dmask"]), jnp.asarray(c["cmask"])]
    return pl.pallas_call(
        functools.partial(_gla_kernel, tb=tb),
        grid=(bsz, seq // tb),
        in_specs=[seq_blk(2 * BRANCH_WIDTH, 0), seq_blk(BRANCH_WIDTH, 3), seq_blk(SMALL_W, 0),
                  _full_spec((SMALL_W, D_KEY)), _full_spec((1, D_KEY)), _full_spec((1, BRANCH_WIDTH))]
                 + [_full_spec(a.shape) for a in consts],
        out_specs=seq_blk(BRANCH_WIDTH, 0),
        out_shape=jax.ShapeDtypeStruct((bsz, seq, BRANCH_WIDTH), BF16),
        scratch_shapes=[pltpu.VMEM((BRANCH_WIDTH, D_KEY), F32)]
                       + [pltpu.VMEM((tb, D_KEY), F32)] * 4 + [pltpu.VMEM((tb, BRANCH_WIDTH), F32)],
        compiler_params=pltpu.CompilerParams(dimension_semantics=("arbitrary", "arbitrary"),
                                             vmem_limit_bytes=VMEM_LIMIT),
        name="gla",
    )(dqkv.reshape(bsz, seq, -1), z_all.reshape(bsz, seq, -1), small.reshape(bsz, seq, -1),
      w2pad, bg_row, og_row, *consts)


def _merge_kernel(x_ref, ng_ref, br0, br1, br2, br3, wg_ref, bg_ref, wb_ref, wo_ref, out_ref):
    x = x_ref[...]
    ms = jnp.mean(x * x, axis=-1, keepdims=True)
    hb = _bf(x * lax.rsqrt(ms + EPS) * ng_ref[...])
    merged = jnp.zeros(x.shape, F32)
    for i, br in enumerate((br0, br1, br2, br3)):
        gate = jax.nn.sigmoid(_dot(hb, wg_ref[i]) + bg_ref[i])
        merged = merged + gate * _dot(br[...], wb_ref[i])
    out_ref[...] = x + _dot(_bf(merged), wo_ref[...])


def _merge(xf, norm_g, brs, w_gate, b_gate, w_branch, w_out, tm):
    t = xf.shape[0]
    row = lambda w: pl.BlockSpec((tm, w), lambda i: (i, 0))
    return pl.pallas_call(
        _merge_kernel,
        grid=(t // tm,),
        in_specs=[row(D_MODEL), _full_spec((1, D_MODEL))] + [row(BRANCH_WIDTH)] * N_BRANCH
                 + [_const_spec(w_gate.shape), _full_spec(b_gate.shape), _const_spec(w_branch.shape),
                    _const_spec(w_out.shape)],
        out_specs=row(D_MODEL),
        out_shape=jax.ShapeDtypeStruct((t, D_MODEL), F32),
        compiler_params=pltpu.CompilerParams(dimension_semantics=("arbitrary",),
                                             vmem_limit_bytes=VMEM_LIMIT),
        name="merge",
    )(xf, norm_g, *brs, w_gate, b_gate, w_branch, w_out)


def _lane_row(vals, offset, width):
    return jnp.zeros((1, width), F32).at[0, offset:offset + vals.shape[0]].set(vals.astype(F32))


def _rope_rows():
    lane = np.arange(LANES) % HEAD_DIM
    inv_freq = ROPE_THETA ** (-jnp.arange(ROPE_HALF, dtype=F32) / ROPE_HALF)
    invf = jnp.where(lane < ROPE_DIM, inv_freq[lane % ROPE_HALF], 0.0).astype(F32)[None, :]
    sgn = np.where(lane < ROPE_HALF, -1.0, np.where(lane < ROPE_DIM, 1.0, 0.0)).astype(np.float32)[None, :]
    return invf, jnp.asarray(sgn)


def _pack_w_in(w_in):
    sizes = (A_QKV, A_QKV, A_QKV, BRANCH_WIDTH, 3 * BRANCH_WIDTH, BRANCH_WIDTH, N_HEADS, N_HEADS,
             3 * BRANCH_WIDTH, BRANCH_WIDTH, N_HEADS, D_KEY, D_KEY, BRANCH_WIDTH, BRANCH_WIDTH, D_GATE_RANK)
    offs = np.concatenate([[0], np.cumsum(sizes)])
    part = lambda i: w_in[:, offs[i]:offs[i + 1]]
    (a_q, a_k, a_v, a_z, b_qkv, b_z, b_beta, b_a, c_xbc, c_z, c_dt, d_q, d_k, d_v, d_z, d_g) = (
        part(i) for i in range(len(sizes)))
    small = jnp.zeros((D_MODEL, SMALL_W), w_in.dtype)
    small = small.at[:, SM_BETA:SM_BETA + N_HEADS].set(b_beta)
    small = small.at[:, SM_BDEC:SM_BDEC + N_HEADS].set(b_a)
    small = small.at[:, SM_CDT:SM_CDT + N_HEADS].set(c_dt)
    small = small.at[:, SM_DG:SM_DG + D_GATE_RANK].set(d_g)
    w_all = jnp.concatenate([a_q, a_k, a_v, a_z, b_z, c_z, d_z, b_qkv, c_xbc, d_q, d_k, d_v, small], axis=1)
    assert w_all.shape[1] == W_TOTAL
    return _bf(w_all)


def _layer(xf, pos, bsz, seq, p, tm, tb):
    invf, sgn = _rope_rows()
    ones_bd = jnp.asarray(_NPC["bd"], BF16)
    w_all = _pack_w_in(p["w_in"])
    ng = p["norm_g"][None, :]
    qn = jnp.tile(p["a_q_norm"], A_QKV // HEAD_DIM)[None, :]
    kn = jnp.tile(p["a_k_norm"], A_QKV // HEAD_DIM)[None, :]
    aq, ak, av, z_all, bqkv, cxbc, dqkv, small = _inproj(xf, pos, ng, w_all, qn, kn, invf, sgn, ones_bd, tm)

    os_, lses = [], []
    for gi in range(A_N_GROUPS):
        o, lse = _attn_group(aq, ak, av, bsz, seq, gi)
        os_.append(o)
        lses.append(lse)
    br_a = _attn_combine(os_, lses, z_all, tm)

    br_b = _gdn(bqkv, z_all, small, p["b_conv_w"],
                _lane_row(p["b_a_log"], SM_BDEC, SMALL_W), _lane_row(p["b_dt_bias"], SM_BDEC, SMALL_W),
                jnp.tile(p["b_out_norm"], N_HEADS)[None, :], bsz, seq, tb).reshape(bsz * seq, BRANCH_WIDTH)

    br_c = _ssd(cxbc, z_all, small, p["c_conv_w"], p["c_conv_b"][None, :],
                _lane_row(p["c_a_log"], SM_CDT, SMALL_W), _lane_row(p["c_dt_bias"], SM_CDT, SMALL_W),
                jnp.repeat(p["c_d_skip"], HEAD_DIM)[None, :], p["c_out_norm"][None, :],
                bsz, seq, tb).reshape(bsz * seq, BRANCH_WIDTH)

    w2pad = jnp.zeros((SMALL_W, D_KEY), F32).at[SM_DG:SM_DG + D_GATE_RANK, :].set(p["d_gate_w2"])
    br_d = _gla(dqkv, z_all, small, w2pad, p["d_gate_b"][None, :],
                jnp.tile(p["d_out_norm"], D_HEADS)[None, :], bsz, seq, tb).reshape(bsz * seq, BRANCH_WIDTH)

    return _merge(xf, ng, (br_a, br_b, br_c, br_d), _bf(p["w_gate"]), p["b_gate"][:, None, :],
                  _bf(p["w_branch"]), _bf(p["w_out"]), tm)


def _pick_tile(n, target):
    t = min(n, target)
    while n % t:
        t //= 2
    return t


def kernel(x, positions, norm_g, w_in, a_q_norm, a_k_norm, b_conv_w, b_a_log, b_dt_bias, b_out_norm,
           c_conv_w, c_conv_b, c_a_log, c_dt_bias, c_d_skip, c_out_norm, d_gate_w2, d_gate_b, d_out_norm,
           w_gate, b_gate, w_branch, w_out):
    bsz, seq, d_model = x.shape
    assert d_model == D_MODEL and seq % (2 * A_BLOCK * A_GROUPS[-1][1]) == 0
    xf = x.reshape(bsz * seq, d_model)
    pos = positions.reshape(bsz * seq, 1).astype(jnp.int32)
    tm = _pick_tile(bsz * seq, 512)
    tb = _pick_tile(seq, 512)
    names = ("norm_g", "w_in", "a_q_norm", "a_k_norm", "b_conv_w", "b_a_log", "b_dt_bias", "b_out_norm",
             "c_conv_w", "c_conv_b", "c_a_log", "c_dt_bias", "c_d_skip", "c_out_norm", "d_gate_w2",
             "d_gate_b", "d_out_norm", "w_gate", "b_gate", "w_branch", "w_out")
    stacked = (norm_g, w_in, a_q_norm, a_k_norm, b_conv_w, b_a_log, b_dt_bias, b_out_norm,
               c_conv_w, c_conv_b, c_a_log, c_dt_bias, c_d_skip, c_out_norm, d_gate_w2, d_gate_b,
               d_out_norm, w_gate, b_gate, w_branch, w_out)
    for layer in range(norm_g.shape[0]):
        p = {n: a[layer] for n, a in zip(names, stacked)}
        xf = _layer(xf, pos, bsz, seq, p, tm, tb)
    return xf.reshape(bsz, seq, d_model)
```

```python
import functools
import math

import numpy as np
import jax
import jax.numpy as jnp
from jax import lax
from jax.experimental import pallas as pl
from jax.experimental.pallas import tpu as pltpu

F32 = jnp.float32
BF16 = jnp.bfloat16

D_MODEL = 1024
N_BRANCH = 4
BRANCH_WIDTH = D_MODEL // N_BRANCH
HEAD_DIM = 64
N_HEADS = BRANCH_WIDTH // HEAD_DIM
EPS = 1e-6
ROPE_DIM = HEAD_DIM // 4
ROPE_HALF = ROPE_DIM // 2
ROPE_THETA = 500000.0
CONV_K = 4

A_GROUPS = ((128, 1), (512, 4), (2048, 16))
A_N_GROUPS = len(A_GROUPS)
A_QKV = A_N_GROUPS * BRANCH_WIDTH
A_BLOCK = 128
A_STEP_ROWS = 1024

B_CHUNK = 64
B_UNROLL = 4
C_GROUPS = 2
C_STATE = 128
C_CHUNK = 64
C_UNROLL = 4
D_HEADS = 4
D_KEY = BRANCH_WIDTH // 2
D_DK = D_KEY // D_HEADS
D_DV = BRANCH_WIDTH // D_HEADS
D_GATE_RANK = 16
D_GATE_NORM = 16.0
D_CHUNK = 32
D_GROUP = 128

LANES = 128
SUBLANES = 8
SLABS = BRANCH_WIDTH // LANES
LSE_W = LANES // N_HEADS
SMALL_W = LANES
SM_BETA, SM_BDEC, SM_CDT, SM_DG = 0, 4, 8, 12

W_AQ, W_AK, W_AV = 0, A_QKV, 2 * A_QKV
W_Z = 3 * A_QKV
W_BQKV = W_Z + D_MODEL
W_CXBC = W_BQKV + 3 * BRANCH_WIDTH
W_DQKV = W_CXBC + 3 * BRANCH_WIDTH
W_SMALL = W_DQKV + 2 * BRANCH_WIDTH
W_TOTAL = W_SMALL + SMALL_W

VMEM_LIMIT = 56 * 1024 * 1024


def _bf(x):
    return x.astype(BF16)


def _dot(a, b):
    return jnp.dot(a, b, preferred_element_type=F32)


def _dot_nt(a, b):
    return lax.dot_general(a, b, (((1,), (1,)), ((), ())), preferred_element_type=F32)


def _dot_tn(a, b):
    return lax.dot_general(a, b, (((0,), (0,)), ((), ())), preferred_element_type=F32)


def _split3(x):
    x1 = _bf(x)
    r1 = x - x1.astype(F32)
    x2 = _bf(r1)
    x3 = _bf(r1 - x2.astype(F32))
    return x1, x2, x3


def _select_dot(x, sel):
    x1, x2, x3 = _split3(x)
    return _dot(x1, sel) + _dot(x2, sel) + _dot(x3, sel)


def _seg_sum(x, ones_bd):
    hi = _bf(x)
    lo = _bf(x - hi.astype(F32))
    return _dot(hi, ones_bd) + _dot(lo, ones_bd)


def _seg_cumsum(x, seg):
    pos = lax.broadcasted_iota(jnp.int32, x.shape, 0) & (seg - 1)
    step = 1
    while step < seg:
        x = x + jnp.where(pos >= step, pltpu.roll(x, step, axis=0), 0.0)
        step *= 2
    return x


def _silu(x):
    return x * jax.nn.sigmoid(x)


def _softplus(x):
    return jnp.maximum(x, 0.0) + jnp.log1p(jnp.exp(-jnp.abs(x)))


def _shift_rows(x, tail, s):
    r = pltpu.roll(x, s, axis=0)
    t = pltpu.roll(tail, s, axis=0)
    row = lax.broadcasted_iota(jnp.int32, tail.shape, 0)
    first = jnp.where(row < s, t, r[:SUBLANES])
    return jnp.concatenate([first, r[SUBLANES:]], axis=0)


def _causal_conv(x, tail, w):
    y = x * w[CONV_K - 1:CONV_K, :]
    for s in range(1, CONV_K):
        y = y + _shift_rows(x, tail, s) * w[CONV_K - 1 - s:CONV_K - s, :]
    return y


def _stack(y, bdmask16):
    return jnp.concatenate([_bf(y)] * N_HEADS, axis=0) * bdmask16


def _np_consts():
    r = np.arange(BRANCH_WIDTH)
    bd = (r[:, None] // HEAD_DIM == r[None, :] // HEAD_DIM).astype(np.float32)
    i = np.arange(B_CHUNK)[:, None]
    j = (r % HEAD_DIM)[None, :]
    xor = np.maximum(i ^ j, 1)
    lv = np.where(i > j, np.floor(np.log2(xor)).astype(np.int32), -1).astype(np.int32)
    eye_cat = (i == j).astype(np.float32)
    gmask = ((r[None, :] // C_STATE) == (r[:, None] // HEAD_DIM) // (N_HEADS // C_GROUPS)).astype(np.float32)
    dmask = ((r[:, None] // D_DV) == (np.arange(D_KEY)[None, :] // D_DK)).astype(np.float32)
    t = np.arange(D_GROUP)
    cmask = ((t[:, None] // D_CHUNK == t[None, :] // D_CHUNK) & (t[:, None] >= t[None, :])).astype(np.float32)

    def sel(offset, heads, width):
        m = np.zeros((SMALL_W, heads * width), np.float32)
        for h in range(heads):
            m[offset + h, h * width:(h + 1) * width] = 1.0
        return m
    return dict(bd=bd, lv=lv, eye_cat=eye_cat, gmask=gmask, dmask=dmask, cmask=cmask,
                sel_beta=sel(SM_BETA, N_HEADS, HEAD_DIM), sel_bdec=sel(SM_BDEC, N_HEADS, HEAD_DIM),
                sel_cdt=sel(SM_CDT, N_HEADS, HEAD_DIM))


_NPC = _np_consts()


def _full_spec(shape):
    nd = len(shape)
    return pl.BlockSpec(shape, lambda *_: (0,) * nd)


def _const_spec(shape):
    nd = len(shape)
    return pl.BlockSpec(shape, lambda *_: (0,) * nd, pipeline_mode=pl.Buffered(1))


def _inproj_kernel(x_ref, pos_ref, ng_ref, w_ref, qn_ref, kn_ref, invf_ref, sgn_ref, ones_ref,
                   q0_ref, q1_ref, q2_ref, k0_ref, k1_ref, k2_ref, v0_ref, v1_ref, v2_ref,
                   z_ref, bqkv_ref, cxbc_ref, dqkv_ref, small_ref, perm_s, *, tm):
    x = x_ref[...]
    ms = jnp.mean(x * x, axis=-1, keepdims=True)
    hb = _bf(x * lax.rsqrt(ms + EPS) * ng_ref[...])

    def proj(lo, width):
        return _dot(hb, w_ref[:, lo:lo + width])

    pos_f = pos_ref[0].astype(F32)
    pos_col = jnp.concatenate([jnp.broadcast_to(pos_f[k:k + 1, :], (LANES, LANES)).T
                               for k in range(tm // LANES)], axis=0)
    ang = pos_col * invf_ref[...]
    cos = jnp.cos(ang)
    sin = jnp.sin(ang) * sgn_ref[...]
    lane = lax.broadcasted_iota(jnp.int32, (1, LANES), 1) & (HEAD_DIM - 1)
    first_half = lane < ROPE_HALF
    ones_bd = ones_ref[...]

    def emit(slabs, out_ref, dil):
        for s, v in enumerate(slabs):
            if dil == 1:
                out_ref[0, 0, :, s * LANES:(s + 1) * LANES] = _bf(v)
            else:
                perm_s[s] = v
        if dil > 1:
            rows = tm // dil
            for s in range(len(slabs)):
                for r in range(dil):
                    out_ref[0, r, :, s * LANES:(s + 1) * LANES] = _bf(perm_s[s, pl.ds(r, rows, stride=dil), :])

    def qk_norm_rope(lo, gain_ref, out_refs, scale):
        for g in range(A_N_GROUPS):
            u = proj(lo + g * BRANCH_WIDTH, BRANCH_WIDTH)
            ss = _seg_sum(u * u, ones_bd)
            gain = gain_ref[:, g * BRANCH_WIDTH:(g + 1) * BRANCH_WIDTH]
            un = u * lax.rsqrt(ss * (1.0 / HEAD_DIM) + EPS) * (gain * scale)
            slabs = []
            for s in range(SLABS):
                v = un[:, s * LANES:(s + 1) * LANES]
                partner = jnp.where(first_half, pltpu.roll(v, LANES - ROPE_HALF, axis=1),
                                    pltpu.roll(v, ROPE_HALF, axis=1))
                slabs.append(v * cos + partner * sin)
            emit(slabs, out_refs[g], A_GROUPS[g][1])

    qk_norm_rope(W_AQ, qn_ref, (q0_ref, q1_ref, q2_ref), HEAD_DIM ** -0.5)
    qk_norm_rope(W_AK, kn_ref, (k0_ref, k1_ref, k2_ref), 1.0)
    for g, v_ref in enumerate((v0_ref, v1_ref, v2_ref)):
        u = proj(W_AV + g * BRANCH_WIDTH, BRANCH_WIDTH)
        emit([u[:, s * LANES:(s + 1) * LANES] for s in range(SLABS)], v_ref, A_GROUPS[g][1])
    z_ref[...] = proj(W_Z, D_MODEL)
    bqkv_ref[...] = proj(W_BQKV, 3 * BRANCH_WIDTH)
    cxbc_ref[...] = proj(W_CXBC, 3 * BRANCH_WIDTH)
    dqkv_ref[...] = proj(W_DQKV, 2 * BRANCH_WIDTH)
    small_ref[...] = proj(W_SMALL, SMALL_W)


def _inproj(xf, pos, norm_g, w_all, qn, kn, invf, sgn, ones_bd, bsz, seq, tm):
    t = xf.shape[0]
    tps = seq // tm
    row = lambda w: pl.BlockSpec((tm, w), lambda i: (i, 0))
    dils = [d for _, d in A_GROUPS]
    res_spec = lambda d: pl.BlockSpec((1, d, tm // d, BRANCH_WIDTH), lambda i: (i // tps, 0, i % tps, 0))
    res_shape = lambda d: jax.ShapeDtypeStruct((bsz, d, seq // d, BRANCH_WIDTH), BF16)
    flat = [(D_MODEL, F32), (3 * BRANCH_WIDTH, F32), (3 * BRANCH_WIDTH, F32), (2 * BRANCH_WIDTH, F32),
            (SMALL_W, F32)]
    return pl.pallas_call(
        functools.partial(_inproj_kernel, tm=tm),
        grid=(t // tm,),
        in_specs=[row(D_MODEL), pl.BlockSpec((1, tm // LANES, LANES), lambda i: (i, 0, 0)),
                  _full_spec((1, D_MODEL)), _const_spec((D_MODEL, W_TOTAL)),
                  _full_spec((1, A_QKV)), _full_spec((1, A_QKV)), _full_spec((1, LANES)),
                  _full_spec((1, LANES)), _full_spec((BRANCH_WIDTH, BRANCH_WIDTH))],
        out_specs=[res_spec(d) for d in dils] * 3 + [row(w) for w, _ in flat],
        out_shape=[res_shape(d) for d in dils] * 3 + [jax.ShapeDtypeStruct((t, w), dt) for w, dt in flat],
        scratch_shapes=[pltpu.VMEM((SLABS, tm, LANES), F32)],
        compiler_params=pltpu.CompilerParams(dimension_semantics=("arbitrary",),
                                             vmem_limit_bytes=VMEM_LIMIT),
        name="inproj",
    )(xf, pos, norm_g, w_all, qn, kn, invf, sgn, ones_bd)


def _attn_kernel(q_ref, k_ref, v_ref, o_ref, lse_ref, *, n, rb, win):
    nb = n // A_BLOCK
    rows4 = N_HEADS * A_BLOCK
    head_lane = lax.broadcasted_iota(jnp.int32, (1, BRANCH_WIDTH), 1) // HEAD_DIM
    lse_lane = lax.broadcasted_iota(jnp.int32, (1, LANES), 1) // LSE_W
    qi = lax.broadcasted_iota(jnp.int32, (rows4, 2 * A_BLOCK), 0) & (A_BLOCK - 1)
    kj = lax.broadcasted_iota(jnp.int32, (rows4, 2 * A_BLOCK), 1)
    rel = qi - kj

    def one_block(rr, j):
        q0 = pl.multiple_of(j * A_BLOCK, A_BLOCK)
        k0 = pl.multiple_of(jnp.maximum(j - 1, 0) * A_BLOCK, A_BLOCK)
        q = q_ref[0, rr, pl.ds(q0, A_BLOCK), :]
        kw = k_ref[0, rr, pl.ds(k0, 2 * A_BLOCK), :]
        vw = v_ref[0, rr, pl.ds(k0, 2 * A_BLOCK), :]
        dist = rel + (q0 - k0)
        valid = (dist >= 0) & (dist <= win)
        zero = jnp.zeros_like(q)
        qs = jnp.concatenate([jnp.where(head_lane == h, q, zero) for h in range(N_HEADS)], axis=0)
        s = jnp.where(valid, _dot_nt(qs, kw), -jnp.inf)
        m = jnp.max(s, axis=-1, keepdims=True)
        p = jnp.exp(s - m)
        l = jnp.sum(p, axis=-1, keepdims=True)
        pv = _dot(_bf(p), vw) * (1.0 / l)
        lse = m + jnp.log(l)
        o_acc = pv[:A_BLOCK]
        lse_acc = jnp.broadcast_to(lse[:A_BLOCK], (A_BLOCK, LANES))
        for h in range(1, N_HEADS):
            o_acc = jnp.where(head_lane == h, pv[h * A_BLOCK:(h + 1) * A_BLOCK], o_acc)
            lse_acc = jnp.where(lse_lane == h, lse[h * A_BLOCK:(h + 1) * A_BLOCK], lse_acc)
        o_ref[0, rr, pl.ds(q0, A_BLOCK), :] = _bf(o_acc)
        lse_ref[0, rr, pl.ds(q0, A_BLOCK), :] = lse_acc

    half = nb // 2

    def pair(it, carry):
        rr = it // half
        jp = it - rr * half
        one_block(rr, 2 * jp)
        one_block(rr, 2 * jp + 1)
        return carry

    lax.fori_loop(0, rb * half, pair, 0)


def _attn_group(q, k, v, gi):
    window, dil = A_GROUPS[gi]
    bsz, _, n, _ = q.shape
    assert n % (2 * A_BLOCK) == 0
    rb = min(dil, max(1, A_STEP_ROWS // n))
    blk = lambda w: pl.BlockSpec((1, rb, n, w), lambda b, r: (b, r, 0, 0))
    return pl.pallas_call(
        functools.partial(_attn_kernel, n=n, rb=rb, win=window // dil),
        grid=(bsz, dil // rb),
        in_specs=[blk(BRANCH_WIDTH)] * 3,
        out_specs=[blk(BRANCH_WIDTH), blk(LANES)],
        out_shape=[jax.ShapeDtypeStruct((bsz, dil, n, BRANCH_WIDTH), BF16),
                   jax.ShapeDtypeStruct((bsz, dil, n, LANES), F32)],
        compiler_params=pltpu.CompilerParams(dimension_semantics=("arbitrary", "arbitrary"),
                                             vmem_limit_bytes=VMEM_LIMIT),
        name=f"attn_d{dil}",
    )(q, k, v)


def _attn_combine_kernel(o0, o1, o2, l0, l1, l2, z_ref, out_ref, lse_s, o_s, *, tm):
    dils = [d for _, d in A_GROUPS]
    lane = lax.broadcasted_iota(jnp.int32, (1, LANES), 1)

    def token_major(ref, dil, scratch, cast):
        if dil == 1:
            return cast(ref())
        rows = tm // dil
        for r in range(dil):
            scratch[pl.ds(r, rows, stride=dil), :] = cast(ref(r))
        return scratch[...]

    lses = []
    for g, (lref, dil) in enumerate(zip((l0, l1, l2), dils)):
        get = (lambda lref: (lambda r=0: lref[0, r]))(lref)
        lses.append(token_major(get, dil, lse_s.at[g], lambda a: a))
    m = jnp.maximum(jnp.maximum(lses[0], lses[1]), lses[2])
    es = [jnp.exp(l - m) for l in lses]
    inv = 1.0 / (es[0] + es[1] + es[2])
    ws = [e * inv for e in es]

    for s in range(SLABS):
        acc = jnp.zeros((tm, LANES), F32)
        for g, (oref, dil) in enumerate(zip((o0, o1, o2), dils)):
            get = (lambda oref: (lambda r=0: oref[0, r, :, s * LANES:(s + 1) * LANES]))(oref)
            og = token_major(get, dil, o_s, lambda a: a.astype(F32))
            w = ws[g]
            src = pltpu.roll(w, (LANES - 2 * s * LSE_W) % LANES, axis=1) if s else w
            wide = jnp.where(lane < LSE_W, src,
                             jnp.where(lane < 3 * LSE_W, pltpu.roll(src, LSE_W, axis=1),
                                       pltpu.roll(src, 2 * LSE_W, axis=1)))
            acc = acc + og * wide
        zs = z_ref[:, s * LANES:(s + 1) * LANES]
        out_ref[:, s * LANES:(s + 1) * LANES] = _bf(acc * _silu(zs))


def _attn_combine(os_, lses, z_all, bsz, seq, tm):
    t = z_all.shape[0]
    tps = seq // tm
    dils = [d for _, d in A_GROUPS]
    res = lambda d, w: pl.BlockSpec((1, d, tm // d, w), lambda i: (i // tps, 0, i % tps, 0))
    row = pl.BlockSpec((tm, BRANCH_WIDTH), lambda i: (i, 0))
    return pl.pallas_call(
        functools.partial(_attn_combine_kernel, tm=tm),
        grid=(t // tm,),
        in_specs=[res(d, BRANCH_WIDTH) for d in dils] + [res(d, LANES) for d in dils] + [row],
        out_specs=row,
        out_shape=jax.ShapeDtypeStruct((t, BRANCH_WIDTH), BF16),
        scratch_shapes=[pltpu.VMEM((A_N_GROUPS, tm, LANES), F32), pltpu.VMEM((tm, LANES), F32)],
        compiler_params=pltpu.CompilerParams(dimension_semantics=("arbitrary",)),
        name="attn_combine",
    )(*os_, *lses, z_all)


def _gdn_kernel(qkv_ref, z_ref, small_ref, cw_ref, alog_ref, dtb_ref, og_ref,
                ones_ref, bd_ref, lv_ref, eye_ref, selb_ref, seld_ref,
                out_ref,
                tail_ref, st_ref, q_s, k_s, vb_s, kb_s, ce_s, qe_s, oi_s, corr_s, add_s, sts_s, *, tb):
    nchunk = tb // B_CHUNK

    @pl.when(pl.program_id(1) == 0)
    def _():
        tail_ref[...] = jnp.zeros_like(tail_ref)
        st_ref[...] = jnp.zeros_like(st_ref)

    ones_bd = ones_ref[...]
    bd16 = ones_bd
    bdmask = bd_ref[...]
    lv = lv_ref[...]
    eye_cat = eye_ref[...]

    sm = small_ref[0]
    beta = _select_dot(jax.nn.sigmoid(sm), selb_ref[...])
    g = -jnp.exp(alog_ref[...]) * _softplus(sm + dtb_ref[...])
    ce_s[...] = _select_dot(_seg_cumsum(g, B_CHUNK), seld_ref[...])
    tail = tail_ref[...]
    for part in range(3):
        cols = slice(part * BRANCH_WIDTH, (part + 1) * BRANCH_WIDTH)
        y = _silu(_causal_conv(qkv_ref[0, :, cols], tail[:, cols], cw_ref[:, cols]))
        if part == 0:
            q_s[...] = y * lax.rsqrt(_seg_sum(y * y, ones_bd) + EPS) * (HEAD_DIM ** -0.5)
        elif part == 1:
            kn = y * lax.rsqrt(_seg_sum(y * y, ones_bd) + EPS)
            k_s[...] = kn
            kb_s[...] = kn * beta
        else:
            vb_s[...] = y * beta
    tail_ref[...] = qkv_ref[0, tb - SUBLANES:, :]

    il = lax.broadcasted_iota(jnp.int32, (B_CHUNK, BRANCH_WIDTH), 0)
    jl = lax.broadcasted_iota(jnp.int32, (B_CHUNK, BRANCH_WIDTH), 1) & (HEAD_DIM - 1)
    causal = il >= jl

    def intra(c):
        r0 = pl.multiple_of(c * B_CHUNK, B_CHUNK)
        rows = pl.ds(r0, B_CHUNK)
        qc, kc, vb, kb, cc = q_s[rows, :], k_s[rows, :], vb_s[rows, :], kb_s[rows, :], ce_s[rows, :]
        crow = jnp.sum(cc * eye_cat, axis=0, keepdims=True)
        gamma = jnp.exp(jnp.where(causal, cc - crow, -jnp.inf))
        prod = _dot_nt(_bf(jnp.concatenate([kb, qc], axis=0)), _stack(kc, bd16))
        a_kk = jnp.where(il > jl, prod[:B_CHUNK] * gamma, 0.0)
        a_qk = prod[B_CHUNK:] * gamma
        t = eye_cat - jnp.where(lv == 0, a_kk, 0.0)
        for lb in range(1, int(math.log2(B_CHUNK))):
            nb_ = jnp.where(lv == lb, a_kk, 0.0)
            x = _dot(_bf(t), _stack(nb_, bd16))
            t = t - _dot(_bf(x), _stack(t, bd16))
        ecum = jnp.exp(cc)
        clast = cc[B_CHUNK - 1:B_CHUNK, :]
        uw = _dot(_bf(t), jnp.concatenate([_stack(vb, bd16), _stack(kb * ecum, bd16)], axis=1))
        u, w = uw[:, :BRANCH_WIDTH], uw[:, BRANCH_WIDTH:]
        both = _dot(_bf(a_qk), jnp.concatenate([_stack(u, bd16), _stack(w, bd16)], axis=1))
        oi_s[rows, :] = both[:, :BRANCH_WIDTH]
        qe_s[rows, :] = qc * ecum - both[:, BRANCH_WIDTH:]
        k_dec = _bf(kc * jnp.exp(clast - cc))
        wu = _dot_tn(_bf(jnp.concatenate([w, u], axis=1)), k_dec)
        corr_s[c] = _bf(-(wu[:BRANCH_WIDTH] * bdmask))
        add_s[c] = wu[BRANCH_WIDTH:] * bdmask

    def intra_group(i, carry):
        for k in range(B_UNROLL):
            intra(i * B_UNROLL + k)
        return carry

    lax.fori_loop(0, nchunk // B_UNROLL, intra_group, 0)

    st = st_ref[...]
    for c in range(nchunk):
        st16 = _bf(st)
        sts_s[c] = st16
        clast = ce_s[(c + 1) * B_CHUNK - 1:(c + 1) * B_CHUNK, :]
        st = jnp.exp(clast) * st + _dot(st16, corr_s[c]) + add_s[c]
    st_ref[...] = st

    og = og_ref[...]

    def outputs(c):
        r0 = pl.multiple_of(c * B_CHUNK, B_CHUNK)
        rows = pl.ds(r0, B_CHUNK)
        o = _dot_nt(_bf(qe_s[rows, :]), sts_s[c]) + oi_s[rows, :]
        ms = _seg_sum(o * o, ones_bd) * (1.0 / HEAD_DIM)
        on = o * lax.rsqrt(ms + EPS) * og
        out_ref[0, rows, :] = _bf(on * _silu(z_ref[0, rows, :]))

    def out_group(i, carry):
        for k in range(B_UNROLL):
            outputs(i * B_UNROLL + k)
        return carry

    lax.fori_loop(0, nchunk // B_UNROLL, out_group, 0)


def _gdn(bqkv, z_all, small, conv_w, alog_row, dtb_row, og_row, bsz, seq, tb):
    c = _NPC
    nchunk = tb // B_CHUNK
    assert nchunk % B_UNROLL == 0
    seq_blk = lambda w, col: pl.BlockSpec((1, tb, w), lambda b, j: (b, j, col))
    consts = [jnp.asarray(c["bd"], BF16), jnp.asarray(c["bd"]), jnp.asarray(c["lv"]), jnp.asarray(c["eye_cat"]),
              jnp.asarray(c["sel_beta"], BF16), jnp.asarray(c["sel_bdec"], BF16)]
    act = pltpu.VMEM((tb, BRANCH_WIDTH), F32)
    sq = (nchunk, BRANCH_WIDTH, BRANCH_WIDTH)
    return pl.pallas_call(
        functools.partial(_gdn_kernel, tb=tb),
        grid=(bsz, seq // tb),
        in_specs=[seq_blk(3 * BRANCH_WIDTH, 0), seq_blk(BRANCH_WIDTH, 1), seq_blk(SMALL_W, 0),
                  _full_spec(conv_w.shape), _full_spec((1, SMALL_W)), _full_spec((1, SMALL_W)),
                  _full_spec((1, BRANCH_WIDTH))] + [_full_spec(a.shape) for a in consts],
        out_specs=seq_blk(BRANCH_WIDTH, 0),
        out_shape=jax.ShapeDtypeStruct((bsz, seq, BRANCH_WIDTH), BF16),
        scratch_shapes=[pltpu.VMEM((SUBLANES, 3 * BRANCH_WIDTH), F32),
                        pltpu.VMEM((BRANCH_WIDTH, BRANCH_WIDTH), F32)] + [act] * 7
                       + [pltpu.VMEM(sq, BF16), pltpu.VMEM(sq, F32), pltpu.VMEM(sq, BF16)],
        compiler_params=pltpu.CompilerParams(dimension_semantics=("arbitrary", "arbitrary"),
                                             vmem_limit_bytes=VMEM_LIMIT),
        name="gdn",
    )(bqkv.reshape(bsz, seq, -1), z_all.reshape(bsz, seq, -1), small.reshape(bsz, seq, -1),
      conv_w, alog_row, dtb_row, og_row, *consts)


def _ssd_kernel(xbc_ref, z_ref, small_ref, cw_ref, cb_ref, alog_ref, dtb_ref, dskip_ref, og_ref,
                bd_ref, eye_ref, gm_ref, gmt_ref, sel_ref,
                out_ref,
                tail_ref, st_ref, xs_s, bm_s, cm_s, dt_s, ac_s, *, tb):
    nchunk = tb // C_CHUNK

    @pl.when(pl.program_id(1) == 0)
    def _():
        tail_ref[...] = jnp.zeros_like(tail_ref)
        st_ref[...] = jnp.zeros_like(st_ref)

    bd16 = bd_ref[...]
    eye_cat = eye_ref[...]
    gmask16 = gm_ref[...]
    gmask_t = gmt_ref[...]

    tail = tail_ref[...]
    for part, dst in enumerate((xs_s, bm_s, cm_s)):
        cols = slice(part * BRANCH_WIDTH, (part + 1) * BRANCH_WIDTH)
        dst[...] = _silu(_causal_conv(xbc_ref[0, :, cols], tail[:, cols], cw_ref[:, cols]) + cb_ref[:, cols])
    tail_ref[...] = xbc_ref[0, tb - SUBLANES:, :]
    dt = _softplus(small_ref[0] + dtb_ref[...])
    a = -jnp.exp(alog_ref[...]) * dt
    dt_s[...] = _select_dot(dt, sel_ref[...])
    ac_s[...] = _select_dot(_seg_cumsum(a, C_CHUNK), sel_ref[...])

    il = lax.broadcasted_iota(jnp.int32, (C_CHUNK, BRANCH_WIDTH), 0)
    jl = lax.broadcasted_iota(jnp.int32, (C_CHUNK, BRANCH_WIDTH), 1) & (HEAD_DIM - 1)
    causal = il >= jl
    og = og_ref[...]
    dskip = dskip_ref[...]
    half = BRANCH_WIDTH // C_GROUPS

    def chunk(c, st):
        r0 = pl.multiple_of(c * C_CHUNK, C_CHUNK)
        rows = pl.ds(r0, C_CHUNK)
        xc, bc, cc, dtc, ac = xs_s[rows, :], bm_s[rows, :], cm_s[rows, :], dt_s[rows, :], ac_s[rows, :]
        arow = jnp.sum(ac * eye_cat, axis=0, keepdims=True)
        seg = jnp.exp(jnp.where(causal, ac - arow, -jnp.inf))
        cc16 = _bf(cc)
        bc16 = _bf(bc)
        rt = jnp.concatenate([bc16] * N_HEADS, axis=0) * gmask16
        scores = _dot_nt(cc16, rt) * seg
        xdt = xc * dtc
        y_diag = _dot(_bf(scores), _stack(xdt, bd16))
        y_off = _dot(cc16, _bf(st)) * jnp.exp(ac)
        alast = ac[C_CHUNK - 1:C_CHUNK, :]
        xdd = xdt * jnp.exp(alast - ac)
        st_new = jnp.exp(alast) * st + gmask_t * _dot_tn(bc16, _bf(xdd))
        yv = (y_diag + y_off + dskip * xc) * _silu(z_ref[0, rows, :])
        for gi in range(C_GROUPS):
            yg = yv[:, gi * half:(gi + 1) * half]
            ms = jnp.mean(yg * yg, axis=-1, keepdims=True)
            out_ref[0, rows, gi * half:(gi + 1) * half] = _bf(
                yg * lax.rsqrt(ms + EPS) * og[:, gi * half:(gi + 1) * half])
        return st_new

    def chunk_group(i, carry):
        st = st_ref[...]
        for k in range(C_UNROLL):
            st = chunk(i * C_UNROLL + k, st)
        st_ref[...] = st
        return carry

    lax.fori_loop(0, nchunk // C_UNROLL, chunk_group, 0)


def _ssd(cxbc, z_all, small, conv_w, conv_b, alog_row, dtb_row, dskip_row, og_row, bsz, seq, tb):
    c = _NPC
    assert (tb // C_CHUNK) % C_UNROLL == 0
    seq_blk = lambda w, col: pl.BlockSpec((1, tb, w), lambda b, j: (b, j, col))
    consts = [jnp.asarray(c["bd"], BF16), jnp.asarray(c["eye_cat"]), jnp.asarray(c["gmask"], BF16),
              jnp.asarray(c["gmask"].T.copy()), jnp.asarray(c["sel_cdt"], BF16)]
    act = pltpu.VMEM((tb, BRANCH_WIDTH), F32)
    return pl.pallas_call(
        functools.partial(_ssd_kernel, tb=tb),
        grid=(bsz, seq // tb),
        in_specs=[seq_blk(3 * BRANCH_WIDTH, 0), seq_blk(BRANCH_WIDTH, 2), seq_blk(SMALL_W, 0),
                  _full_spec(conv_w.shape), _full_spec((1, 3 * BRANCH_WIDTH)), _full_spec((1, SMALL_W)),
                  _full_spec((1, SMALL_W)), _full_spec((1, BRANCH_WIDTH)), _full_spec((1, BRANCH_WIDTH))]
                 + [_full_spec(a.shape) for a in consts],
        out_specs=seq_blk(BRANCH_WIDTH, 0),
        out_shape=jax.ShapeDtypeStruct((bsz, seq, BRANCH_WIDTH), BF16),
        scratch_shapes=[pltpu.VMEM((SUBLANES, 3 * BRANCH_WIDTH), F32),
                        pltpu.VMEM((BRANCH_WIDTH, BRANCH_WIDTH), F32)] + [act] * 5,
        compiler_params=pltpu.CompilerParams(dimension_semantics=("arbitrary", "arbitrary"),
                                             vmem_limit_bytes=VMEM_LIMIT),
        name="ssd",
    )(cxbc.reshape(bsz, seq, -1), z_all.reshape(bsz, seq, -1), small.reshape(bsz, seq, -1),
      conv_w, conv_b, alog_row, dtb_row, dskip_row, og_row, *consts)


def _gla_kernel(qkv_ref, z_ref, small_ref, w2_ref, bg_ref, og_ref, ones_ref, dm_ref, cm_ref,
                out_ref,
                st_ref, qg_s, kg_s, k_s, cum_s, *, tb):
    @pl.when(pl.program_id(1) == 0)
    def _():
        st_ref[...] = jnp.zeros_like(st_ref)

    qkv = qkv_ref[0]
    q = qkv[:, :D_KEY] * (D_DK ** -0.5)
    k = qkv[:, D_KEY:2 * D_KEY]
    sm = small_ref[0]
    s_hi = _bf(sm)
    s_lo = _bf(sm - s_hi.astype(F32))
    w2 = w2_ref[...]
    w_hi = _bf(w2)
    w_lo = _bf(w2 - w_hi.astype(F32))
    gl = _dot(s_hi, w_hi) + _dot(s_lo, w_hi) + _dot(s_hi, w_lo) + bg_ref[...]
    glog = jax.nn.log_sigmoid(gl) * (1.0 / D_GATE_NORM)
    cum = _seg_cumsum(glog, D_CHUNK)
    qg_s[...] = q * jnp.exp(cum)
    kg_s[...] = k * jnp.exp(-cum)
    k_s[...] = k
    cum_s[...] = cum

    cmask = cm_ref[...] > 0.0
    klane = lax.broadcasted_iota(jnp.int32, (1, D_KEY), 1) // D_DK
    vlane = lax.broadcasted_iota(jnp.int32, (1, BRANCH_WIDTH), 1) // D_DV
    dmask = dm_ref[...]
    ones_bd = ones_ref[...]
    og = og_ref[...]
    per_group = D_GROUP // D_CHUNK

    def group(gidx, carry):
        r0 = pl.multiple_of(gidx * D_GROUP, D_GROUP)
        rows = pl.ds(r0, D_GROUP)
        qgg, kgg, kk, cc = qg_s[rows, :], kg_s[rows, :], k_s[rows, :], cum_s[rows, :]
        vf = qkv_ref[0, rows, 2 * D_KEY:]
        vg = _bf(vf)
        kg16 = _bf(kgg)
        zero = jnp.zeros_like(qgg)
        qs = _bf(jnp.concatenate([jnp.where(klane == h, qgg, zero) for h in range(D_HEADS)], axis=0))
        a = _dot_nt(qs, kg16)
        a = jnp.where(jnp.concatenate([cmask] * D_HEADS, axis=0), a, 0.0)
        av = _dot(_bf(a), vg)
        o_intra = av[:D_GROUP]
        for h in range(1, D_HEADS):
            o_intra = jnp.where(vlane == h, av[h * D_GROUP:(h + 1) * D_GROUP], o_intra)
        st = st_ref[...]
        o_parts = []
        for c in range(per_group):
            sl = slice(c * D_CHUNK, (c + 1) * D_CHUNK)
            ccc = cc[sl]
            clast = ccc[D_CHUNK - 1:D_CHUNK, :]
            o_parts.append(_dot_nt(_bf(qgg[sl]), _bf(st)))
            kd = kk[sl] * jnp.exp(clast - ccc)
            st = jnp.exp(clast) * st + dmask * _dot_tn(vg[sl], _bf(kd))
        st_ref[...] = st
        o = jnp.concatenate(o_parts, axis=0) + o_intra
        ms = _seg_sum(o * o, ones_bd) * (1.0 / D_DV)
        on = o * lax.rsqrt(ms + EPS) * og
        out_ref[0, rows, :] = _bf(on * _silu(z_ref[0, rows, :]))
        return carry

    lax.fori_loop(0, tb // D_GROUP, group, 0)


def _gla(dqkv, z_all, small, w2pad, bg_row, og_row, bsz, seq, tb):
    c = _NPC
    seq_blk = lambda w, col: pl.BlockSpec((1, tb, w), lambda b, j: (b, j, col))
    consts = [jnp.asarray(c["bd"], BF16), jnp.asarray(c["dmask"]), jnp.asarray(c["cmask"])]
    return pl.pallas_call(
        functools.partial(_gla_kernel, tb=tb),
        grid=(bsz, seq // tb),
        in_specs=[seq_blk(2 * BRANCH_WIDTH, 0), seq_blk(BRANCH_WIDTH, 3), seq_blk(SMALL_W, 0),
                  _full_spec((SMALL_W, D_KEY)), _full_spec((1, D_KEY)), _full_spec((1, BRANCH_WIDTH))]
                 + [_full_spec(a.shape) for a in consts],
        out_specs=seq_blk(BRANCH_WIDTH, 0),
        out_shape=jax.ShapeDtypeStruct((bsz, seq, BRANCH_WIDTH), BF16),
        scratch_shapes=[pltpu.VMEM((BRANCH_WIDTH, D_KEY), F32)] + [pltpu.VMEM((tb, D_KEY), F32)] * 4,
        compiler_params=pltpu.CompilerParams(dimension_semantics=("arbitrary", "arbitrary"),
                                             vmem_limit_bytes=VMEM_LIMIT),
        name="gla",
    )(dqkv.reshape(bsz, seq, -1), z_all.reshape(bsz, seq, -1), small.reshape(bsz, seq, -1),
      w2pad, bg_row, og_row, *consts)


def _merge_kernel(x_ref, ng_ref, br0, br1, br2, br3, wg_ref, bg_ref, wb_ref, wo_ref, out_ref):
    x = x_ref[...]
    ms = jnp.mean(x * x, axis=-1, keepdims=True)
    hb = _bf(x * lax.rsqrt(ms + EPS) * ng_ref[...])
    merged = jnp.zeros(x.shape, F32)
    for i, br in enumerate((br0, br1, br2, br3)):
        gate = jax.nn.sigmoid(_dot(hb, wg_ref[i]) + bg_ref[i])
        merged = merged + gate * _dot(br[...], wb_ref[i])
    out_ref[...] = x + _dot(_bf(merged), wo_ref[...])


def _merge(xf, norm_g, brs, w_gate, b_gate, w_branch, w_out, tm):
    t = xf.shape[0]
    row = lambda w: pl.BlockSpec((tm, w), lambda i: (i, 0))
    return pl.pallas_call(
        _merge_kernel,
        grid=(t // tm,),
        in_specs=[row(D_MODEL), _full_spec((1, D_MODEL))] + [row(BRANCH_WIDTH)] * N_BRANCH
                 + [_const_spec(w_gate.shape), _full_spec(b_gate.shape), _const_spec(w_branch.shape),
                    _const_spec(w_out.shape)],
        out_specs=row(D_MODEL),
        out_shape=jax.ShapeDtypeStruct((t, D_MODEL), F32),
        compiler_params=pltpu.CompilerParams(dimension_semantics=("arbitrary",),
                                             vmem_limit_bytes=VMEM_LIMIT),
        name="merge",
    )(xf, norm_g, *brs, w_gate, b_gate, w_branch, w_out)


def _lane_row(vals, offset, width):
    return jnp.zeros((1, width), F32).at[0, offset:offset + vals.shape[0]].set(vals.astype(F32))


def _rope_rows():
    lane = np.arange(LANES) % HEAD_DIM
    inv_freq = ROPE_THETA ** (-jnp.arange(ROPE_HALF, dtype=F32) / ROPE_HALF)
    invf = jnp.where(lane < ROPE_DIM, inv_freq[lane % ROPE_HALF], 0.0).astype(F32)[None, :]
    sgn = np.where(lane < ROPE_HALF, -1.0, np.where(lane < ROPE_DIM, 1.0, 0.0)).astype(np.float32)[None, :]
    return invf, jnp.asarray(sgn)


def _pack_w_in(w_in):
    sizes = (A_QKV, A_QKV, A_QKV, BRANCH_WIDTH, 3 * BRANCH_WIDTH, BRANCH_WIDTH, N_HEADS, N_HEADS,
             3 * BRANCH_WIDTH, BRANCH_WIDTH, N_HEADS, D_KEY, D_KEY, BRANCH_WIDTH, BRANCH_WIDTH, D_GATE_RANK)
    offs = np.concatenate([[0], np.cumsum(sizes)])
    part = lambda i: w_in[:, offs[i]:offs[i + 1]]
    (a_q, a_k, a_v, a_z, b_qkv, b_z, b_beta, b_a, c_xbc, c_z, c_dt, d_q, d_k, d_v, d_z, d_g) = (
        part(i) for i in range(len(sizes)))
    small = jnp.zeros((D_MODEL, SMALL_W), w_in.dtype)
    small = small.at[:, SM_BETA:SM_BETA + N_HEADS].set(b_beta)
    small = small.at[:, SM_BDEC:SM_BDEC + N_HEADS].set(b_a)
    small = small.at[:, SM_CDT:SM_CDT + N_HEADS].set(c_dt)
    small = small.at[:, SM_DG:SM_DG + D_GATE_RANK].set(d_g)
    w_all = jnp.concatenate([a_q, a_k, a_v, a_z, b_z, c_z, d_z, b_qkv, c_xbc, d_q, d_k, d_v, small], axis=1)
    assert w_all.shape[1] == W_TOTAL
    return _bf(w_all)


def _layer(xf, pos, bsz, seq, p, tm, tb):
    invf, sgn = _rope_rows()
    ones_bd = jnp.asarray(_NPC["bd"], BF16)
    w_all = _pack_w_in(p["w_in"])
    ng = p["norm_g"][None, :]
    qn = jnp.tile(p["a_q_norm"], A_QKV // HEAD_DIM)[None, :]
    kn = jnp.tile(p["a_k_norm"], A_QKV // HEAD_DIM)[None, :]
    outs = _inproj(xf, pos, ng, w_all, qn, kn, invf, sgn, ones_bd, bsz, seq, tm)
    qs, ks, vs = outs[0:3], outs[3:6], outs[6:9]
    z_all, bqkv, cxbc, dqkv, small = outs[9:]

    os_, lses = [], []
    for gi in range(A_N_GROUPS):
        o, lse = _attn_group(qs[gi], ks[gi], vs[gi], gi)
        os_.append(o)
        lses.append(lse)
    br_a = _attn_combine(os_, lses, z_all, bsz, seq, tm)

    br_b = _gdn(bqkv, z_all, small, p["b_conv_w"],
                _lane_row(p["b_a_log"], SM_BDEC, SMALL_W), _lane_row(p["b_dt_bias"], SM_BDEC, SMALL_W),
                jnp.tile(p["b_out_norm"], N_HEADS)[None, :], bsz, seq, tb).reshape(bsz * seq, BRANCH_WIDTH)

    br_c = _ssd(cxbc, z_all, small, p["c_conv_w"], p["c_conv_b"][None, :],
                _lane_row(p["c_a_log"], SM_CDT, SMALL_W), _lane_row(p["c_dt_bias"], SM_CDT, SMALL_W),
                jnp.repeat(p["c_d_skip"], HEAD_DIM)[None, :], p["c_out_norm"][None, :],
                bsz, seq, tb).reshape(bsz * seq, BRANCH_WIDTH)

    w2pad = jnp.zeros((SMALL_W, D_KEY), F32).at[SM_DG:SM_DG + D_GATE_RANK, :].set(p["d_gate_w2"])
    br_d = _gla(dqkv, z_all, small, w2pad, p["d_gate_b"][None, :],
                jnp.tile(p["d_out_norm"], D_HEADS)[None, :], bsz, seq, tb).reshape(bsz * seq, BRANCH_WIDTH)

    return _merge(xf, ng, (br_a, br_b, br_c, br_d), _bf(p["w_gate"]), p["b_gate"][:, None, :],
                  _bf(p["w_branch"]), _bf(p["w_out"]), tm)


def _pick_tile(n, target):
    t = min(n, target)
    while n % t:
        t //= 2
    return t


def kernel(x, positions, norm_g, w_in, a_q_norm, a_k_norm, b_conv_w, b_a_log, b_dt_bias, b_out_norm,
           c_conv_w, c_conv_b, c_a_log, c_dt_bias, c_d_skip, c_out_norm, d_gate_w2, d_gate_b, d_out_norm,
           w_gate, b_gate, w_branch, w_out):
    bsz, seq, d_model = x.shape
    assert d_model == D_MODEL and seq % (2 * A_BLOCK * A_GROUPS[-1][1]) == 0
    xf = x.reshape(bsz * seq, d_model)
    tm = _pick_tile(seq, 512)
    assert tm % LANES == 0
    pos = positions.reshape(bsz * seq // tm, tm // LANES, LANES).astype(jnp.int32)
    tb = _pick_tile(seq, 512)
    names = ("norm_g", "w_in", "a_q_norm", "a_k_norm", "b_conv_w", "b_a_log", "b_dt_bias", "b_out_norm",
             "c_conv_w", "c_conv_b", "c_a_log", "c_dt_bias", "c_d_skip", "c_out_norm", "d_gate_w2",
             "d_gate_b", "d_out_norm", "w_gate", "b_gate", "w_branch", "w_out")
    stacked = (norm_g, w_in, a_q_norm, a_k_norm, b_conv_w, b_a_log, b_dt_bias, b_out_norm,
               c_conv_w, c_conv_b, c_a_log, c_dt_bias, c_d_skip, c_out_norm, d_gate_w2, d_gate_b,
               d_out_norm, w_gate, b_gate, w_branch, w_out)
    for layer in range(norm_g.shape[0]):
        p = {n: a[layer] for n, a in zip(names, stacked)}
        xf = _layer(xf, pos, bsz, seq, p, tm, tb)
    return xf.reshape(bsz, seq, d_model)
```

```python
import functools
import math

import numpy as np
import jax
import jax.numpy as jnp
from jax import lax
from jax.experimental import pallas as pl
from jax.experimental.pallas import tpu as pltpu

F32 = jnp.float32
BF16 = jnp.bfloat16

D_MODEL = 1024
N_BRANCH = 4
BRANCH_WIDTH = D_MODEL // N_BRANCH
HEAD_DIM = 64
N_HEADS = BRANCH_WIDTH // HEAD_DIM
EPS = 1e-6
ROPE_DIM = HEAD_DIM // 4
ROPE_HALF = ROPE_DIM // 2
ROPE_THETA = 500000.0
CONV_K = 4
PRO_ROWS = 128

A_GROUPS = ((128, 1), (512, 4), (2048, 16))
A_N_GROUPS = len(A_GROUPS)
A_QKV = A_N_GROUPS * BRANCH_WIDTH
A_BLOCK = 128
A_STEP_ROWS = 1024

B_CHUNK = 64
B_UNROLL = 8
C_GROUPS = 2
C_STATE = 128
C_CHUNK = 64
C_UNROLL = 4
D_HEADS = 4
D_KEY = BRANCH_WIDTH // 2
D_DK = D_KEY // D_HEADS
D_DV = BRANCH_WIDTH // D_HEADS
D_GATE_RANK = 16
D_GATE_NORM = 16.0
D_CHUNK = 32
D_GROUP = 128
D_UNROLL = 2

LANES = 128
SUBLANES = 8
SLABS = BRANCH_WIDTH // LANES
SMALL_W = LANES
SM_BETA, SM_BDEC, SM_CDT, SM_DG = 0, 4, 8, 12

W_AQ, W_AK, W_AV = 0, A_QKV, 2 * A_QKV
W_Z = 3 * A_QKV
W_BQKV = W_Z + D_MODEL
W_CXBC = W_BQKV + 3 * BRANCH_WIDTH
W_DQKV = W_CXBC + 3 * BRANCH_WIDTH
W_SMALL = W_DQKV + 2 * BRANCH_WIDTH
W_TOTAL = W_SMALL + SMALL_W

VMEM_LIMIT = 56 * 1024 * 1024


def _bf(x):
    return x.astype(BF16)


def _dot(a, b):
    return jnp.dot(a, b, preferred_element_type=F32)


def _dot_nt(a, b):
    return lax.dot_general(a, b, (((1,), (1,)), ((), ())), preferred_element_type=F32)


def _dot_tn(a, b):
    return lax.dot_general(a, b, (((0,), (0,)), ((), ())), preferred_element_type=F32)


def _expand_heads(x, offset):
    rows = x.shape[0]
    head_lane = lax.broadcasted_iota(jnp.int32, (1, BRANCH_WIDTH), 1) // HEAD_DIM
    out = jnp.broadcast_to(x[:, offset:offset + 1], (rows, BRANCH_WIDTH))
    for h in range(1, N_HEADS):
        out = jnp.where(head_lane == h, jnp.broadcast_to(x[:, offset + h:offset + h + 1], (rows, BRANCH_WIDTH)), out)
    return out


def _seg_sum(x, ones_bd):
    hi = _bf(x)
    lo = _bf(x - hi.astype(F32))
    return _dot(hi, ones_bd) + _dot(lo, ones_bd)


def _seg_cumsum(x, seg):
    pos = lax.broadcasted_iota(jnp.int32, x.shape, 0) & (seg - 1)
    step = 1
    while step < seg:
        x = x + jnp.where(pos >= step, pltpu.roll(x, step, axis=0), 0.0)
        step *= 2
    return x


def _silu(x):
    h = 0.5 * x
    return h + h * jnp.tanh(h)


def _softplus(x):
    return jnp.maximum(x, 0.0) + jnp.log1p(jnp.exp(-jnp.abs(x)))


def _shift_rows(x, tail, s):
    r = pltpu.roll(x, s, axis=0)
    t = pltpu.roll(tail, s, axis=0)
    row = lax.broadcasted_iota(jnp.int32, tail.shape, 0)
    first = jnp.where(row < s, t, r[:SUBLANES])
    return jnp.concatenate([first, r[SUBLANES:]], axis=0)


def _causal_conv(x, tail, w):
    y = x * w[CONV_K - 1:CONV_K, :]
    for s in range(1, CONV_K):
        y = y + _shift_rows(x, tail, s) * w[CONV_K - 1 - s:CONV_K - s, :]
    return y


def _stack(y, bdmask16):
    return jnp.concatenate([_bf(y)] * N_HEADS, axis=0) * bdmask16


def _np_consts():
    r = np.arange(BRANCH_WIDTH)
    bd = (r[:, None] // HEAD_DIM == r[None, :] // HEAD_DIM).astype(np.float32)
    i = np.arange(B_CHUNK)[:, None]
    j = (r % HEAD_DIM)[None, :]
    xor = np.maximum(i ^ j, 1)
    lv = np.where(i > j, np.floor(np.log2(xor)).astype(np.int32), -1).astype(np.int32)
    eye_cat = (i == j).astype(np.float32)
    gmask = ((r[None, :] // C_STATE) == (r[:, None] // HEAD_DIM) // (N_HEADS // C_GROUPS)).astype(np.float32)
    dmask = ((np.arange(D_KEY)[:, None] // D_DK) == (r[None, :] // D_DV)).astype(np.float32)
    t = np.arange(D_GROUP)
    cmask = ((t[:, None] // D_CHUNK == t[None, :] // D_CHUNK) & (t[:, None] >= t[None, :])).astype(np.float32)

    return dict(bd=bd, lv=lv, eye_cat=eye_cat, gmask=gmask, dmask=dmask, cmask=cmask)


_NPC = _np_consts()


def _full_spec(shape):
    nd = len(shape)
    return pl.BlockSpec(shape, lambda *_: (0,) * nd)


def _const_spec(shape):
    nd = len(shape)
    return pl.BlockSpec(shape, lambda *_: (0,) * nd, pipeline_mode=pl.Buffered(1))


def _inproj_kernel(x_ref, pos_ref, ng_ref, w_ref, qn_ref, kn_ref, invf_ref, sgn_ref, ones_ref,
                   q0_ref, q1_ref, q2_ref, k0_ref, k1_ref, k2_ref, v0_ref, v1_ref, v2_ref,
                   z_ref, bqkv_ref, cxbc_ref, dqkv_ref, small_ref, perm_s, *, tm):
    x = x_ref[...]
    ms = jnp.mean(x * x, axis=-1, keepdims=True)
    hb = _bf(x * lax.rsqrt(ms + EPS) * ng_ref[...])

    def proj(lo, width):
        return _dot(hb, w_ref[:, lo:lo + width])

    pos_f = pos_ref[0].astype(F32)
    pos_col = jnp.concatenate([jnp.broadcast_to(pos_f[k:k + 1, :], (LANES, LANES)).T
                               for k in range(tm // LANES)], axis=0)
    ang = pos_col * invf_ref[...]
    cos = jnp.cos(ang)
    sin = jnp.sin(ang) * sgn_ref[...]
    lane = lax.broadcasted_iota(jnp.int32, (1, LANES), 1) & (HEAD_DIM - 1)
    first_half = lane < ROPE_HALF
    ones_bd = ones_ref[...]

    def emit(slabs, out_ref, dil):
        for s, v in enumerate(slabs):
            if dil == 1:
                out_ref[0, 0, :, s * LANES:(s + 1) * LANES] = _bf(v)
            else:
                perm_s[s] = v
        if dil > 1:
            rows = tm // dil
            for s in range(len(slabs)):
                for r in range(dil):
                    out_ref[0, r, :, s * LANES:(s + 1) * LANES] = _bf(perm_s[s, pl.ds(r, rows, stride=dil), :])

    def qk_norm_rope(lo, gain_ref, out_refs, scale):
        for g in range(A_N_GROUPS):
            u = proj(lo + g * BRANCH_WIDTH, BRANCH_WIDTH)
            ss = _seg_sum(u * u, ones_bd)
            gain = gain_ref[:, g * BRANCH_WIDTH:(g + 1) * BRANCH_WIDTH]
            un = u * lax.rsqrt(ss * (1.0 / HEAD_DIM) + EPS) * (gain * scale)
            slabs = []
            for s in range(SLABS):
                v = un[:, s * LANES:(s + 1) * LANES]
                partner = jnp.where(first_half, pltpu.roll(v, LANES - ROPE_HALF, axis=1),
                                    pltpu.roll(v, ROPE_HALF, axis=1))
                slabs.append(v * cos + partner * sin)
            emit(slabs, out_refs[g], A_GROUPS[g][1])

    qk_norm_rope(W_AQ, qn_ref, (q0_ref, q1_ref, q2_ref), HEAD_DIM ** -0.5)
    qk_norm_rope(W_AK, kn_ref, (k0_ref, k1_ref, k2_ref), 1.0)
    for g, v_ref in enumerate((v0_ref, v1_ref, v2_ref)):
        u = proj(W_AV + g * BRANCH_WIDTH, BRANCH_WIDTH)
        emit([u[:, s * LANES:(s + 1) * LANES] for s in range(SLABS)], v_ref, A_GROUPS[g][1])
    z_ref[...] = proj(W_Z, D_MODEL)
    bqkv_ref[...] = proj(W_BQKV, 3 * BRANCH_WIDTH)
    cxbc_ref[...] = proj(W_CXBC, 3 * BRANCH_WIDTH)
    dqkv_ref[...] = proj(W_DQKV, 2 * BRANCH_WIDTH)
    small_ref[...] = proj(W_SMALL, SMALL_W)


def _inproj(xf, pos, norm_g, w_all, qn, kn, invf, sgn, ones_bd, bsz, seq, tm):
    t = xf.shape[0]
    tps = seq // tm
    row = lambda w: pl.BlockSpec((tm, w), lambda i: (i, 0))
    dils = [d for _, d in A_GROUPS]
    res_spec = lambda d: pl.BlockSpec((1, d, tm // d, BRANCH_WIDTH), lambda i: (i // tps, 0, i % tps, 0))
    res_shape = lambda d: jax.ShapeDtypeStruct((bsz, d, seq // d, BRANCH_WIDTH), BF16)
    flat = [(D_MODEL, F32), (3 * BRANCH_WIDTH, F32), (3 * BRANCH_WIDTH, F32), (2 * BRANCH_WIDTH, F32),
            (SMALL_W, F32)]
    return pl.pallas_call(
        functools.partial(_inproj_kernel, tm=tm),
        grid=(t // tm,),
        in_specs=[row(D_MODEL), pl.BlockSpec((1, tm // LANES, LANES), lambda i: (i, 0, 0)),
                  _full_spec((1, D_MODEL)), _const_spec((D_MODEL, W_TOTAL)),
                  _full_spec((1, A_QKV)), _full_spec((1, A_QKV)), _full_spec((1, LANES)),
                  _full_spec((1, LANES)), _full_spec((BRANCH_WIDTH, BRANCH_WIDTH))],
        out_specs=[res_spec(d) for d in dils] * 3 + [row(w) for w, _ in flat],
        out_shape=[res_shape(d) for d in dils] * 3 + [jax.ShapeDtypeStruct((t, w), dt) for w, dt in flat],
        scratch_shapes=[pltpu.VMEM((SLABS, tm, LANES), F32)],
        compiler_params=pltpu.CompilerParams(dimension_semantics=("arbitrary",),
                                             vmem_limit_bytes=VMEM_LIMIT),
        name="inproj",
    )(xf, pos, norm_g, w_all, qn, kn, invf, sgn, ones_bd)


def _attn_kernel(q_ref, k_ref, v_ref, bias_ref, o_ref, lse_ref, *, n, rb):
    nb = n // A_BLOCK
    head_lane = lax.broadcasted_iota(jnp.int32, (1, BRANCH_WIDTH), 1) // HEAD_DIM
    half = nb // 2
    per_iter = 2

    def pair(it, carry):
        rr = it // half
        jp = it - rr * half
        js = [per_iter * jp + b for b in range(per_iter)]
        q0 = [pl.multiple_of(j * A_BLOCK, A_BLOCK) for j in js]
        k0 = [pl.multiple_of(jnp.maximum(j - 1, 0) * A_BLOCK, A_BLOCK) for j in js]
        s = []
        for b in range(per_iter):
            q = q_ref[0, rr, pl.ds(q0[b], A_BLOCK), :]
            zero = jnp.zeros_like(q)
            qs = jnp.concatenate([jnp.where(head_lane == h, q, zero) for h in range(N_HEADS)], axis=0)
            s.append(_dot_nt(qs, k_ref[0, rr, pl.ds(k0[b], 2 * A_BLOCK), :]) + bias_ref[jnp.minimum(js[b], 1)])
        m = [jnp.max(x, axis=-1, keepdims=True) for x in s]
        p = [jnp.exp(x - mx) for x, mx in zip(s, m)]
        l = [jnp.sum(x, axis=-1, keepdims=True) for x in p]
        pv = [_dot(_bf(p[b]), v_ref[0, rr, pl.ds(k0[b], 2 * A_BLOCK), :]) * (1.0 / l[b]) for b in range(per_iter)]
        for b in range(per_iter):
            lse = m[b] + jnp.log(l[b])
            o_acc = pv[b][:A_BLOCK]
            lse_acc = jnp.broadcast_to(lse[:A_BLOCK], (A_BLOCK, BRANCH_WIDTH))
            for h in range(1, N_HEADS):
                o_acc = jnp.where(head_lane == h, pv[b][h * A_BLOCK:(h + 1) * A_BLOCK], o_acc)
                lse_acc = jnp.where(head_lane == h, lse[h * A_BLOCK:(h + 1) * A_BLOCK], lse_acc)
            o_ref[0, rr, pl.ds(q0[b], A_BLOCK), :] = _bf(o_acc)
            lse_ref[0, rr, pl.ds(q0[b], A_BLOCK), :] = lse_acc
        return carry

    lax.fori_loop(0, rb * half, pair, 0)


def _attn_group(q, k, v, gi):
    window, dil = A_GROUPS[gi]
    bsz, _, n, _ = q.shape
    assert n % (2 * A_BLOCK) == 0
    rb = min(dil, max(1, A_STEP_ROWS // n))
    win = window // dil
    qi = (np.arange(N_HEADS * A_BLOCK) % A_BLOCK)[:, None]
    kj = np.arange(2 * A_BLOCK)[None, :]
    band = lambda off: np.where((qi - kj + off >= 0) & (qi - kj + off <= win), 0.0, -np.inf).astype(np.float32)
    bias = jnp.asarray(np.stack([band(0), band(A_BLOCK)]))
    blk = lambda w: pl.BlockSpec((1, rb, n, w), lambda b, r: (b, r, 0, 0))
    return pl.pallas_call(
        functools.partial(_attn_kernel, n=n, rb=rb),
        grid=(bsz, dil // rb),
        in_specs=[blk(BRANCH_WIDTH)] * 3 + [_full_spec(bias.shape)],
        out_specs=[blk(BRANCH_WIDTH), blk(BRANCH_WIDTH)],
        out_shape=[jax.ShapeDtypeStruct((bsz, dil, n, BRANCH_WIDTH), BF16),
                   jax.ShapeDtypeStruct((bsz, dil, n, BRANCH_WIDTH), F32)],
        compiler_params=pltpu.CompilerParams(dimension_semantics=("arbitrary", "arbitrary"),
                                             vmem_limit_bytes=VMEM_LIMIT),
        name=f"attn_d{dil}",
    )(q, k, v, bias)


def _attn_combine_kernel(o0, o1, o2, l0, l1, l2, z_ref, out_ref, lse_s, o_s, *, tm):
    dils = [d for _, d in A_GROUPS]

    def token_major(ref, dil, scratch, sl, cast):
        if dil == 1:
            return cast(ref[0, 0, :, sl])
        rows = tm // dil
        for r in range(dil):
            scratch[pl.ds(r, rows, stride=dil), :] = cast(ref[0, r, :, sl])
        return scratch[...]

    for s in range(SLABS):
        sl = slice(s * LANES, (s + 1) * LANES)
        lses = [token_major(lref, dil, lse_s.at[g], sl, lambda a: a)
                for g, (lref, dil) in enumerate(zip((l0, l1, l2), dils))]
        m = jnp.maximum(jnp.maximum(lses[0], lses[1]), lses[2])
        es = [jnp.exp(l - m) for l in lses]
        acc = jnp.zeros((tm, LANES), F32)
        for g, (oref, dil) in enumerate(zip((o0, o1, o2), dils)):
            acc = acc + token_major(oref, dil, o_s, sl, lambda a: a.astype(F32)) * es[g]
        acc = acc / (es[0] + es[1] + es[2])
        out_ref[:, sl] = _bf(acc * _silu(z_ref[:, sl]))


def _attn_combine(os_, lses, z_all, bsz, seq, tm):
    t = z_all.shape[0]
    tps = seq // tm
    dils = [d for _, d in A_GROUPS]
    res = lambda d, w: pl.BlockSpec((1, d, tm // d, w), lambda i: (i // tps, 0, i % tps, 0))
    row = pl.BlockSpec((tm, BRANCH_WIDTH), lambda i: (i, 0))
    return pl.pallas_call(
        functools.partial(_attn_combine_kernel, tm=tm),
        grid=(t // tm,),
        in_specs=[res(d, BRANCH_WIDTH) for d in dils] * 2 + [row],
        out_specs=row,
        out_shape=jax.ShapeDtypeStruct((t, BRANCH_WIDTH), BF16),
        scratch_shapes=[pltpu.VMEM((A_N_GROUPS, tm, LANES), F32), pltpu.VMEM((tm, LANES), F32)],
        compiler_params=pltpu.CompilerParams(dimension_semantics=("arbitrary",)),
        name="attn_combine",
    )(*os_, *lses, z_all)


def _gdn_kernel(qkv_ref, z_ref, small_ref, cw_ref, alog_ref, dtb_ref, og_ref,
                ones_ref, bd_ref, lv_ref, eye_ref,
                out_ref,
                tail_ref, st_ref, q_s, k_s, vb_s, kb_s, ce_s, qe_s, oi_s, corr_s, add_s, sts_s, *, tb):
    nchunk = tb // B_CHUNK

    @pl.when(pl.program_id(1) == 0)
    def _():
        tail_ref[...] = jnp.zeros_like(tail_ref)
        st_ref[...] = jnp.zeros_like(st_ref)

    ones_bd = ones_ref[...]
    bd16 = ones_bd
    bdmask = bd_ref[...]
    lv = lv_ref[...]
    eye_cat = eye_ref[...]

    sm = small_ref[0]
    g = -jnp.exp(alog_ref[...]) * _softplus(sm + dtb_ref[...])
    ce_s[...] = _expand_heads(_seg_cumsum(g, B_CHUNK), SM_BDEC)
    vb_s[...] = _expand_heads(jax.nn.sigmoid(sm), SM_BETA)

    def prologue(r, carry):
        rows = pl.ds(pl.multiple_of(r * PRO_ROWS, PRO_ROWS), PRO_ROWS)
        halo = pl.ds(pl.multiple_of(jnp.maximum(r * PRO_ROWS - SUBLANES, 0), SUBLANES), SUBLANES)
        tail = jnp.where(r == 0, tail_ref[...], qkv_ref[0, halo, :])
        beta = vb_s[rows, :]
        for part in range(3):
            cols = slice(part * BRANCH_WIDTH, (part + 1) * BRANCH_WIDTH)
            y = _silu(_causal_conv(qkv_ref[0, rows, cols], tail[:, cols], cw_ref[:, cols]))
            if part == 0:
                q_s[rows, :] = y * lax.rsqrt(_seg_sum(y * y, ones_bd) + EPS) * (HEAD_DIM ** -0.5)
            elif part == 1:
                kn = y * lax.rsqrt(_seg_sum(y * y, ones_bd) + EPS)
                k_s[rows, :] = kn
                kb_s[rows, :] = kn * beta
            else:
                vb_s[rows, :] = y * beta
        return carry

    lax.fori_loop(0, tb // PRO_ROWS, prologue, 0)
    tail_ref[...] = qkv_ref[0, tb - SUBLANES:, :]

    il = lax.broadcasted_iota(jnp.int32, (B_CHUNK, BRANCH_WIDTH), 0)
    jl = lax.broadcasted_iota(jnp.int32, (B_CHUNK, BRANCH_WIDTH), 1) & (HEAD_DIM - 1)
    causal = il >= jl

    def intra_group(i, carry):
        ks = range(B_UNROLL)
        rows = [pl.ds(pl.multiple_of((i * B_UNROLL + k) * B_CHUNK, B_CHUNK), B_CHUNK) for k in ks]
        cc = [ce_s[r, :] for r in rows]
        kc = [k_s[r, :] for r in rows]
        qc = [q_s[r, :] for r in rows]
        kb = [kb_s[r, :] for r in rows]
        gamma = [jnp.exp(jnp.where(causal, c_ - jnp.sum(c_ * eye_cat, axis=0, keepdims=True), -jnp.inf))
                 for c_ in cc]
        prod = [_dot_nt(_bf(jnp.concatenate([kb[k], qc[k]], axis=0)), _stack(kc[k], bd16)) for k in ks]
        a_kk = [jnp.where(il > jl, prod[k][:B_CHUNK] * gamma[k], 0.0) for k in ks]
        a_qk = [_bf(prod[k][B_CHUNK:] * gamma[k]) for k in ks]
        t = [eye_cat - jnp.where(lv == 0, a, 0.0) for a in a_kk]
        for lb in range(1, int(math.log2(B_CHUNK))):
            x = [_dot(_bf(t[k]), _stack(jnp.where(lv == lb, a_kk[k], 0.0), bd16)) for k in ks]
            t = [t[k] - _dot(_bf(x[k]), _stack(t[k], bd16)) for k in ks]
        ecum = [jnp.exp(c_) for c_ in cc]
        uw = [_dot(_bf(t[k]), jnp.concatenate([_stack(vb_s[rows[k], :], bd16),
                                               _stack(kb[k] * ecum[k], bd16)], axis=1)) for k in ks]
        u = [a[:, :BRANCH_WIDTH] for a in uw]
        w = [a[:, BRANCH_WIDTH:] for a in uw]
        both = [_dot(a_qk[k], jnp.concatenate([_stack(u[k], bd16), _stack(w[k], bd16)], axis=1)) for k in ks]
        k_dec = [_bf(kc[k] * jnp.exp(cc[k][B_CHUNK - 1:B_CHUNK, :] - cc[k])) for k in ks]
        wu = [_dot_tn(_bf(jnp.concatenate([w[k], u[k]], axis=1)), k_dec[k]) for k in ks]
        for k in ks:
            oi_s[rows[k], :] = both[k][:, :BRANCH_WIDTH]
            qe_s[rows[k], :] = qc[k] * ecum[k] - both[k][:, BRANCH_WIDTH:]
            corr_s[i * B_UNROLL + k] = _bf(-(wu[k][:BRANCH_WIDTH] * bdmask))
            add_s[i * B_UNROLL + k] = wu[k][BRANCH_WIDTH:] * bdmask
        return carry

    lax.fori_loop(0, nchunk // B_UNROLL, intra_group, 0)

    st = st_ref[...]
    for c in range(nchunk):
        st16 = _bf(st)
        sts_s[c] = st16
        clast = ce_s[(c + 1) * B_CHUNK - 1:(c + 1) * B_CHUNK, :]
        st = jnp.exp(clast) * st + _dot(st16, corr_s[c]) + add_s[c]
    st_ref[...] = st

    og = og_ref[...]

    def outputs(c):
        r0 = pl.multiple_of(c * B_CHUNK, B_CHUNK)
        rows = pl.ds(r0, B_CHUNK)
        o = _dot_nt(_bf(qe_s[rows, :]), sts_s[c]) + oi_s[rows, :]
        ms = _seg_sum(o * o, ones_bd) * (1.0 / HEAD_DIM)
        on = o * lax.rsqrt(ms + EPS) * og
        out_ref[0, rows, :] = _bf(on * _silu(z_ref[0, rows, :]))

    def out_group(i, carry):
        for k in range(B_UNROLL):
            outputs(i * B_UNROLL + k)
        return carry

    lax.fori_loop(0, nchunk // B_UNROLL, out_group, 0)


def _gdn(bqkv, z_all, small, conv_w, alog_row, dtb_row, og_row, bsz, seq, tb):
    c = _NPC
    nchunk = tb // B_CHUNK
    assert nchunk % B_UNROLL == 0
    seq_blk = lambda w, col: pl.BlockSpec((1, tb, w), lambda b, j: (b, j, col))
    consts = [jnp.asarray(c["bd"], BF16), jnp.asarray(c["bd"]), jnp.asarray(c["lv"]), jnp.asarray(c["eye_cat"])]
    act = pltpu.VMEM((tb, BRANCH_WIDTH), F32)
    sq = (nchunk, BRANCH_WIDTH, BRANCH_WIDTH)
    return pl.pallas_call(
        functools.partial(_gdn_kernel, tb=tb),
        grid=(bsz, seq // tb),
        in_specs=[seq_blk(3 * BRANCH_WIDTH, 0), seq_blk(BRANCH_WIDTH, 1), seq_blk(SMALL_W, 0),
                  _full_spec(conv_w.shape), _full_spec((1, SMALL_W)), _full_spec((1, SMALL_W)),
                  _full_spec((1, BRANCH_WIDTH))] + [_full_spec(a.shape) for a in consts],
        out_specs=seq_blk(BRANCH_WIDTH, 0),
        out_shape=jax.ShapeDtypeStruct((bsz, seq, BRANCH_WIDTH), BF16),
        scratch_shapes=[pltpu.VMEM((SUBLANES, 3 * BRANCH_WIDTH), F32),
                        pltpu.VMEM((BRANCH_WIDTH, BRANCH_WIDTH), F32)] + [act] * 7
                       + [pltpu.VMEM(sq, BF16), pltpu.VMEM(sq, F32), pltpu.VMEM(sq, BF16)],
        compiler_params=pltpu.CompilerParams(dimension_semantics=("arbitrary", "arbitrary"),
                                             vmem_limit_bytes=VMEM_LIMIT),
        name="gdn",
    )(bqkv.reshape(bsz, seq, -1), z_all.reshape(bsz, seq, -1), small.reshape(bsz, seq, -1),
      conv_w, alog_row, dtb_row, og_row, *consts)


def _ssd_kernel(xbc_ref, z_ref, small_ref, cw_ref, cb_ref, alog_ref, dtb_ref, dskip_ref, og_ref,
                bd_ref, eye_ref, gm_ref, gmt_ref,
                out_ref,
                tail_ref, st_ref, xs_s, bm_s, cm_s, dt_s, ac_s, *, tb):
    nchunk = tb // C_CHUNK

    @pl.when(pl.program_id(1) == 0)
    def _():
        tail_ref[...] = jnp.zeros_like(tail_ref)
        st_ref[...] = jnp.zeros_like(st_ref)

    bd16 = bd_ref[...]
    eye_cat = eye_ref[...]
    gmask16 = gm_ref[...]
    gmask_t = gmt_ref[...]

    def prologue(r, carry):
        rows = pl.ds(pl.multiple_of(r * PRO_ROWS, PRO_ROWS), PRO_ROWS)
        halo = pl.ds(pl.multiple_of(jnp.maximum(r * PRO_ROWS - SUBLANES, 0), SUBLANES), SUBLANES)
        tail = jnp.where(r == 0, tail_ref[...], xbc_ref[0, halo, :])
        for part, dst in enumerate((xs_s, bm_s, cm_s)):
            cols = slice(part * BRANCH_WIDTH, (part + 1) * BRANCH_WIDTH)
            dst[rows, :] = _silu(_causal_conv(xbc_ref[0, rows, cols], tail[:, cols], cw_ref[:, cols])
                                 + cb_ref[:, cols])
        return carry

    lax.fori_loop(0, tb // PRO_ROWS, prologue, 0)
    tail_ref[...] = xbc_ref[0, tb - SUBLANES:, :]
    dt = _softplus(small_ref[0] + dtb_ref[...])
    a = -jnp.exp(alog_ref[...]) * dt
    dt_s[...] = _expand_heads(dt, SM_CDT)
    ac_s[...] = _expand_heads(_seg_cumsum(a, C_CHUNK), SM_CDT)

    il = lax.broadcasted_iota(jnp.int32, (C_CHUNK, BRANCH_WIDTH), 0)
    jl = lax.broadcasted_iota(jnp.int32, (C_CHUNK, BRANCH_WIDTH), 1) & (HEAD_DIM - 1)
    causal = il >= jl
    og = og_ref[...]
    dskip = dskip_ref[...]
    half = BRANCH_WIDTH // C_GROUPS

    def chunk_group(i, carry):
        ks = range(C_UNROLL)
        rows = [pl.ds(pl.multiple_of((i * C_UNROLL + k) * C_CHUNK, C_CHUNK), C_CHUNK) for k in ks]
        xc = [xs_s[r, :] for r in rows]
        ac = [ac_s[r, :] for r in rows]
        cc16 = [_bf(cm_s[r, :]) for r in rows]
        bc16 = [_bf(bm_s[r, :]) for r in rows]
        xdt = [xc[k] * dt_s[rows[k], :] for k in ks]
        seg = [jnp.exp(jnp.where(causal, a - jnp.sum(a * eye_cat, axis=0, keepdims=True), -jnp.inf)) for a in ac]
        scores = [_dot_nt(cc16[k], jnp.concatenate([bc16[k]] * N_HEADS, axis=0) * gmask16) * seg[k] for k in ks]
        y_diag = [_dot(_bf(scores[k]), _stack(xdt[k], bd16)) for k in ks]
        alast = [a[C_CHUNK - 1:C_CHUNK, :] for a in ac]
        delta = [gmask_t * _dot_tn(bc16[k], _bf(xdt[k] * jnp.exp(alast[k] - ac[k]))) for k in ks]
        st = st_ref[...]
        sts = []
        for k in ks:
            sts.append(_bf(st))
            st = jnp.exp(alast[k]) * st + delta[k]
        st_ref[...] = st
        y_off = [_dot(cc16[k], sts[k]) * jnp.exp(ac[k]) for k in ks]
        for k in ks:
            yv = (y_diag[k] + y_off[k] + dskip * xc[k]) * _silu(z_ref[0, rows[k], :])
            for gi in range(C_GROUPS):
                yg = yv[:, gi * half:(gi + 1) * half]
                ms = jnp.mean(yg * yg, axis=-1, keepdims=True)
                out_ref[0, rows[k], gi * half:(gi + 1) * half] = _bf(
                    yg * lax.rsqrt(ms + EPS) * og[:, gi * half:(gi + 1) * half])
        return carry

    lax.fori_loop(0, nchunk // C_UNROLL, chunk_group, 0)


def _ssd(cxbc, z_all, small, conv_w, conv_b, alog_row, dtb_row, dskip_row, og_row, bsz, seq, tb):
    c = _NPC
    assert (tb // C_CHUNK) % C_UNROLL == 0
    seq_blk = lambda w, col: pl.BlockSpec((1, tb, w), lambda b, j: (b, j, col))
    consts = [jnp.asarray(c["bd"], BF16), jnp.asarray(c["eye_cat"]), jnp.asarray(c["gmask"], BF16),
              jnp.asarray(c["gmask"].T.copy())]
    act = pltpu.VMEM((tb, BRANCH_WIDTH), F32)
    return pl.pallas_call(
        functools.partial(_ssd_kernel, tb=tb),
        grid=(bsz, seq // tb),
        in_specs=[seq_blk(3 * BRANCH_WIDTH, 0), seq_blk(BRANCH_WIDTH, 2), seq_blk(SMALL_W, 0),
                  _full_spec(conv_w.shape), _full_spec((1, 3 * BRANCH_WIDTH)), _full_spec((1, SMALL_W)),
                  _full_spec((1, SMALL_W)), _full_spec((1, BRANCH_WIDTH)), _full_spec((1, BRANCH_WIDTH))]
                 + [_full_spec(a.shape) for a in consts],
        out_specs=seq_blk(BRANCH_WIDTH, 0),
        out_shape=jax.ShapeDtypeStruct((bsz, seq, BRANCH_WIDTH), BF16),
        scratch_shapes=[pltpu.VMEM((SUBLANES, 3 * BRANCH_WIDTH), F32),
                        pltpu.VMEM((BRANCH_WIDTH, BRANCH_WIDTH), F32)] + [act] * 5,
        compiler_params=pltpu.CompilerParams(dimension_semantics=("arbitrary", "arbitrary"),
                                             vmem_limit_bytes=VMEM_LIMIT),
        name="ssd",
    )(cxbc.reshape(bsz, seq, -1), z_all.reshape(bsz, seq, -1), small.reshape(bsz, seq, -1),
      conv_w, conv_b, alog_row, dtb_row, dskip_row, og_row, *consts)


def _gla_kernel(qkv_ref, z_ref, small_ref, w2_ref, bg_ref, og_ref, ones_ref, dm_ref, cm_ref,
                out_ref,
                st_ref, qg_s, kg_s, k_s, cum_s, *, tb):
    @pl.when(pl.program_id(1) == 0)
    def _():
        st_ref[...] = jnp.zeros_like(st_ref)

    qkv = qkv_ref[0]
    q = qkv[:, :D_KEY] * (D_DK ** -0.5)
    k = qkv[:, D_KEY:2 * D_KEY]
    sm = small_ref[0]
    s_hi = _bf(sm)
    s_lo = _bf(sm - s_hi.astype(F32))
    w2 = w2_ref[...]
    w_hi = _bf(w2)
    w_lo = _bf(w2 - w_hi.astype(F32))
    gl = _dot(s_hi, w_hi) + _dot(s_lo, w_hi) + _dot(s_hi, w_lo) + bg_ref[...]
    glog = jax.nn.log_sigmoid(gl) * (1.0 / D_GATE_NORM)
    cum = _seg_cumsum(glog, D_CHUNK)
    qg_s[...] = q * jnp.exp(cum)
    kg_s[...] = k * jnp.exp(-cum)
    k_s[...] = k
    cum_s[...] = cum

    cmask = cm_ref[...] > 0.0
    klane = lax.broadcasted_iota(jnp.int32, (1, D_KEY), 1) // D_DK
    vlane = lax.broadcasted_iota(jnp.int32, (1, BRANCH_WIDTH), 1) // D_DV
    dmask = dm_ref[...]
    ones_bd = ones_ref[...]
    og = og_ref[...]
    per_group = D_GROUP // D_CHUNK

    cmask4 = jnp.concatenate([cmask] * D_HEADS, axis=0)

    def groups(it, carry):
        gs = range(D_UNROLL)
        rows = [pl.ds(pl.multiple_of((it * D_UNROLL + g) * D_GROUP, D_GROUP), D_GROUP) for g in gs]
        qgg = [qg_s[r, :] for r in rows]
        cc = [cum_s[r, :] for r in rows]
        vg = [_bf(qkv_ref[0, r, 2 * D_KEY:]) for r in rows]
        a = []
        for g in gs:
            zero = jnp.zeros_like(qgg[g])
            qs = _bf(jnp.concatenate([jnp.where(klane == h, qgg[g], zero) for h in range(D_HEADS)], axis=0))
            a.append(_dot_nt(qs, _bf(kg_s[rows[g], :])))
        av = [_dot(_bf(jnp.where(cmask4, a[g], 0.0)), vg[g]) for g in gs]
        chunks = [(g, c) for g in gs for c in range(per_group)]
        clast, delta = {}, {}
        for g, c in chunks:
            sl = slice(c * D_CHUNK, (c + 1) * D_CHUNK)
            ccc = cc[g][sl]
            clast[g, c] = ccc[D_CHUNK - 1:D_CHUNK, :]
            kd = k_s[rows[g], :][sl] * jnp.exp(clast[g, c] - ccc)
            delta[g, c] = dmask * _dot_tn(_bf(kd), vg[g][sl])
        dcol = {}
        for g in gs:
            dec_rows = jnp.concatenate([jnp.broadcast_to(jnp.exp(clast[g, c]), (D_CHUNK, D_KEY))
                                        for c in range(per_group)], axis=0)
            dec_t = dec_rows.T
            for c in range(per_group):
                dcol[g, c] = dec_t[:, c * D_CHUNK:c * D_CHUNK + 1]
        st = st_ref[...]
        sts = {}
        for g, c in chunks:
            sts[g, c] = _bf(st)
            st = dcol[g, c] * st + delta[g, c]
        st_ref[...] = st
        for g in gs:
            o_inter = jnp.concatenate([_dot(_bf(qgg[g][c * D_CHUNK:(c + 1) * D_CHUNK]), sts[g, c])
                                       for c in range(per_group)], axis=0)
            o_intra = av[g][:D_GROUP]
            for h in range(1, D_HEADS):
                o_intra = jnp.where(vlane == h, av[g][h * D_GROUP:(h + 1) * D_GROUP], o_intra)
            o = o_inter + o_intra
            ms = _seg_sum(o * o, ones_bd) * (1.0 / D_DV)
            on = o * lax.rsqrt(ms + EPS) * og
            out_ref[0, rows[g], :] = _bf(on * _silu(z_ref[0, rows[g], :]))
        return carry

    lax.fori_loop(0, tb // (D_GROUP * D_UNROLL), groups, 0)


def _gla(dqkv, z_all, small, w2pad, bg_row, og_row, bsz, seq, tb):
    c = _NPC
    seq_blk = lambda w, col: pl.BlockSpec((1, tb, w), lambda b, j: (b, j, col))
    consts = [jnp.asarray(c["bd"], BF16), jnp.asarray(c["dmask"]), jnp.asarray(c["cmask"])]
    return pl.pallas_call(
        functools.partial(_gla_kernel, tb=tb),
        grid=(bsz, seq // tb),
        in_specs=[seq_blk(2 * BRANCH_WIDTH, 0), seq_blk(BRANCH_WIDTH, 3), seq_blk(SMALL_W, 0),
                  _full_spec((SMALL_W, D_KEY)), _full_spec((1, D_KEY)), _full_spec((1, BRANCH_WIDTH))]
                 + [_full_spec(a.shape) for a in consts],
        out_specs=seq_blk(BRANCH_WIDTH, 0),
        out_shape=jax.ShapeDtypeStruct((bsz, seq, BRANCH_WIDTH), BF16),
        scratch_shapes=[pltpu.VMEM((D_KEY, BRANCH_WIDTH), F32)] + [pltpu.VMEM((tb, D_KEY), F32)] * 4,
        compiler_params=pltpu.CompilerParams(dimension_semantics=("arbitrary", "arbitrary"),
                                             vmem_limit_bytes=VMEM_LIMIT),
        name="gla",
    )(dqkv.reshape(bsz, seq, -1), z_all.reshape(bsz, seq, -1), small.reshape(bsz, seq, -1),
      w2pad, bg_row, og_row, *consts)


def _merge_kernel(x_ref, ng_ref, br0, br1, br2, br3, wg_ref, bg_ref, wb_ref, wo_ref, out_ref):
    x = x_ref[...]
    ms = jnp.mean(x * x, axis=-1, keepdims=True)
    hb = _bf(x * lax.rsqrt(ms + EPS) * ng_ref[...])
    merged = jnp.zeros(x.shape, F32)
    for i, br in enumerate((br0, br1, br2, br3)):
        gate = jax.nn.sigmoid(_dot(hb, wg_ref[i]) + bg_ref[i])
        merged = merged + gate * _dot(br[...], wb_ref[i])
    out_ref[...] = x + _dot(_bf(merged), wo_ref[...])


def _merge(xf, norm_g, brs, w_gate, b_gate, w_branch, w_out, tm):
    t = xf.shape[0]
    row = lambda w: pl.BlockSpec((tm, w), lambda i: (i, 0))
    return pl.pallas_call(
        _merge_kernel,
        grid=(t // tm,),
        in_specs=[row(D_MODEL), _full_spec((1, D_MODEL))] + [row(BRANCH_WIDTH)] * N_BRANCH
                 + [_const_spec(w_gate.shape), _full_spec(b_gate.shape), _const_spec(w_branch.shape),
                    _const_spec(w_out.shape)],
        out_specs=row(D_MODEL),
        out_shape=jax.ShapeDtypeStruct((t, D_MODEL), F32),
        compiler_params=pltpu.CompilerParams(dimension_semantics=("arbitrary",),
                                             vmem_limit_bytes=VMEM_LIMIT),
        name="merge",
    )(xf, norm_g, *brs, w_gate, b_gate, w_branch, w_out)


def _lane_row(vals, offset, width):
    return jnp.zeros((1, width), F32).at[0, offset:offset + vals.shape[0]].set(vals.astype(F32))


def _rope_rows():
    lane = np.arange(LANES) % HEAD_DIM
    inv_freq = ROPE_THETA ** (-jnp.arange(ROPE_HALF, dtype=F32) / ROPE_HALF)
    invf = jnp.where(lane < ROPE_DIM, inv_freq[lane % ROPE_HALF], 0.0).astype(F32)[None, :]
    sgn = np.where(lane < ROPE_HALF, -1.0, np.where(lane < ROPE_DIM, 1.0, 0.0)).astype(np.float32)[None, :]
    return invf, jnp.asarray(sgn)


def _w_in_layout():
    sizes = (A_QKV, A_QKV, A_QKV, BRANCH_WIDTH, 3 * BRANCH_WIDTH, BRANCH_WIDTH, N_HEADS, N_HEADS,
             3 * BRANCH_WIDTH, BRANCH_WIDTH, N_HEADS, D_KEY, D_KEY, BRANCH_WIDTH, BRANCH_WIDTH, D_GATE_RANK)
    names = ("a_q", "a_k", "a_v", "a_z", "b_qkv", "b_z", "b_beta", "b_a", "c_xbc", "c_z", "c_dt",
             "d_q", "d_k", "d_v", "d_z", "d_g")
    src = dict(zip(names, np.concatenate([[0], np.cumsum(sizes)[:-1]]).tolist()))
    total = int(sum(sizes))
    moves = [(W_AQ, src["a_q"], 3 * A_QKV + BRANCH_WIDTH),
             (W_Z + BRANCH_WIDTH, src["b_z"], BRANCH_WIDTH),
             (W_Z + 2 * BRANCH_WIDTH, src["c_z"], BRANCH_WIDTH),
             (W_Z + 3 * BRANCH_WIDTH, src["d_z"], BRANCH_WIDTH),
             (W_BQKV, src["b_qkv"], 3 * BRANCH_WIDTH),
             (W_CXBC, src["c_xbc"], 3 * BRANCH_WIDTH),
             (W_DQKV, src["d_q"], 2 * BRANCH_WIDTH)]
    small = []
    for lane0, name, width in ((SM_BETA, "b_beta", 2 * N_HEADS), (SM_CDT, "c_dt", N_HEADS),
                               (SM_DG, "d_g", D_GATE_RANK)):
        win = min(src[name] - lane0, total - SMALL_W)
        small.append((lane0 + width, win, (lane0 - (src[name] - win)) % SMALL_W))
    return total, moves, small


def _pack_kernel(w_ref, out_ref):
    _, moves, small = _w_in_layout()
    for dst, src, width in moves:
        out_ref[:, dst:dst + width] = _bf(w_ref[:, src:src + width])
    lane = lax.broadcasted_iota(jnp.int32, (1, SMALL_W), 1)
    blk = jnp.zeros((w_ref.shape[0], SMALL_W), F32)
    for lane_end, win, shift in reversed(small):
        piece = w_ref[:, win:win + SMALL_W]
        if shift:
            piece = pltpu.roll(piece, shift, axis=1)
        blk = jnp.where(lane < lane_end, piece, blk)
    out_ref[:, W_SMALL:W_SMALL + SMALL_W] = _bf(blk)


def _pack_w_in(w_in, layer):
    total, _, _ = _w_in_layout()
    assert w_in.shape[1:] == (D_MODEL, total)
    rows = D_MODEL // 4
    return pl.pallas_call(
        _pack_kernel,
        grid=(D_MODEL // rows,),
        in_specs=[pl.BlockSpec((None, rows, total), lambda i: (layer, i, 0))],
        out_specs=pl.BlockSpec((rows, W_TOTAL), lambda i: (i, 0)),
        out_shape=jax.ShapeDtypeStruct((D_MODEL, W_TOTAL), BF16),
        compiler_params=pltpu.CompilerParams(dimension_semantics=("arbitrary",)),
        name="pack_w_in",
    )(w_in)


def _layer(xf, pos, bsz, seq, p, tm, tb):
    invf, sgn = _rope_rows()
    ones_bd = jnp.asarray(_NPC["bd"], BF16)
    w_all = _pack_w_in(p["w_in_stacked"], p["layer"])
    ng = p["norm_g"][None, :]
    qn = jnp.tile(p["a_q_norm"], A_QKV // HEAD_DIM)[None, :]
    kn = jnp.tile(p["a_k_norm"], A_QKV // HEAD_DIM)[None, :]
    outs = _inproj(xf, pos, ng, w_all, qn, kn, invf, sgn, ones_bd, bsz, seq, tm)
    qs, ks, vs = outs[0:3], outs[3:6], outs[6:9]
    z_all, bqkv, cxbc, dqkv, small = outs[9:]

    os_, lses = [], []
    for gi in range(A_N_GROUPS):
        o, lse = _attn_group(qs[gi], ks[gi], vs[gi], gi)
        os_.append(o)
        lses.append(lse)
    br_a = _attn_combine(os_, lses, z_all, bsz, seq, tm)

    br_b = _gdn(bqkv, z_all, small, p["b_conv_w"],
                _lane_row(p["b_a_log"], SM_BDEC, SMALL_W), _lane_row(p["b_dt_bias"], SM_BDEC, SMALL_W),
                jnp.tile(p["b_out_norm"], N_HEADS)[None, :], bsz, seq, tb).reshape(bsz * seq, BRANCH_WIDTH)

    br_c = _ssd(cxbc, z_all, small, p["c_conv_w"], p["c_conv_b"][None, :],
                _lane_row(p["c_a_log"], SM_CDT, SMALL_W), _lane_row(p["c_dt_bias"], SM_CDT, SMALL_W),
                jnp.repeat(p["c_d_skip"], HEAD_DIM)[None, :], p["c_out_norm"][None, :],
                bsz, seq, tb).reshape(bsz * seq, BRANCH_WIDTH)

    w2pad = jnp.zeros((SMALL_W, D_KEY), F32).at[SM_DG:SM_DG + D_GATE_RANK, :].set(p["d_gate_w2"])
    br_d = _gla(dqkv, z_all, small, w2pad, p["d_gate_b"][None, :],
                jnp.tile(p["d_out_norm"], D_HEADS)[None, :], bsz, seq, tb).reshape(bsz * seq, BRANCH_WIDTH)

    return _merge(xf, ng, (br_a, br_b, br_c, br_d), _bf(p["w_gate"]), p["b_gate"][:, None, :],
                  _bf(p["w_branch"]), _bf(p["w_out"]), tm)


def _pick_tile(n, target):
    t = min(n, target)
    while n % t:
        t //= 2
    return t


def kernel(x, positions, norm_g, w_in, a_q_norm, a_k_norm, b_conv_w, b_a_log, b_dt_bias, b_out_norm,
           c_conv_w, c_conv_b, c_a_log, c_dt_bias, c_d_skip, c_out_norm, d_gate_w2, d_gate_b, d_out_norm,
           w_gate, b_gate, w_branch, w_out):
    bsz, seq, d_model = x.shape
    assert d_model == D_MODEL and seq % (2 * A_BLOCK * A_GROUPS[-1][1]) == 0
    xf = x.reshape(bsz * seq, d_model)
    tm = _pick_tile(seq, 512)
    assert tm % LANES == 0
    pos = positions.reshape(bsz * seq // tm, tm // LANES, LANES).astype(jnp.int32)
    tb = _pick_tile(seq, 512)
    names = ("norm_g", "w_in", "a_q_norm", "a_k_norm", "b_conv_w", "b_a_log", "b_dt_bias", "b_out_norm",
             "c_conv_w", "c_conv_b", "c_a_log", "c_dt_bias", "c_d_skip", "c_out_norm", "d_gate_w2",
             "d_gate_b", "d_out_norm", "w_gate", "b_gate", "w_branch", "w_out")
    stacked = (norm_g, w_in, a_q_norm, a_k_norm, b_conv_w, b_a_log, b_dt_bias, b_out_norm,
               c_conv_w, c_conv_b, c_a_log, c_dt_bias, c_d_skip, c_out_norm, d_gate_w2, d_gate_b,
               d_out_norm, w_gate, b_gate, w_branch, w_out)
    for layer in range(norm_g.shape[0]):
        p = {n: a[layer] for n, a in zip(names, stacked) if n != "w_in"}
        p["w_in_stacked"], p["layer"] = w_in, layer
        xf = _layer(xf, pos, bsz, seq, p, tm, tb)
    return xf.reshape(bsz, seq, d_model)
```

```python
import functools
import math

import numpy as np
import jax
import jax.numpy as jnp
from jax import lax
from jax.experimental import pallas as pl
from jax.experimental.pallas import tpu as pltpu

F32 = jnp.float32
BF16 = jnp.bfloat16

D_MODEL = 1024
N_BRANCH = 4
BRANCH_WIDTH = D_MODEL // N_BRANCH
HEAD_DIM = 64
N_HEADS = BRANCH_WIDTH // HEAD_DIM
EPS = 1e-6
ROPE_DIM = HEAD_DIM // 4
ROPE_HALF = ROPE_DIM // 2
ROPE_THETA = 500000.0
CONV_K = 4
PRO_ROWS = 128

A_GROUPS = ((128, 1), (512, 4), (2048, 16))
A_N_GROUPS = len(A_GROUPS)
A_QKV = A_N_GROUPS * BRANCH_WIDTH
A_BLOCK = 128
A_STEP_ROWS = 1024
A_UNROLL = 4

B_CHUNK = 64
B_UNROLL = 8
C_GROUPS = 2
C_STATE = 128
C_CHUNK = 64
C_UNROLL = 4
D_HEADS = 4
D_KEY = BRANCH_WIDTH // 2
D_DK = D_KEY // D_HEADS
D_DV = BRANCH_WIDTH // D_HEADS
D_GATE_RANK = 16
D_GATE_NORM = 16.0
D_CHUNK = 32
D_GROUP = 128
D_UNROLL = 2

LANES = 128
SUBLANES = 8
SLABS = BRANCH_WIDTH // LANES
SMALL_W = LANES
SM_BETA, SM_BDEC, SM_CDT, SM_DG = 0, 4, 8, 12

W_AQ, W_AK, W_AV = 0, A_QKV, 2 * A_QKV
W_Z = 3 * A_QKV
W_BQKV = W_Z + D_MODEL
W_CXBC = W_BQKV + 3 * BRANCH_WIDTH
W_DQKV = W_CXBC + 3 * BRANCH_WIDTH
W_SMALL = W_DQKV + 2 * BRANCH_WIDTH
W_TOTAL = W_SMALL + SMALL_W

VMEM_LIMIT = 56 * 1024 * 1024


def _bf(x):
    return x.astype(BF16)


def _dot(a, b):
    return jnp.dot(a, b, preferred_element_type=F32)


def _dot_nt(a, b):
    return lax.dot_general(a, b, (((1,), (1,)), ((), ())), preferred_element_type=F32)


def _dot_tn(a, b):
    return lax.dot_general(a, b, (((0,), (0,)), ((), ())), preferred_element_type=F32)


def _expand_heads(x, offset):
    rows = x.shape[0]
    head_lane = lax.broadcasted_iota(jnp.int32, (1, BRANCH_WIDTH), 1) // HEAD_DIM
    out = jnp.broadcast_to(x[:, offset:offset + 1], (rows, BRANCH_WIDTH))
    for h in range(1, N_HEADS):
        out = jnp.where(head_lane == h, jnp.broadcast_to(x[:, offset + h:offset + h + 1], (rows, BRANCH_WIDTH)), out)
    return out


def _seg_sum(x, ones_bd):
    hi = _bf(x)
    lo = _bf(x - hi.astype(F32))
    return _dot(hi, ones_bd) + _dot(lo, ones_bd)


def _seg_cumsum(x, seg):
    pos = lax.broadcasted_iota(jnp.int32, x.shape, 0) & (seg - 1)
    step = 1
    while step < seg:
        x = x + jnp.where(pos >= step, pltpu.roll(x, step, axis=0), 0.0)
        step *= 2
    return x


def _silu(x):
    h = 0.5 * x
    return h + h * jnp.tanh(h)


def _softplus(x):
    return jnp.maximum(x, 0.0) + jnp.log1p(jnp.exp(-jnp.abs(x)))


def _shift_rows(x, tail, s):
    r = pltpu.roll(x, s, axis=0)
    t = pltpu.roll(tail, s, axis=0)
    row = lax.broadcasted_iota(jnp.int32, tail.shape, 0)
    first = jnp.where(row < s, t, r[:SUBLANES])
    return jnp.concatenate([first, r[SUBLANES:]], axis=0)


def _causal_conv(x, tail, w):
    y = x * w[CONV_K - 1:CONV_K, :]
    for s in range(1, CONV_K):
        y = y + _shift_rows(x, tail, s) * w[CONV_K - 1 - s:CONV_K - s, :]
    return y


def _stack(y, bdmask16):
    return jnp.concatenate([_bf(y)] * N_HEADS, axis=0) * bdmask16


def _np_consts():
    r = np.arange(BRANCH_WIDTH)
    bd = (r[:, None] // HEAD_DIM == r[None, :] // HEAD_DIM).astype(np.float32)
    i = np.arange(B_CHUNK)[:, None]
    j = (r % HEAD_DIM)[None, :]
    xor = np.maximum(i ^ j, 1)
    lv = np.where(i > j, np.floor(np.log2(xor)).astype(np.int32), -1).astype(np.int32)
    eye_cat = (i == j).astype(np.float32)
    gmask = ((r[None, :] // C_STATE) == (r[:, None] // HEAD_DIM) // (N_HEADS // C_GROUPS)).astype(np.float32)
    dmask = ((np.arange(D_KEY)[:, None] // D_DK) == (r[None, :] // D_DV)).astype(np.float32)
    t = np.arange(D_GROUP)
    cmask = ((t[:, None] // D_CHUNK == t[None, :] // D_CHUNK) & (t[:, None] >= t[None, :])).astype(np.float32)

    return dict(bd=bd, lv=lv, eye_cat=eye_cat, gmask=gmask, dmask=dmask, cmask=cmask)


_NPC = _np_consts()


def _full_spec(shape):
    nd = len(shape)
    return pl.BlockSpec(shape, lambda *_: (0,) * nd)


def _const_spec(shape):
    nd = len(shape)
    return pl.BlockSpec(shape, lambda *_: (0,) * nd, pipeline_mode=pl.Buffered(1))


def _inproj_kernel(x_ref, pos_ref, ng_ref, w_ref, qn_ref, kn_ref, invf_ref, sgn_ref, ones_ref,
                   q0_ref, q1_ref, q2_ref, k0_ref, k1_ref, k2_ref, v0_ref, v1_ref, v2_ref,
                   z_ref, bqkv_ref, cxbc_ref, dqkv_ref, small_ref, perm_s, *, tm):
    x = x_ref[...]
    ms = jnp.mean(x * x, axis=-1, keepdims=True)
    hb = _bf(x * lax.rsqrt(ms + EPS) * ng_ref[...])

    def proj(lo, width):
        return _dot(hb, w_ref[:, lo:lo + width])

    pos_f = pos_ref[0].astype(F32)
    pos_col = jnp.concatenate([jnp.broadcast_to(pos_f[k:k + 1, :], (LANES, LANES)).T
                               for k in range(tm // LANES)], axis=0)
    ang = pos_col * invf_ref[...]
    cos = jnp.cos(ang)
    sin = jnp.sin(ang) * sgn_ref[...]
    lane = lax.broadcasted_iota(jnp.int32, (1, LANES), 1) & (HEAD_DIM - 1)
    first_half = lane < ROPE_HALF
    ones_bd = ones_ref[...]

    def emit(slabs, out_ref, which, g):
        dil = A_GROUPS[g][1]
        for s, v in enumerate(slabs):
            if dil == 1:
                out_ref[0, 0, :, s * LANES:(s + 1) * LANES] = _bf(v)
            else:
                perm_s[which, g - 1, s] = v
        if dil > 1:
            rows = tm // dil
            for s in range(len(slabs)):
                for r in range(dil):
                    out_ref[0, r, :, s * LANES:(s + 1) * LANES] = _bf(
                        perm_s[which, g - 1, s, pl.ds(r, rows, stride=dil), :])

    uqk = proj(W_AQ, 2 * A_QKV)
    uvz = proj(W_AV, A_QKV + D_MODEL)
    z_ref[...] = uvz[:, A_QKV:]
    for g, v_ref in enumerate((v0_ref, v1_ref, v2_ref)):
        emit([uvz[:, g * BRANCH_WIDTH + s * LANES:g * BRANCH_WIDTH + (s + 1) * LANES] for s in range(SLABS)],
             v_ref, 2, g)

    def qk_norm_rope(which, gain_ref, out_refs, scale):
        for g in range(A_N_GROUPS):
            c0 = which * A_QKV + g * BRANCH_WIDTH
            u = uqk[:, c0:c0 + BRANCH_WIDTH]
            ss = _seg_sum(u * u, ones_bd)
            gain = gain_ref[:, g * BRANCH_WIDTH:(g + 1) * BRANCH_WIDTH]
            un = u * lax.rsqrt(ss * (1.0 / HEAD_DIM) + EPS) * (gain * scale)
            slabs = []
            for s in range(SLABS):
                v = un[:, s * LANES:(s + 1) * LANES]
                partner = jnp.where(first_half, pltpu.roll(v, LANES - ROPE_HALF, axis=1),
                                    pltpu.roll(v, ROPE_HALF, axis=1))
                slabs.append(v * cos + partner * sin)
            emit(slabs, out_refs[g], which, g)

    qk_norm_rope(0, qn_ref, (q0_ref, q1_ref, q2_ref), HEAD_DIM ** -0.5)
    qk_norm_rope(1, kn_ref, (k0_ref, k1_ref, k2_ref), 1.0)
    rest = proj(W_BQKV, W_TOTAL - W_BQKV)
    bqkv_ref[...] = rest[:, :W_CXBC - W_BQKV]
    cxbc_ref[...] = rest[:, W_CXBC - W_BQKV:W_DQKV - W_BQKV]
    dqkv_ref[...] = rest[:, W_DQKV - W_BQKV:W_SMALL - W_BQKV]
    small_ref[...] = rest[:, W_SMALL - W_BQKV:]


def _inproj(xf, pos, norm_g, w_all, qn, kn, invf, sgn, ones_bd, bsz, seq, tm):
    t = xf.shape[0]
    tps = seq // tm
    row = lambda w: pl.BlockSpec((tm, w), lambda i: (i, 0))
    dils = [d for _, d in A_GROUPS]
    res_spec = lambda d: pl.BlockSpec((1, d, tm // d, BRANCH_WIDTH), lambda i: (i // tps, 0, i % tps, 0))
    res_shape = lambda d: jax.ShapeDtypeStruct((bsz, d, seq // d, BRANCH_WIDTH), BF16)
    flat = [(D_MODEL, F32), (3 * BRANCH_WIDTH, F32), (3 * BRANCH_WIDTH, F32), (2 * BRANCH_WIDTH, F32),
            (SMALL_W, F32)]
    return pl.pallas_call(
        functools.partial(_inproj_kernel, tm=tm),
        grid=(t // tm,),
        in_specs=[row(D_MODEL), pl.BlockSpec((1, tm // LANES, LANES), lambda i: (i, 0, 0)),
                  _full_spec((1, D_MODEL)), _const_spec((D_MODEL, W_TOTAL)),
                  _full_spec((1, A_QKV)), _full_spec((1, A_QKV)), _full_spec((1, LANES)),
                  _full_spec((1, LANES)), _full_spec((BRANCH_WIDTH, BRANCH_WIDTH))],
        out_specs=[res_spec(d) for d in dils] * 3 + [row(w) for w, _ in flat],
        out_shape=[res_shape(d) for d in dils] * 3 + [jax.ShapeDtypeStruct((t, w), dt) for w, dt in flat],
        scratch_shapes=[pltpu.VMEM((3, A_N_GROUPS - 1, SLABS, tm, LANES), F32)],
        compiler_params=pltpu.CompilerParams(dimension_semantics=("arbitrary",),
                                             vmem_limit_bytes=VMEM_LIMIT),
        name="inproj",
    )(xf, pos, norm_g, w_all, qn, kn, invf, sgn, ones_bd)


def _attn_kernel(q_ref, k_ref, v_ref, bias_ref, o_ref, lse_ref, *, n, rb):
    nb = n // A_BLOCK
    head_lane = lax.broadcasted_iota(jnp.int32, (1, BRANCH_WIDTH), 1) // HEAD_DIM
    shift = nb.bit_length() - 1
    assert nb == 1 << shift and (rb * nb) % A_UNROLL == 0

    def blocks(it, carry):
        units = [it * A_UNROLL + b for b in range(A_UNROLL)]
        rr = [u >> shift for u in units]
        js = [u & (nb - 1) for u in units]
        q0 = [pl.multiple_of(j * A_BLOCK, A_BLOCK) for j in js]
        k0 = [pl.multiple_of(jnp.maximum(j - 1, 0) * A_BLOCK, A_BLOCK) for j in js]
        s = []
        for b in range(A_UNROLL):
            q = q_ref[0, rr[b], pl.ds(q0[b], A_BLOCK), :]
            zero = jnp.zeros_like(q)
            qs = jnp.concatenate([jnp.where(head_lane == h, q, zero) for h in range(N_HEADS)], axis=0)
            s.append(_dot_nt(qs, k_ref[0, rr[b], pl.ds(k0[b], 2 * A_BLOCK), :])
                     + bias_ref[jnp.minimum(js[b], 1)])
        m = [jnp.max(x, axis=-1, keepdims=True) for x in s]
        p = [jnp.exp(x - mx) for x, mx in zip(s, m)]
        l = [jnp.sum(x, axis=-1, keepdims=True) for x in p]
        pv = [_dot(_bf(p[b]), v_ref[0, rr[b], pl.ds(k0[b], 2 * A_BLOCK), :]) * (1.0 / l[b])
              for b in range(A_UNROLL)]
        for b in range(A_UNROLL):
            lse = m[b] + jnp.log(l[b])
            o_acc = pv[b][:A_BLOCK]
            lse_acc = jnp.broadcast_to(lse[:A_BLOCK], (A_BLOCK, BRANCH_WIDTH))
            for h in range(1, N_HEADS):
                o_acc = jnp.where(head_lane == h, pv[b][h * A_BLOCK:(h + 1) * A_BLOCK], o_acc)
                lse_acc = jnp.where(head_lane == h, lse[h * A_BLOCK:(h + 1) * A_BLOCK], lse_acc)
            o_ref[0, rr[b], pl.ds(q0[b], A_BLOCK), :] = _bf(o_acc)
            lse_ref[0, rr[b], pl.ds(q0[b], A_BLOCK), :] = lse_acc
        return carry

    lax.fori_loop(0, rb * nb // A_UNROLL, blocks, 0)


def _attn_group(q, k, v, gi):
    window, dil = A_GROUPS[gi]
    bsz, _, n, _ = q.shape
    assert n % (2 * A_BLOCK) == 0
    rb = min(dil, max(1, A_STEP_ROWS // n))
    win = window // dil
    qi = (np.arange(N_HEADS * A_BLOCK) % A_BLOCK)[:, None]
    kj = np.arange(2 * A_BLOCK)[None, :]
    band = lambda off: np.where((qi - kj + off >= 0) & (qi - kj + off <= win), 0.0, -np.inf).astype(np.float32)
    bias = jnp.asarray(np.stack([band(0), band(A_BLOCK)]))
    blk = lambda w: pl.BlockSpec((1, rb, n, w), lambda b, r: (b, r, 0, 0))
    return pl.pallas_call(
        functools.partial(_attn_kernel, n=n, rb=rb),
        grid=(bsz, dil // rb),
        in_specs=[blk(BRANCH_WIDTH)] * 3 + [_full_spec(bias.shape)],
        out_specs=[blk(BRANCH_WIDTH), blk(BRANCH_WIDTH)],
        out_shape=[jax.ShapeDtypeStruct((bsz, dil, n, BRANCH_WIDTH), BF16),
                   jax.ShapeDtypeStruct((bsz, dil, n, BRANCH_WIDTH), F32)],
        compiler_params=pltpu.CompilerParams(dimension_semantics=("arbitrary", "arbitrary"),
                                             vmem_limit_bytes=VMEM_LIMIT),
        name=f"attn_d{dil}",
    )(q, k, v, bias)


def _attn_combine_kernel(o0, o1, o2, l0, l1, l2, z_ref, out_ref, lse_s, o_s, *, tm):
    dils = [d for _, d in A_GROUPS]

    def token_major(ref, dil, scratch, sl, cast):
        if dil == 1:
            return cast(ref[0, 0, :, sl])
        rows = tm // dil
        for r in range(dil):
            scratch[pl.ds(r, rows, stride=dil), :] = cast(ref[0, r, :, sl])
        return scratch[...]

    for s in range(SLABS):
        sl = slice(s * LANES, (s + 1) * LANES)
        lses = [token_major(lref, dil, lse_s.at[g], sl, lambda a: a)
                for g, (lref, dil) in enumerate(zip((l0, l1, l2), dils))]
        m = jnp.maximum(jnp.maximum(lses[0], lses[1]), lses[2])
        es = [jnp.exp(l - m) for l in lses]
        acc = jnp.zeros((tm, LANES), F32)
        for g, (oref, dil) in enumerate(zip((o0, o1, o2), dils)):
            acc = acc + token_major(oref, dil, o_s, sl, lambda a: a.astype(F32)) * es[g]
        acc = acc / (es[0] + es[1] + es[2])
        out_ref[:, sl] = _bf(acc * _silu(z_ref[:, sl]))


def _attn_combine(os_, lses, z_all, bsz, seq, tm):
    t = z_all.shape[0]
    tps = seq // tm
    dils = [d for _, d in A_GROUPS]
    res = lambda d, w: pl.BlockSpec((1, d, tm // d, w), lambda i: (i // tps, 0, i % tps, 0))
    row = pl.BlockSpec((tm, BRANCH_WIDTH), lambda i: (i, 0))
    return pl.pallas_call(
        functools.partial(_attn_combine_kernel, tm=tm),
        grid=(t // tm,),
        in_specs=[res(d, BRANCH_WIDTH) for d in dils] * 2 + [row],
        out_specs=row,
        out_shape=jax.ShapeDtypeStruct((t, BRANCH_WIDTH), BF16),
        scratch_shapes=[pltpu.VMEM((A_N_GROUPS, tm, LANES), F32), pltpu.VMEM((tm, LANES), F32)],
        compiler_params=pltpu.CompilerParams(dimension_semantics=("arbitrary",)),
        name="attn_combine",
    )(*os_, *lses, z_all)


def _gdn_kernel(qkv_ref, z_ref, small_ref, cw_ref, alog_ref, dtb_ref, og_ref,
                ones_ref, bd_ref, lv_ref, eye_ref,
                out_ref,
                tail_ref, st_ref, q_s, k_s, vb_s, kb_s, ce_s, qe_s, oi_s, corr_s, add_s, *, tb):
    nchunk = tb // B_CHUNK

    @pl.when(pl.program_id(1) == 0)
    def _():
        tail_ref[...] = jnp.zeros_like(tail_ref)
        st_ref[...] = jnp.zeros_like(st_ref)

    ones_bd = ones_ref[...]
    bd16 = ones_bd
    bdmask = bd_ref[...]
    lv = lv_ref[...]
    eye_cat = eye_ref[...]

    def prologue(r, carry):
        rows = pl.ds(pl.multiple_of(r * PRO_ROWS, PRO_ROWS), PRO_ROWS)
        halo = pl.ds(pl.multiple_of(jnp.maximum(r * PRO_ROWS - SUBLANES, 0), SUBLANES), SUBLANES)
        tail = jnp.where(r == 0, tail_ref[...], qkv_ref[0, halo, :])
        sm = small_ref[0, rows, :]
        g = -jnp.exp(alog_ref[...]) * _softplus(sm + dtb_ref[...])
        ce_s[rows, :] = _expand_heads(_seg_cumsum(g, B_CHUNK), SM_BDEC)
        beta = _expand_heads(jax.nn.sigmoid(sm), SM_BETA)
        for part in range(3):
            cols = slice(part * BRANCH_WIDTH, (part + 1) * BRANCH_WIDTH)
            y = _silu(_causal_conv(qkv_ref[0, rows, cols], tail[:, cols], cw_ref[:, cols]))
            if part == 0:
                q_s[rows, :] = y * lax.rsqrt(_seg_sum(y * y, ones_bd) + EPS) * (HEAD_DIM ** -0.5)
            elif part == 1:
                kn = y * lax.rsqrt(_seg_sum(y * y, ones_bd) + EPS)
                k_s[rows, :] = kn
                kb_s[rows, :] = kn * beta
            else:
                vb_s[rows, :] = y * beta
        return carry

    lax.fori_loop(0, tb // PRO_ROWS, prologue, 0)
    tail_ref[...] = qkv_ref[0, tb - SUBLANES:, :]

    il = lax.broadcasted_iota(jnp.int32, (B_CHUNK, BRANCH_WIDTH), 0)
    jl = lax.broadcasted_iota(jnp.int32, (B_CHUNK, BRANCH_WIDTH), 1) & (HEAD_DIM - 1)
    causal = il >= jl

    def intra_group(i, carry):
        ks = range(B_UNROLL)
        rows = [pl.ds(pl.multiple_of((i * B_UNROLL + k) * B_CHUNK, B_CHUNK), B_CHUNK) for k in ks]
        cc = [ce_s[r, :] for r in rows]
        kc = [k_s[r, :] for r in rows]
        qc = [q_s[r, :] for r in rows]
        kb = [kb_s[r, :] for r in rows]
        gamma = [jnp.exp(jnp.where(causal, c_ - jnp.sum(c_ * eye_cat, axis=0, keepdims=True), -jnp.inf))
                 for c_ in cc]
        prod = [_dot_nt(_bf(jnp.concatenate([kb[k], qc[k]], axis=0)), _stack(kc[k], bd16)) for k in ks]
        a_kk = [jnp.where(il > jl, prod[k][:B_CHUNK] * gamma[k], 0.0) for k in ks]
        a_qk = [_bf(prod[k][B_CHUNK:] * gamma[k]) for k in ks]
        t = [eye_cat - jnp.where(lv == 0, a, 0.0) for a in a_kk]
        for lb in range(1, int(math.log2(B_CHUNK))):
            x = [_dot(_bf(t[k]), _stack(jnp.where(lv == lb, a_kk[k], 0.0), bd16)) for k in ks]
            t = [t[k] - _dot(_bf(x[k]), _stack(t[k], bd16)) for k in ks]
        ecum = [jnp.exp(c_) for c_ in cc]
        uw = [_dot(_bf(t[k]), jnp.concatenate([_stack(vb_s[rows[k], :], bd16),
                                               _stack(kb[k] * ecum[k], bd16)], axis=1)) for k in ks]
        u = [a[:, :BRANCH_WIDTH] for a in uw]
        w = [a[:, BRANCH_WIDTH:] for a in uw]
        both = [_dot(a_qk[k], jnp.concatenate([_stack(u[k], bd16), _stack(w[k], bd16)], axis=1)) for k in ks]
        k_dec = [_bf(kc[k] * jnp.exp(cc[k][B_CHUNK - 1:B_CHUNK, :] - cc[k])) for k in ks]
        wu = [_dot_tn(_bf(jnp.concatenate([w[k], u[k]], axis=1)), k_dec[k]) for k in ks]
        for k in ks:
            oi_s[rows[k], :] = both[k][:, :BRANCH_WIDTH]
            qe_s[rows[k], :] = qc[k] * ecum[k] - both[k][:, BRANCH_WIDTH:]
            corr_s[i * B_UNROLL + k] = _bf(-(wu[k][:BRANCH_WIDTH] * bdmask))
            add_s[i * B_UNROLL + k] = wu[k][BRANCH_WIDTH:] * bdmask
        return carry

    lax.fori_loop(0, nchunk // B_UNROLL, intra_group, 0)

    og = og_ref[...]
    st = st_ref[...]
    for c in range(nchunk):
        rows = slice(c * B_CHUNK, (c + 1) * B_CHUNK)
        st16 = _bf(st)
        clast = ce_s[(c + 1) * B_CHUNK - 1:(c + 1) * B_CHUNK, :]
        st = jnp.exp(clast) * st + _dot(st16, corr_s[c]) + add_s[c]
        o = _dot_nt(_bf(qe_s[rows, :]), st16) + oi_s[rows, :]
        ms = _seg_sum(o * o, ones_bd) * (1.0 / HEAD_DIM)
        on = o * lax.rsqrt(ms + EPS) * og
        out_ref[0, rows, :] = _bf(on * _silu(z_ref[0, rows, :]))
    st_ref[...] = st


def _gdn(bqkv, z_all, small, conv_w, alog_row, dtb_row, og_row, bsz, seq, tb):
    c = _NPC
    nchunk = tb // B_CHUNK
    assert nchunk % B_UNROLL == 0
    seq_blk = lambda w, col: pl.BlockSpec((1, tb, w), lambda b, j: (b, j, col))
    consts = [jnp.asarray(c["bd"], BF16), jnp.asarray(c["bd"]), jnp.asarray(c["lv"]), jnp.asarray(c["eye_cat"])]
    act = pltpu.VMEM((tb, BRANCH_WIDTH), F32)
    sq = (nchunk, BRANCH_WIDTH, BRANCH_WIDTH)
    return pl.pallas_call(
        functools.partial(_gdn_kernel, tb=tb),
        grid=(bsz, seq // tb),
        in_specs=[seq_blk(3 * BRANCH_WIDTH, 0), seq_blk(BRANCH_WIDTH, 1), seq_blk(SMALL_W, 0),
                  _full_spec(conv_w.shape), _full_spec((1, SMALL_W)), _full_spec((1, SMALL_W)),
                  _full_spec((1, BRANCH_WIDTH))] + [_full_spec(a.shape) for a in consts],
        out_specs=seq_blk(BRANCH_WIDTH, 0),
        out_shape=jax.ShapeDtypeStruct((bsz, seq, BRANCH_WIDTH), BF16),
        scratch_shapes=[pltpu.VMEM((SUBLANES, 3 * BRANCH_WIDTH), F32),
                        pltpu.VMEM((BRANCH_WIDTH, BRANCH_WIDTH), F32)] + [act] * 7
                       + [pltpu.VMEM(sq, BF16), pltpu.VMEM(sq, F32)],
        compiler_params=pltpu.CompilerParams(dimension_semantics=("arbitrary", "arbitrary"),
                                             vmem_limit_bytes=VMEM_LIMIT),
        name="gdn",
    )(bqkv.reshape(bsz, seq, -1), z_all.reshape(bsz, seq, -1), small.reshape(bsz, seq, -1),
      conv_w, alog_row, dtb_row, og_row, *consts)


def _ssd_kernel(xbc_ref, z_ref, small_ref, cw_ref, cb_ref, alog_ref, dtb_ref, dskip_ref, og_ref,
                bd_ref, eye_ref, gm_ref, gmt_ref,
                out_ref,
                tail_ref, st_ref, xs_s, bm_s, cm_s, dt_s, ac_s, *, tb):
    nchunk = tb // C_CHUNK

    @pl.when(pl.program_id(1) == 0)
    def _():
        tail_ref[...] = jnp.zeros_like(tail_ref)
        st_ref[...] = jnp.zeros_like(st_ref)

    bd16 = bd_ref[...]
    eye_cat = eye_ref[...]
    gmask16 = gm_ref[...]
    gmask_t = gmt_ref[...]

    def prologue(r, carry):
        rows = pl.ds(pl.multiple_of(r * PRO_ROWS, PRO_ROWS), PRO_ROWS)
        halo = pl.ds(pl.multiple_of(jnp.maximum(r * PRO_ROWS - SUBLANES, 0), SUBLANES), SUBLANES)
        tail = jnp.where(r == 0, tail_ref[...], xbc_ref[0, halo, :])
        for part, dst in enumerate((xs_s, bm_s, cm_s)):
            cols = slice(part * BRANCH_WIDTH, (part + 1) * BRANCH_WIDTH)
            dst[rows, :] = _silu(_causal_conv(xbc_ref[0, rows, cols], tail[:, cols], cw_ref[:, cols])
                                 + cb_ref[:, cols])
        return carry

    lax.fori_loop(0, tb // PRO_ROWS, prologue, 0)
    tail_ref[...] = xbc_ref[0, tb - SUBLANES:, :]
    dt = _softplus(small_ref[0] + dtb_ref[...])
    a = -jnp.exp(alog_ref[...]) * dt
    dt_s[...] = _expand_heads(dt, SM_CDT)
    ac_s[...] = _expand_heads(_seg_cumsum(a, C_CHUNK), SM_CDT)

    il = lax.broadcasted_iota(jnp.int32, (C_CHUNK, BRANCH_WIDTH), 0)
    jl = lax.broadcasted_iota(jnp.int32, (C_CHUNK, BRANCH_WIDTH), 1) & (HEAD_DIM - 1)
    causal = il >= jl
    og = og_ref[...]
    dskip = dskip_ref[...]
    half = BRANCH_WIDTH // C_GROUPS

    def chunk_group(i, carry):
        ks = range(C_UNROLL)
        rows = [pl.ds(pl.multiple_of((i * C_UNROLL + k) * C_CHUNK, C_CHUNK), C_CHUNK) for k in ks]
        xc = [xs_s[r, :] for r in rows]
        ac = [ac_s[r, :] for r in rows]
        cc16 = [_bf(cm_s[r, :]) for r in rows]
        bc16 = [_bf(bm_s[r, :]) for r in rows]
        xdt = [xc[k] * dt_s[rows[k], :] for k in ks]
        seg = [jnp.exp(jnp.where(causal, a - jnp.sum(a * eye_cat, axis=0, keepdims=True), -jnp.inf)) for a in ac]
        scores = [_dot_nt(cc16[k], jnp.concatenate([bc16[k]] * N_HEADS, axis=0) * gmask16) * seg[k] for k in ks]
        y_diag = [_dot(_bf(scores[k]), _stack(xdt[k], bd16)) for k in ks]
        alast = [a[C_CHUNK - 1:C_CHUNK, :] for a in ac]
        delta = [gmask_t * _dot_tn(bc16[k], _bf(xdt[k] * jnp.exp(alast[k] - ac[k]))) for k in ks]
        st = st_ref[...]
        sts = []
        for k in ks:
            sts.append(_bf(st))
            st = jnp.exp(alast[k]) * st + delta[k]
        st_ref[...] = st
        y_off = [_dot(cc16[k], sts[k]) * jnp.exp(ac[k]) for k in ks]
        for k in ks:
            yv = (y_diag[k] + y_off[k] + dskip * xc[k]) * _silu(z_ref[0, rows[k], :])
            for gi in range(C_GROUPS):
                yg = yv[:, gi * half:(gi + 1) * half]
                ms = jnp.mean(yg * yg, axis=-1, keepdims=True)
                out_ref[0, rows[k], gi * half:(gi + 1) * half] = _bf(
                    yg * lax.rsqrt(ms + EPS) * og[:, gi * half:(gi + 1) * half])
        return carry

    lax.fori_loop(0, nchunk // C_UNROLL, chunk_group, 0)


def _ssd(cxbc, z_all, small, conv_w, conv_b, alog_row, dtb_row, dskip_row, og_row, bsz, seq, tb):
    c = _NPC
    assert (tb // C_CHUNK) % C_UNROLL == 0
    seq_blk = lambda w, col: pl.BlockSpec((1, tb, w), lambda b, j: (b, j, col))
    consts = [jnp.asarray(c["bd"], BF16), jnp.asarray(c["eye_cat"]), jnp.asarray(c["gmask"], BF16),
              jnp.asarray(c["gmask"].T.copy())]
    act = pltpu.VMEM((tb, BRANCH_WIDTH), F32)
    return pl.pallas_call(
        functools.partial(_ssd_kernel, tb=tb),
        grid=(bsz, seq // tb),
        in_specs=[seq_blk(3 * BRANCH_WIDTH, 0), seq_blk(BRANCH_WIDTH, 2), seq_blk(SMALL_W, 0),
                  _full_spec(conv_w.shape), _full_spec((1, 3 * BRANCH_WIDTH)), _full_spec((1, SMALL_W)),
                  _full_spec((1, SMALL_W)), _full_spec((1, BRANCH_WIDTH)), _full_spec((1, BRANCH_WIDTH))]
                 + [_full_spec(a.shape) for a in consts],
        out_specs=seq_blk(BRANCH_WIDTH, 0),
        out_shape=jax.ShapeDtypeStruct((bsz, seq, BRANCH_WIDTH), BF16),
        scratch_shapes=[pltpu.VMEM((SUBLANES, 3 * BRANCH_WIDTH), F32),
                        pltpu.VMEM((BRANCH_WIDTH, BRANCH_WIDTH), F32)] + [act] * 5,
        compiler_params=pltpu.CompilerParams(dimension_semantics=("arbitrary", "arbitrary"),
                                             vmem_limit_bytes=VMEM_LIMIT),
        name="ssd",
    )(cxbc.reshape(bsz, seq, -1), z_all.reshape(bsz, seq, -1), small.reshape(bsz, seq, -1),
      conv_w, conv_b, alog_row, dtb_row, dskip_row, og_row, *consts)


def _gla_kernel(qkv_ref, z_ref, small_ref, w2_ref, bg_ref, og_ref, ones_ref, dm_ref, cm_ref,
                out_ref,
                st_ref, qg_s, kg_s, k_s, cum_s, *, tb):
    @pl.when(pl.program_id(1) == 0)
    def _():
        st_ref[...] = jnp.zeros_like(st_ref)

    qkv = qkv_ref[0]
    q = qkv[:, :D_KEY] * (D_DK ** -0.5)
    k = qkv[:, D_KEY:2 * D_KEY]
    sm = small_ref[0]
    s_hi = _bf(sm)
    s_lo = _bf(sm - s_hi.astype(F32))
    w2 = w2_ref[...]
    w_hi = _bf(w2)
    w_lo = _bf(w2 - w_hi.astype(F32))
    gl = _dot(s_hi, w_hi) + _dot(s_lo, w_hi) + _dot(s_hi, w_lo) + bg_ref[...]
    glog = jax.nn.log_sigmoid(gl) * (1.0 / D_GATE_NORM)
    cum = _seg_cumsum(glog, D_CHUNK)
    qg_s[...] = q * jnp.exp(cum)
    kg_s[...] = k * jnp.exp(-cum)
    k_s[...] = k
    cum_s[...] = cum

    cmask = cm_ref[...] > 0.0
    klane = lax.broadcasted_iota(jnp.int32, (1, D_KEY), 1) // D_DK
    vlane = lax.broadcasted_iota(jnp.int32, (1, BRANCH_WIDTH), 1) // D_DV
    dmask = dm_ref[...]
    ones_bd = ones_ref[...]
    og = og_ref[...]
    per_group = D_GROUP // D_CHUNK

    cmask4 = jnp.concatenate([cmask] * D_HEADS, axis=0)

    def groups(it, carry):
        gs = range(D_UNROLL)
        rows = [pl.ds(pl.multiple_of((it * D_UNROLL + g) * D_GROUP, D_GROUP), D_GROUP) for g in gs]
        qgg = [qg_s[r, :] for r in rows]
        cc = [cum_s[r, :] for r in rows]
        vg = [_bf(qkv_ref[0, r, 2 * D_KEY:]) for r in rows]
        a = []
        for g in gs:
            zero = jnp.zeros_like(qgg[g])
            qs = _bf(jnp.concatenate([jnp.where(klane == h, qgg[g], zero) for h in range(D_HEADS)], axis=0))
            a.append(_dot_nt(qs, _bf(kg_s[rows[g], :])))
        av = [_dot(_bf(jnp.where(cmask4, a[g], 0.0)), vg[g]) for g in gs]
        chunks = [(g, c) for g in gs for c in range(per_group)]
        clast, delta = {}, {}
        for g, c in chunks:
            sl = slice(c * D_CHUNK, (c + 1) * D_CHUNK)
            ccc = cc[g][sl]
            clast[g, c] = ccc[D_CHUNK - 1:D_CHUNK, :]
            kd = k_s[rows[g], :][sl] * jnp.exp(clast[g, c] - ccc)
            delta[g, c] = dmask * _dot_tn(_bf(kd), vg[g][sl])
        dcol = {}
        for g in gs:
            dec_rows = jnp.concatenate([jnp.broadcast_to(jnp.exp(clast[g, c]), (D_CHUNK, D_KEY))
                                        for c in range(per_group)], axis=0)
            dec_t = dec_rows.T
            for c in range(per_group):
                dcol[g, c] = dec_t[:, c * D_CHUNK:c * D_CHUNK + 1]
        st = st_ref[...]
        sts = {}
        for g, c in chunks:
            sts[g, c] = _bf(st)
            st = dcol[g, c] * st + delta[g, c]
        st_ref[...] = st
        for g in gs:
            o_inter = jnp.concatenate([_dot(_bf(qgg[g][c * D_CHUNK:(c + 1) * D_CHUNK]), sts[g, c])
                                       for c in range(per_group)], axis=0)
            o_intra = av[g][:D_GROUP]
            for h in range(1, D_HEADS):
                o_intra = jnp.where(vlane == h, av[g][h * D_GROUP:(h + 1) * D_GROUP], o_intra)
            o = o_inter + o_intra
            ms = _seg_sum(o * o, ones_bd) * (1.0 / D_DV)
            on = o * lax.rsqrt(ms + EPS) * og
            out_ref[0, rows[g], :] = _bf(on * _silu(z_ref[0, rows[g], :]))
        return carry

    lax.fori_loop(0, tb // (D_GROUP * D_UNROLL), groups, 0)


def _gla(dqkv, z_all, small, w2pad, bg_row, og_row, bsz, seq, tb):
    c = _NPC
    seq_blk = lambda w, col: pl.BlockSpec((1, tb, w), lambda b, j: (b, j, col))
    consts = [jnp.asarray(c["bd"], BF16), jnp.asarray(c["dmask"]), jnp.asarray(c["cmask"])]
    return pl.pallas_call(
        functools.partial(_gla_kernel, tb=tb),
        grid=(bsz, seq // tb),
        in_specs=[seq_blk(2 * BRANCH_WIDTH, 0), seq_blk(BRANCH_WIDTH, 3), seq_blk(SMALL_W, 0),
                  _full_spec((SMALL_W, D_KEY)), _full_spec((1, D_KEY)), _full_spec((1, BRANCH_WIDTH))]
                 + [_full_spec(a.shape) for a in consts],
        out_specs=seq_blk(BRANCH_WIDTH, 0),
        out_shape=jax.ShapeDtypeStruct((bsz, seq, BRANCH_WIDTH), BF16),
        scratch_shapes=[pltpu.VMEM((D_KEY, BRANCH_WIDTH), F32)] + [pltpu.VMEM((tb, D_KEY), F32)] * 4,
        compiler_params=pltpu.CompilerParams(dimension_semantics=("arbitrary", "arbitrary"),
                                             vmem_limit_bytes=VMEM_LIMIT),
        name="gla",
    )(dqkv.reshape(bsz, seq, -1), z_all.reshape(bsz, seq, -1), small.reshape(bsz, seq, -1),
      w2pad, bg_row, og_row, *consts)


def _merge_kernel(x_ref, ng_ref, br0, br1, br2, br3, wg_ref, bg_ref, wb_ref, wo_ref, out_ref):
    x = x_ref[...]
    ms = jnp.mean(x * x, axis=-1, keepdims=True)
    hb = _bf(x * lax.rsqrt(ms + EPS) * ng_ref[...])
    merged = jnp.zeros(x.shape, F32)
    for i, br in enumerate((br0, br1, br2, br3)):
        gate = jax.nn.sigmoid(_dot(hb, wg_ref[i]) + bg_ref[i])
        merged = merged + gate * _dot(br[...], wb_ref[i])
    out_ref[...] = x + _dot(_bf(merged), wo_ref[...])


def _merge(xf, norm_g, brs, w_gate, b_gate, w_branch, w_out, tm):
    t = xf.shape[0]
    row = lambda w: pl.BlockSpec((tm, w), lambda i: (i, 0))
    return pl.pallas_call(
        _merge_kernel,
        grid=(t // tm,),
        in_specs=[row(D_MODEL), _full_spec((1, D_MODEL))] + [row(BRANCH_WIDTH)] * N_BRANCH
                 + [_const_spec(w_gate.shape), _full_spec(b_gate.shape), _const_spec(w_branch.shape),
                    _const_spec(w_out.shape)],
        out_specs=row(D_MODEL),
        out_shape=jax.ShapeDtypeStruct((t, D_MODEL), F32),
        compiler_params=pltpu.CompilerParams(dimension_semantics=("arbitrary",),
                                             vmem_limit_bytes=VMEM_LIMIT),
        name="merge",
    )(xf, norm_g, *brs, w_gate, b_gate, w_branch, w_out)


def _lane_row(vals, offset, width):
    return jnp.zeros((1, width), F32).at[0, offset:offset + vals.shape[0]].set(vals.astype(F32))


def _rope_rows():
    lane = np.arange(LANES) % HEAD_DIM
    inv_freq = ROPE_THETA ** (-jnp.arange(ROPE_HALF, dtype=F32) / ROPE_HALF)
    invf = jnp.where(lane < ROPE_DIM, inv_freq[lane % ROPE_HALF], 0.0).astype(F32)[None, :]
    sgn = np.where(lane < ROPE_HALF, -1.0, np.where(lane < ROPE_DIM, 1.0, 0.0)).astype(np.float32)[None, :]
    return invf, jnp.asarray(sgn)


def _w_in_layout():
    sizes = (A_QKV, A_QKV, A_QKV, BRANCH_WIDTH, 3 * BRANCH_WIDTH, BRANCH_WIDTH, N_HEADS, N_HEADS,
             3 * BRANCH_WIDTH, BRANCH_WIDTH, N_HEADS, D_KEY, D_KEY, BRANCH_WIDTH, BRANCH_WIDTH, D_GATE_RANK)
    names = ("a_q", "a_k", "a_v", "a_z", "b_qkv", "b_z", "b_beta", "b_a", "c_xbc", "c_z", "c_dt",
             "d_q", "d_k", "d_v", "d_z", "d_g")
    src = dict(zip(names, np.concatenate([[0], np.cumsum(sizes)[:-1]]).tolist()))
    total = int(sum(sizes))
    moves = [(W_AQ, src["a_q"], 3 * A_QKV + BRANCH_WIDTH),
             (W_Z + BRANCH_WIDTH, src["b_z"], BRANCH_WIDTH),
             (W_Z + 2 * BRANCH_WIDTH, src["c_z"], BRANCH_WIDTH),
             (W_Z + 3 * BRANCH_WIDTH, src["d_z"], BRANCH_WIDTH),
             (W_BQKV, src["b_qkv"], 3 * BRANCH_WIDTH),
             (W_CXBC, src["c_xbc"], 3 * BRANCH_WIDTH),
             (W_DQKV, src["d_q"], 2 * BRANCH_WIDTH)]
    small = []
    for lane0, name, width in ((SM_BETA, "b_beta", 2 * N_HEADS), (SM_CDT, "c_dt", N_HEADS),
                               (SM_DG, "d_g", D_GATE_RANK)):
        win = min(src[name] - lane0, total - SMALL_W)
        small.append((lane0 + width, win, (lane0 - (src[name] - win)) % SMALL_W))
    return total, moves, small


def _pack_kernel(w_ref, out_ref):
    _, moves, small = _w_in_layout()
    for dst, src, width in moves:
        out_ref[:, dst:dst + width] = _bf(w_ref[:, src:src + width])
    lane = lax.broadcasted_iota(jnp.int32, (1, SMALL_W), 1)
    blk = jnp.zeros((w_ref.shape[0], SMALL_W), F32)
    for lane_end, win, shift in reversed(small):
        piece = w_ref[:, win:win + SMALL_W]
        if shift:
            piece = pltpu.roll(piece, shift, axis=1)
        blk = jnp.where(lane < lane_end, piece, blk)
    out_ref[:, W_SMALL:W_SMALL + SMALL_W] = _bf(blk)


def _pack_w_in(w_in, layer):
    total, _, _ = _w_in_layout()
    assert w_in.shape[1:] == (D_MODEL, total)
    rows = D_MODEL // 4
    return pl.pallas_call(
        _pack_kernel,
        grid=(D_MODEL // rows,),
        in_specs=[pl.BlockSpec((None, rows, total), lambda i: (layer, i, 0))],
        out_specs=pl.BlockSpec((rows, W_TOTAL), lambda i: (i, 0)),
        out_shape=jax.ShapeDtypeStruct((D_MODEL, W_TOTAL), BF16),
        compiler_params=pltpu.CompilerParams(dimension_semantics=("arbitrary",)),
        name="pack_w_in",
    )(w_in)


def _layer(xf, pos, bsz, seq, p, tm, tb):
    invf, sgn = _rope_rows()
    ones_bd = jnp.asarray(_NPC["bd"], BF16)
    w_all = _pack_w_in(p["w_in_stacked"], p["layer"])
    ng = p["norm_g"][None, :]
    qn = jnp.tile(p["a_q_norm"], A_QKV // HEAD_DIM)[None, :]
    kn = jnp.tile(p["a_k_norm"], A_QKV // HEAD_DIM)[None, :]
    outs = _inproj(xf, pos, ng, w_all, qn, kn, invf, sgn, ones_bd, bsz, seq, tm)
    qs, ks, vs = outs[0:3], outs[3:6], outs[6:9]
    z_all, bqkv, cxbc, dqkv, small = outs[9:]

    os_, lses = [], []
    for gi in range(A_N_GROUPS):
        o, lse = _attn_group(qs[gi], ks[gi], vs[gi], gi)
        os_.append(o)
        lses.append(lse)
    br_a = _attn_combine(os_, lses, z_all, bsz, seq, tm)

    br_b = _gdn(bqkv, z_all, small, p["b_conv_w"],
                _lane_row(p["b_a_log"], SM_BDEC, SMALL_W), _lane_row(p["b_dt_bias"], SM_BDEC, SMALL_W),
                jnp.tile(p["b_out_norm"], N_HEADS)[None, :], bsz, seq, tb).reshape(bsz * seq, BRANCH_WIDTH)

    br_c = _ssd(cxbc, z_all, small, p["c_conv_w"], p["c_conv_b"][None, :],
                _lane_row(p["c_a_log"], SM_CDT, SMALL_W), _lane_row(p["c_dt_bias"], SM_CDT, SMALL_W),
                jnp.repeat(p["c_d_skip"], HEAD_DIM)[None, :], p["c_out_norm"][None, :],
                bsz, seq, tb).reshape(bsz * seq, BRANCH_WIDTH)

    w2pad = jnp.zeros((SMALL_W, D_KEY), F32).at[SM_DG:SM_DG + D_GATE_RANK, :].set(p["d_gate_w2"])
    br_d = _gla(dqkv, z_all, small, w2pad, p["d_gate_b"][None, :],
                jnp.tile(p["d_out_norm"], D_HEADS)[None, :], bsz, seq, tb).reshape(bsz * seq, BRANCH_WIDTH)

    return _merge(xf, ng, (br_a, br_b, br_c, br_d), _bf(p["w_gate"]), p["b_gate"][:, None, :],
                  _bf(p["w_branch"]), _bf(p["w_out"]), tm)


def _pick_tile(n, target):
    t = min(n, target)
    while n % t:
        t //= 2
    return t


def kernel(x, positions, norm_g, w_in, a_q_norm, a_k_norm, b_conv_w, b_a_log, b_dt_bias, b_out_norm,
           c_conv_w, c_conv_b, c_a_log, c_dt_bias, c_d_skip, c_out_norm, d_gate_w2, d_gate_b, d_out_norm,
           w_gate, b_gate, w_branch, w_out):
    bsz, seq, d_model = x.shape
    assert d_model == D_MODEL and seq % (2 * A_BLOCK * A_GROUPS[-1][1]) == 0
    xf = x.reshape(bsz * seq, d_model)
    tm = _pick_tile(seq, 512)
    assert tm % LANES == 0
    pos = positions.reshape(bsz * seq // tm, tm // LANES, LANES).astype(jnp.int32)
    tb = _pick_tile(seq, 512)
    names = ("norm_g", "w_in", "a_q_norm", "a_k_norm", "b_conv_w", "b_a_log", "b_dt_bias", "b_out_norm",
             "c_conv_w", "c_conv_b", "c_a_log", "c_dt_bias", "c_d_skip", "c_out_norm", "d_gate_w2",
             "d_gate_b", "d_out_norm", "w_gate", "b_gate", "w_branch", "w_out")
    stacked = (norm_g, w_in, a_q_norm, a_k_norm, b_conv_w, b_a_log, b_dt_bias, b_out_norm,
               c_conv_w, c_conv_b, c_a_log, c_dt_bias, c_d_skip, c_out_norm, d_gate_w2, d_gate_b,
               d_out_norm, w_gate, b_gate, w_branch, w_out)
    for layer in range(norm_g.shape[0]):
        p = {n: a[layer] for n, a in zip(names, stacked) if n != "w_in"}
        p["w_in_stacked"], p["layer"] = w_in, layer
        xf = _layer(xf, pos, bsz, seq, p, tm, tb)
    return xf.reshape(bsz, seq, d_model)
```

```python
import functools
import math

import numpy as np
import jax
import jax.numpy as jnp
from jax import lax
from jax.experimental import pallas as pl
from jax.experimental.pallas import tpu as pltpu

F32 = jnp.float32
BF16 = jnp.bfloat16

D_MODEL = 1024
N_BRANCH = 4
BRANCH_WIDTH = D_MODEL // N_BRANCH
HEAD_DIM = 64
N_HEADS = BRANCH_WIDTH // HEAD_DIM
EPS = 1e-6
ROPE_DIM = HEAD_DIM // 4
ROPE_HALF = ROPE_DIM // 2
ROPE_THETA = 500000.0
CONV_K = 4
PRO_ROWS = 128

A_GROUPS = ((128, 1), (512, 4), (2048, 16))
A_N_GROUPS = len(A_GROUPS)
A_QKV = A_N_GROUPS * BRANCH_WIDTH
A_BLOCK = 128
A_STEP_ROWS = 1024
A_UNROLL = 4

B_CHUNK = 64
B_UNROLL = 8
C_GROUPS = 2
C_STATE = 128
C_CHUNK = 64
C_UNROLL = 4
D_HEADS = 4
D_KEY = BRANCH_WIDTH // 2
D_DK = D_KEY // D_HEADS
D_DV = BRANCH_WIDTH // D_HEADS
D_GATE_RANK = 16
D_GATE_NORM = 16.0
D_CHUNK = 32
D_GROUP = 128
D_UNROLL = 2

LANES = 128
SUBLANES = 8
SLABS = BRANCH_WIDTH // LANES
SMALL_W = LANES
SM_BETA, SM_BDEC, SM_CDT, SM_DG = 0, 4, 8, 12

W_AQ, W_AK, W_AV = 0, A_QKV, 2 * A_QKV
W_Z = 3 * A_QKV
W_BQKV = W_Z + D_MODEL
W_CXBC = W_BQKV + 3 * BRANCH_WIDTH
W_DQKV = W_CXBC + 3 * BRANCH_WIDTH
W_SMALL = W_DQKV + 2 * BRANCH_WIDTH
W_TOTAL = W_SMALL + SMALL_W

VMEM_LIMIT = 56 * 1024 * 1024


def _bf(x):
    return x.astype(BF16)


def _dot(a, b):
    return jnp.dot(a, b, preferred_element_type=F32)


def _dot_nt(a, b):
    return lax.dot_general(a, b, (((1,), (1,)), ((), ())), preferred_element_type=F32)


def _dot_tn(a, b):
    return lax.dot_general(a, b, (((0,), (0,)), ((), ())), preferred_element_type=F32)


def _expand_heads(x, offset):
    rows = x.shape[0]
    head_lane = lax.broadcasted_iota(jnp.int32, (1, BRANCH_WIDTH), 1) // HEAD_DIM
    out = jnp.broadcast_to(x[:, offset:offset + 1], (rows, BRANCH_WIDTH))
    for h in range(1, N_HEADS):
        out = jnp.where(head_lane == h, jnp.broadcast_to(x[:, offset + h:offset + h + 1], (rows, BRANCH_WIDTH)), out)
    return out


def _seg_sum(x, ones_bd):
    hi = _bf(x)
    lo = _bf(x - hi.astype(F32))
    return _dot(hi, ones_bd) + _dot(lo, ones_bd)


def _seg_cumsum(x, seg):
    pos = lax.broadcasted_iota(jnp.int32, x.shape, 0) & (seg - 1)
    step = 1
    while step < seg:
        x = x + jnp.where(pos >= step, pltpu.roll(x, step, axis=0), 0.0)
        step *= 2
    return x


def _silu(x):
    h = 0.5 * x
    return h + h * jnp.tanh(h)


def _softplus(x):
    return jnp.maximum(x, 0.0) + jnp.log1p(jnp.exp(-jnp.abs(x)))


def _shift_rows(x, tail, s):
    r = pltpu.roll(x, s, axis=0)
    t = pltpu.roll(tail, s, axis=0)
    row = lax.broadcasted_iota(jnp.int32, tail.shape, 0)
    first = jnp.where(row < s, t, r[:SUBLANES])
    return jnp.concatenate([first, r[SUBLANES:]], axis=0)


def _causal_conv(x, tail, w):
    y = x * w[CONV_K - 1:CONV_K, :]
    for s in range(1, CONV_K):
        y = y + _shift_rows(x, tail, s) * w[CONV_K - 1 - s:CONV_K - s, :]
    return y


def _stack(y, bdmask16):
    return jnp.concatenate([_bf(y)] * N_HEADS, axis=0) * bdmask16


def _np_consts():
    r = np.arange(BRANCH_WIDTH)
    bd = (r[:, None] // HEAD_DIM == r[None, :] // HEAD_DIM).astype(np.float32)
    i = np.arange(B_CHUNK)[:, None]
    j = (r % HEAD_DIM)[None, :]
    xor = np.maximum(i ^ j, 1)
    lv = np.where(i > j, np.floor(np.log2(xor)).astype(np.int32), -1).astype(np.int32)
    eye_cat = (i == j).astype(np.float32)
    gmask = ((r[None, :] // C_STATE) == (r[:, None] // HEAD_DIM) // (N_HEADS // C_GROUPS)).astype(np.float32)
    dmask = ((np.arange(D_KEY)[:, None] // D_DK) == (r[None, :] // D_DV)).astype(np.float32)
    t = np.arange(D_GROUP)
    cmask = ((t[:, None] // D_CHUNK == t[None, :] // D_CHUNK) & (t[:, None] >= t[None, :])).astype(np.float32)

    return dict(bd=bd, lv=lv, eye_cat=eye_cat, gmask=gmask, dmask=dmask, cmask=cmask)


_NPC = _np_consts()


def _full_spec(shape):
    nd = len(shape)
    return pl.BlockSpec(shape, lambda *_: (0,) * nd)


def _const_spec(shape):
    nd = len(shape)
    return pl.BlockSpec(shape, lambda *_: (0,) * nd, pipeline_mode=pl.Buffered(1))


def _inproj_kernel(x_ref, pos_ref, ng_ref, w_ref, qn_ref, kn_ref, invf_ref, sgn_ref, ones_ref,
                   q0_ref, q1_ref, q2_ref, k0_ref, k1_ref, k2_ref, v0_ref, v1_ref, v2_ref,
                   z_ref, bqkv_ref, cxbc_ref, dqkv_ref, small_ref, perm_s, *, tm):
    x = x_ref[...]
    ms = jnp.mean(x * x, axis=-1, keepdims=True)
    hb = _bf(x * lax.rsqrt(ms + EPS) * ng_ref[...])

    def proj(lo, width):
        return _dot(hb, w_ref[:, lo:lo + width])

    pos_f = pos_ref[0].astype(F32)
    pos_col = jnp.concatenate([jnp.broadcast_to(pos_f[k:k + 1, :], (LANES, LANES)).T
                               for k in range(tm // LANES)], axis=0)
    ang = pos_col * invf_ref[...]
    cos = jnp.cos(ang)
    sin = jnp.sin(ang) * sgn_ref[...]
    lane = lax.broadcasted_iota(jnp.int32, (1, LANES), 1) & (HEAD_DIM - 1)
    first_half = lane < ROPE_HALF
    ones_bd = ones_ref[...]

    def emit(slabs, out_ref, which, g):
        dil = A_GROUPS[g][1]
        for s, v in enumerate(slabs):
            if dil == 1:
                out_ref[0, 0, :, s * LANES:(s + 1) * LANES] = _bf(v)
            else:
                perm_s[which, g - 1, s] = v
        if dil > 1:
            rows = tm // dil
            for s in range(len(slabs)):
                for r in range(dil):
                    out_ref[0, r, :, s * LANES:(s + 1) * LANES] = _bf(
                        perm_s[which, g - 1, s, pl.ds(r, rows, stride=dil), :])

    def qk_piece(u, ss, which, g):
        gain_ref, out_ref, scale = ((qn_ref, (q0_ref, q1_ref, q2_ref)[g], HEAD_DIM ** -0.5) if which == 0
                                    else (kn_ref, (k0_ref, k1_ref, k2_ref)[g], 1.0))
        gain = gain_ref[:, g * BRANCH_WIDTH:(g + 1) * BRANCH_WIDTH]
        un = u * lax.rsqrt(ss * (1.0 / HEAD_DIM) + EPS) * (gain * scale)
        slabs = []
        for s in range(SLABS):
            v = un[:, s * LANES:(s + 1) * LANES]
            partner = jnp.where(first_half, pltpu.roll(v, LANES - ROPE_HALF, axis=1),
                                pltpu.roll(v, ROPE_HALF, axis=1))
            slabs.append(v * cos + partner * sin)
        emit(slabs, out_ref, which, g)

    def plain_piece(dst_ref, col):
        def store(u):
            dst_ref[:, col:col + u.shape[1]] = u
        return store

    pieces = [("qk", which, g) for which in range(2) for g in range(A_N_GROUPS)]
    pieces += [("v", g) for g in range(A_N_GROUPS)]
    pieces += [("plain", plain_piece(z_ref, i * BRANCH_WIDTH)) for i in range(N_BRANCH)]
    pieces += [("plain", plain_piece(bqkv_ref, i * BRANCH_WIDTH)) for i in range(3)]
    pieces += [("plain", plain_piece(cxbc_ref, i * BRANCH_WIDTH)) for i in range(3)]
    pieces += [("plain", plain_piece(dqkv_ref, i * BRANCH_WIDTH)) for i in range(2)]
    assert len(pieces) * BRANCH_WIDTH == W_SMALL

    def finish(chunk):
        u, kinds = chunk
        sq = {i: _dot(_bf(u[:, i * BRANCH_WIDTH:(i + 1) * BRANCH_WIDTH] * u[:, i * BRANCH_WIDTH:(i + 1) * BRANCH_WIDTH]),
                      ones_bd) for i, kind in enumerate(kinds) if kind[0] == "qk"}
        for i, kind in enumerate(kinds):
            ui = u[:, i * BRANCH_WIDTH:(i + 1) * BRANCH_WIDTH]
            if kind[0] == "qk":
                qk_piece(ui, sq[i], kind[1], kind[2])
            elif kind[0] == "v":
                emit([ui[:, s * LANES:(s + 1) * LANES] for s in range(SLABS)],
                     (v0_ref, v1_ref, v2_ref)[kind[1]], 2, kind[1])
            else:
                kind[1](ui)

    pending = None
    for c0 in range(0, len(pieces), 2):
        kinds = pieces[c0:c0 + 2]
        width = len(kinds) * BRANCH_WIDTH + (SMALL_W if c0 + 2 >= len(pieces) else 0)
        u = proj(c0 * BRANCH_WIDTH, width)
        if c0 + 2 >= len(pieces):
            kinds = kinds + [("plain", plain_piece(small_ref, 0))]
        if pending is not None:
            finish(pending)
        pending = (u, kinds)
    finish(pending)


def _inproj(xf, pos, norm_g, w_all, qn, kn, invf, sgn, ones_bd, bsz, seq, tm):
    t = xf.shape[0]
    tps = seq // tm
    row = lambda w: pl.BlockSpec((tm, w), lambda i: (i, 0))
    dils = [d for _, d in A_GROUPS]
    res_spec = lambda d: pl.BlockSpec((1, d, tm // d, BRANCH_WIDTH), lambda i: (i // tps, 0, i % tps, 0))
    res_shape = lambda d: jax.ShapeDtypeStruct((bsz, d, seq // d, BRANCH_WIDTH), BF16)
    flat = [(D_MODEL, F32), (3 * BRANCH_WIDTH, F32), (3 * BRANCH_WIDTH, F32), (2 * BRANCH_WIDTH, F32),
            (SMALL_W, F32)]
    return pl.pallas_call(
        functools.partial(_inproj_kernel, tm=tm),
        grid=(t // tm,),
        in_specs=[row(D_MODEL), pl.BlockSpec((1, tm // LANES, LANES), lambda i: (i, 0, 0)),
                  _full_spec((1, D_MODEL)), _const_spec((D_MODEL, W_TOTAL)),
                  _full_spec((1, A_QKV)), _full_spec((1, A_QKV)), _full_spec((1, LANES)),
                  _full_spec((1, LANES)), _full_spec((BRANCH_WIDTH, BRANCH_WIDTH))],
        out_specs=[res_spec(d) for d in dils] * 3 + [row(w) for w, _ in flat],
        out_shape=[res_shape(d) for d in dils] * 3 + [jax.ShapeDtypeStruct((t, w), dt) for w, dt in flat],
        scratch_shapes=[pltpu.VMEM((3, A_N_GROUPS - 1, SLABS, tm, LANES), F32)],
        compiler_params=pltpu.CompilerParams(dimension_semantics=("arbitrary",),
                                             vmem_limit_bytes=VMEM_LIMIT),
        name="inproj",
    )(xf, pos, norm_g, w_all, qn, kn, invf, sgn, ones_bd)


def _attn_kernel(q_ref, k_ref, v_ref, bias_ref, o_ref, lse_ref, *, n, rb, dil):
    nb = n // A_BLOCK
    head_lane = lax.broadcasted_iota(jnp.int32, (1, BRANCH_WIDTH), 1) // HEAD_DIM
    shift = nb.bit_length() - 1
    assert nb == 1 << shift and (rb * nb) % A_UNROLL == 0

    def blocks(it, carry):
        units = [it * A_UNROLL + b for b in range(A_UNROLL)]
        rr = [u >> shift for u in units]
        js = [u & (nb - 1) for u in units]
        q0 = [pl.multiple_of(j * A_BLOCK, A_BLOCK) for j in js]
        k0 = [pl.multiple_of(jnp.maximum(j - 1, 0) * A_BLOCK, A_BLOCK) for j in js]
        s = []
        for b in range(A_UNROLL):
            q = q_ref[0, rr[b], pl.ds(q0[b], A_BLOCK), :]
            zero = jnp.zeros_like(q)
            qs = jnp.concatenate([jnp.where(head_lane == h, q, zero) for h in range(N_HEADS)], axis=0)
            s.append(_dot_nt(qs, k_ref[0, rr[b], pl.ds(k0[b], 2 * A_BLOCK), :])
                     + bias_ref[jnp.minimum(js[b], 1)])
        m = [jnp.max(x, axis=-1, keepdims=True) for x in s]
        p = [jnp.exp(x - mx) for x, mx in zip(s, m)]
        l = [jnp.sum(x, axis=-1, keepdims=True) for x in p]
        pv = [_dot(_bf(p[b]), v_ref[0, rr[b], pl.ds(k0[b], 2 * A_BLOCK), :]) * (1.0 / l[b])
              for b in range(A_UNROLL)]
        for b in range(A_UNROLL):
            lse = m[b] + jnp.log(l[b])
            o_acc = pv[b][:A_BLOCK]
            lse_acc = jnp.broadcast_to(lse[:A_BLOCK], (A_BLOCK, BRANCH_WIDTH))
            for h in range(1, N_HEADS):
                o_acc = jnp.where(head_lane == h, pv[b][h * A_BLOCK:(h + 1) * A_BLOCK], o_acc)
                lse_acc = jnp.where(head_lane == h, lse[h * A_BLOCK:(h + 1) * A_BLOCK], lse_acc)
            if dil == 1:
                rows = pl.ds(q0[b], A_BLOCK)
            else:
                rows = pl.ds(q0[b] * dil + pl.program_id(1) * rb + rr[b], A_BLOCK, stride=dil)
            for s in range(SLABS):
                o_ref[0, s, rows, :] = o_acc[:, s * LANES:(s + 1) * LANES]
                lse_ref[0, s, rows, :] = lse_acc[:, s * LANES:(s + 1) * LANES]
        return carry

    lax.fori_loop(0, rb * nb // A_UNROLL, blocks, 0)


def _attn_group(q, k, v, gi):
    window, dil = A_GROUPS[gi]
    bsz, _, n, _ = q.shape
    assert n % (2 * A_BLOCK) == 0
    rb = min(dil, max(1, A_STEP_ROWS // n))
    win = window // dil
    qi = (np.arange(N_HEADS * A_BLOCK) % A_BLOCK)[:, None]
    kj = np.arange(2 * A_BLOCK)[None, :]
    band = lambda off: np.where((qi - kj + off >= 0) & (qi - kj + off <= win), 0.0, -np.inf).astype(np.float32)
    bias = jnp.asarray(np.stack([band(0), band(A_BLOCK)]))
    blk = pl.BlockSpec((1, rb, n, BRANCH_WIDTH), lambda b, r: (b, r, 0, 0))
    seq = n * dil
    out_blk = pl.BlockSpec((1, SLABS, seq, LANES), lambda b, r: (b, 0, 0, 0))
    out_shape = jax.ShapeDtypeStruct((bsz, SLABS, seq, LANES), F32)
    return pl.pallas_call(
        functools.partial(_attn_kernel, n=n, rb=rb, dil=dil),
        grid=(bsz, dil // rb),
        in_specs=[blk] * 3 + [_full_spec(bias.shape)],
        out_specs=[out_blk, out_blk],
        out_shape=[out_shape, out_shape],
        compiler_params=pltpu.CompilerParams(dimension_semantics=("arbitrary", "arbitrary"),
                                             vmem_limit_bytes=VMEM_LIMIT),
        name=f"attn_d{dil}",
    )(q, k, v, bias)


def _attn_merge_groups(o_refs, lse_refs, z):
    slabs = []
    for s in range(SLABS):
        lses = [ref[0, s] for ref in lse_refs]
        m = jnp.maximum(jnp.maximum(lses[0], lses[1]), lses[2])
        es = [jnp.exp(l - m) for l in lses]
        acc = o_refs[0][0, s] * es[0] + o_refs[1][0, s] * es[1] + o_refs[2][0, s] * es[2]
        acc = acc / (es[0] + es[1] + es[2])
        slabs.append(_bf(acc * _silu(z[:, s * LANES:(s + 1) * LANES])))
    return jnp.concatenate(slabs, axis=1)


def _gdn_kernel(qkv_ref, z_ref, small_ref, cw_ref, alog_ref, dtb_ref, og_ref,
                ones_ref, bd_ref, lv_ref, eye_ref,
                out_ref,
                tail_ref, st_ref, q_s, k_s, vb_s, kb_s, ce_s, qe_s, oi_s, corr_s, add_s, *, tb):
    nchunk = tb // B_CHUNK

    @pl.when(pl.program_id(1) == 0)
    def _():
        tail_ref[...] = jnp.zeros_like(tail_ref)
        st_ref[...] = jnp.zeros_like(st_ref)

    ones_bd = ones_ref[...]
    bd16 = ones_bd
    bdmask = bd_ref[...]
    lv = lv_ref[...]
    eye_cat = eye_ref[...]

    def prologue(r, carry):
        rows = pl.ds(pl.multiple_of(r * PRO_ROWS, PRO_ROWS), PRO_ROWS)
        halo = pl.ds(pl.multiple_of(jnp.maximum(r * PRO_ROWS - SUBLANES, 0), SUBLANES), SUBLANES)
        tail = jnp.where(r == 0, tail_ref[...], qkv_ref[0, halo, :])
        sm = small_ref[0, rows, :]
        g = -jnp.exp(alog_ref[...]) * _softplus(sm + dtb_ref[...])
        ce_s[rows, :] = _expand_heads(_seg_cumsum(g, B_CHUNK), SM_BDEC)
        beta = _expand_heads(jax.nn.sigmoid(sm), SM_BETA)
        for part in range(3):
            cols = slice(part * BRANCH_WIDTH, (part + 1) * BRANCH_WIDTH)
            y = _silu(_causal_conv(qkv_ref[0, rows, cols], tail[:, cols], cw_ref[:, cols]))
            if part == 0:
                q_s[rows, :] = y * lax.rsqrt(_seg_sum(y * y, ones_bd) + EPS) * (HEAD_DIM ** -0.5)
            elif part == 1:
                kn = y * lax.rsqrt(_seg_sum(y * y, ones_bd) + EPS)
                k_s[rows, :] = kn
                kb_s[rows, :] = kn * beta
            else:
                vb_s[rows, :] = y * beta
        return carry

    lax.fori_loop(0, tb // PRO_ROWS, prologue, 0)
    tail_ref[...] = qkv_ref[0, tb - SUBLANES:, :]

    il = lax.broadcasted_iota(jnp.int32, (B_CHUNK, BRANCH_WIDTH), 0)
    jl = lax.broadcasted_iota(jnp.int32, (B_CHUNK, BRANCH_WIDTH), 1) & (HEAD_DIM - 1)
    causal = il >= jl

    def intra_group(i, carry):
        ks = range(B_UNROLL)
        rows = [pl.ds(pl.multiple_of((i * B_UNROLL + k) * B_CHUNK, B_CHUNK), B_CHUNK) for k in ks]
        cc = [ce_s[r, :] for r in rows]
        kc = [k_s[r, :] for r in rows]
        qc = [q_s[r, :] for r in rows]
        kb = [kb_s[r, :] for r in rows]
        gamma = [jnp.exp(jnp.where(causal, c_ - jnp.sum(c_ * eye_cat, axis=0, keepdims=True), -jnp.inf))
                 for c_ in cc]
        prod = [_dot_nt(_bf(jnp.concatenate([kb[k], qc[k]], axis=0)), _stack(kc[k], bd16)) for k in ks]
        a_kk = [jnp.where(il > jl, prod[k][:B_CHUNK] * gamma[k], 0.0) for k in ks]
        a_qk = [_bf(prod[k][B_CHUNK:] * gamma[k]) for k in ks]
        t = [eye_cat - jnp.where(lv == 0, a, 0.0) for a in a_kk]
        for lb in range(1, int(math.log2(B_CHUNK))):
            x = [_dot(_bf(t[k]), _stack(jnp.where(lv == lb, a_kk[k], 0.0), bd16)) for k in ks]
            t = [t[k] - _dot(_bf(x[k]), _stack(t[k], bd16)) for k in ks]
        ecum = [jnp.exp(c_) for c_ in cc]
        uw = [_dot(_bf(t[k]), jnp.concatenate([_stack(vb_s[rows[k], :], bd16),
                                               _stack(kb[k] * ecum[k], bd16)], axis=1)) for k in ks]
        u = [a[:, :BRANCH_WIDTH] for a in uw]
        w = [a[:, BRANCH_WIDTH:] for a in uw]
        both = [_dot(a_qk[k], jnp.concatenate([_stack(u[k], bd16), _stack(w[k], bd16)], axis=1)) for k in ks]
        k_dec = [_bf(kc[k] * jnp.exp(cc[k][B_CHUNK - 1:B_CHUNK, :] - cc[k])) for k in ks]
        wu = [_dot_tn(_bf(jnp.concatenate([w[k], u[k]], axis=1)), k_dec[k]) for k in ks]
        for k in ks:
            oi_s[rows[k], :] = both[k][:, :BRANCH_WIDTH]
            qe_s[rows[k], :] = qc[k] * ecum[k] - both[k][:, BRANCH_WIDTH:]
            corr_s[i * B_UNROLL + k] = _bf(-(wu[k][:BRANCH_WIDTH] * bdmask))
            add_s[i * B_UNROLL + k] = wu[k][BRANCH_WIDTH:] * bdmask
        return carry

    lax.fori_loop(0, nchunk // B_UNROLL, intra_group, 0)

    og = og_ref[...]
    st = st_ref[...]
    for c in range(nchunk):
        rows = slice(c * B_CHUNK, (c + 1) * B_CHUNK)
        st16 = _bf(st)
        clast = ce_s[(c + 1) * B_CHUNK - 1:(c + 1) * B_CHUNK, :]
        st = jnp.exp(clast) * st + _dot(st16, corr_s[c]) + add_s[c]
        o = _dot_nt(_bf(qe_s[rows, :]), st16) + oi_s[rows, :]
        ms = _seg_sum(o * o, ones_bd) * (1.0 / HEAD_DIM)
        on = o * lax.rsqrt(ms + EPS) * og
        out_ref[0, rows, :] = _bf(on * _silu(z_ref[0, rows, :]))
    st_ref[...] = st


def _gdn(bqkv, z_all, small, conv_w, alog_row, dtb_row, og_row, bsz, seq, tb):
    c = _NPC
    nchunk = tb // B_CHUNK
    assert nchunk % B_UNROLL == 0
    seq_blk = lambda w, col: pl.BlockSpec((1, tb, w), lambda b, j: (b, j, col))
    consts = [jnp.asarray(c["bd"], BF16), jnp.asarray(c["bd"]), jnp.asarray(c["lv"]), jnp.asarray(c["eye_cat"])]
    act = pltpu.VMEM((tb, BRANCH_WIDTH), F32)
    sq = (nchunk, BRANCH_WIDTH, BRANCH_WIDTH)
    return pl.pallas_call(
        functools.partial(_gdn_kernel, tb=tb),
        grid=(bsz, seq // tb),
        in_specs=[seq_blk(3 * BRANCH_WIDTH, 0), seq_blk(BRANCH_WIDTH, 1), seq_blk(SMALL_W, 0),
                  _full_spec(conv_w.shape), _full_spec((1, SMALL_W)), _full_spec((1, SMALL_W)),
                  _full_spec((1, BRANCH_WIDTH))] + [_full_spec(a.shape) for a in consts],
        out_specs=seq_blk(BRANCH_WIDTH, 0),
        out_shape=jax.ShapeDtypeStruct((bsz, seq, BRANCH_WIDTH), BF16),
        scratch_shapes=[pltpu.VMEM((SUBLANES, 3 * BRANCH_WIDTH), F32),
                        pltpu.VMEM((BRANCH_WIDTH, BRANCH_WIDTH), F32)] + [act] * 7
                       + [pltpu.VMEM(sq, BF16), pltpu.VMEM(sq, F32)],
        compiler_params=pltpu.CompilerParams(dimension_semantics=("arbitrary", "arbitrary"),
                                             vmem_limit_bytes=VMEM_LIMIT),
        name="gdn",
    )(bqkv.reshape(bsz, seq, -1), z_all.reshape(bsz, seq, -1), small.reshape(bsz, seq, -1),
      conv_w, alog_row, dtb_row, og_row, *consts)


def _ssd_kernel(xbc_ref, z_ref, small_ref, cw_ref, cb_ref, alog_ref, dtb_ref, dskip_ref, og_ref,
                bd_ref, eye_ref, gm_ref, gmt_ref,
                out_ref,
                tail_ref, st_ref, xs_s, bm_s, cm_s, dt_s, ac_s, *, tb):
    nchunk = tb // C_CHUNK

    @pl.when(pl.program_id(1) == 0)
    def _():
        tail_ref[...] = jnp.zeros_like(tail_ref)
        st_ref[...] = jnp.zeros_like(st_ref)

    bd16 = bd_ref[...]
    eye_cat = eye_ref[...]
    gmask16 = gm_ref[...]
    gmask_t = gmt_ref[...]

    def prologue(r, carry):
        rows = pl.ds(pl.multiple_of(r * PRO_ROWS, PRO_ROWS), PRO_ROWS)
        halo = pl.ds(pl.multiple_of(jnp.maximum(r * PRO_ROWS - SUBLANES, 0), SUBLANES), SUBLANES)
        tail = jnp.where(r == 0, tail_ref[...], xbc_ref[0, halo, :])
        for part, dst in enumerate((xs_s, bm_s, cm_s)):
            cols = slice(part * BRANCH_WIDTH, (part + 1) * BRANCH_WIDTH)
            dst[rows, :] = _silu(_causal_conv(xbc_ref[0, rows, cols], tail[:, cols], cw_ref[:, cols])
                                 + cb_ref[:, cols])
        return carry

    lax.fori_loop(0, tb // PRO_ROWS, prologue, 0)
    tail_ref[...] = xbc_ref[0, tb - SUBLANES:, :]
    dt = _softplus(small_ref[0] + dtb_ref[...])
    a = -jnp.exp(alog_ref[...]) * dt
    dt_s[...] = _expand_heads(dt, SM_CDT)
    ac_s[...] = _expand_heads(_seg_cumsum(a, C_CHUNK), SM_CDT)

    il = lax.broadcasted_iota(jnp.int32, (C_CHUNK, BRANCH_WIDTH), 0)
    jl = lax.broadcasted_iota(jnp.int32, (C_CHUNK, BRANCH_WIDTH), 1) & (HEAD_DIM - 1)
    causal = il >= jl
    og = og_ref[...]
    dskip = dskip_ref[...]
    half = BRANCH_WIDTH // C_GROUPS

    def chunk_group(i, carry):
        ks = range(C_UNROLL)
        rows = [pl.ds(pl.multiple_of((i * C_UNROLL + k) * C_CHUNK, C_CHUNK), C_CHUNK) for k in ks]
        xc = [xs_s[r, :] for r in rows]
        ac = [ac_s[r, :] for r in rows]
        cc16 = [_bf(cm_s[r, :]) for r in rows]
        bc16 = [_bf(bm_s[r, :]) for r in rows]
        xdt = [xc[k] * dt_s[rows[k], :] for k in ks]
        seg = [jnp.exp(jnp.where(causal, a - jnp.sum(a * eye_cat, axis=0, keepdims=True), -jnp.inf)) for a in ac]
        scores = [_dot_nt(cc16[k], jnp.concatenate([bc16[k]] * N_HEADS, axis=0) * gmask16) * seg[k] for k in ks]
        y_diag = [_dot(_bf(scores[k]), _stack(xdt[k], bd16)) for k in ks]
        alast = [a[C_CHUNK - 1:C_CHUNK, :] for a in ac]
        delta = [gmask_t * _dot_tn(bc16[k], _bf(xdt[k] * jnp.exp(alast[k] - ac[k]))) for k in ks]
        st = st_ref[...]
        sts = []
        for k in ks:
            sts.append(_bf(st))
            st = jnp.exp(alast[k]) * st + delta[k]
        st_ref[...] = st
        y_off = [_dot(cc16[k], sts[k]) * jnp.exp(ac[k]) for k in ks]
        for k in ks:
            yv = (y_diag[k] + y_off[k] + dskip * xc[k]) * _silu(z_ref[0, rows[k], :])
            for gi in range(C_GROUPS):
                yg = yv[:, gi * half:(gi + 1) * half]
                ms = jnp.mean(yg * yg, axis=-1, keepdims=True)
                out_ref[0, rows[k], gi * half:(gi + 1) * half] = _bf(
                    yg * lax.rsqrt(ms + EPS) * og[:, gi * half:(gi + 1) * half])
        return carry

    lax.fori_loop(0, nchunk // C_UNROLL, chunk_group, 0)


def _ssd(cxbc, z_all, small, conv_w, conv_b, alog_row, dtb_row, dskip_row, og_row, bsz, seq, tb):
    c = _NPC
    assert (tb // C_CHUNK) % C_UNROLL == 0
    seq_blk = lambda w, col: pl.BlockSpec((1, tb, w), lambda b, j: (b, j, col))
    consts = [jnp.asarray(c["bd"], BF16), jnp.asarray(c["eye_cat"]), jnp.asarray(c["gmask"], BF16),
              jnp.asarray(c["gmask"].T.copy())]
    act = pltpu.VMEM((tb, BRANCH_WIDTH), F32)
    return pl.pallas_call(
        functools.partial(_ssd_kernel, tb=tb),
        grid=(bsz, seq // tb),
        in_specs=[seq_blk(3 * BRANCH_WIDTH, 0), seq_blk(BRANCH_WIDTH, 2), seq_blk(SMALL_W, 0),
                  _full_spec(conv_w.shape), _full_spec((1, 3 * BRANCH_WIDTH)), _full_spec((1, SMALL_W)),
                  _full_spec((1, SMALL_W)), _full_spec((1, BRANCH_WIDTH)), _full_spec((1, BRANCH_WIDTH))]
                 + [_full_spec(a.shape) for a in consts],
        out_specs=seq_blk(BRANCH_WIDTH, 0),
        out_shape=jax.ShapeDtypeStruct((bsz, seq, BRANCH_WIDTH), BF16),
        scratch_shapes=[pltpu.VMEM((SUBLANES, 3 * BRANCH_WIDTH), F32),
                        pltpu.VMEM((BRANCH_WIDTH, BRANCH_WIDTH), F32)] + [act] * 5,
        compiler_params=pltpu.CompilerParams(dimension_semantics=("arbitrary", "arbitrary"),
                                             vmem_limit_bytes=VMEM_LIMIT),
        name="ssd",
    )(cxbc.reshape(bsz, seq, -1), z_all.reshape(bsz, seq, -1), small.reshape(bsz, seq, -1),
      conv_w, conv_b, alog_row, dtb_row, dskip_row, og_row, *consts)


def _gla_kernel(qkv_ref, z_ref, small_ref, w2_ref, bg_ref, og_ref, ones_ref, dm_ref, cm_ref,
                out_ref,
                st_ref, qg_s, kg_s, k_s, cum_s, *, tb):
    @pl.when(pl.program_id(1) == 0)
    def _():
        st_ref[...] = jnp.zeros_like(st_ref)

    qkv = qkv_ref[0]
    q = qkv[:, :D_KEY] * (D_DK ** -0.5)
    k = qkv[:, D_KEY:2 * D_KEY]
    sm = small_ref[0]
    s_hi = _bf(sm)
    s_lo = _bf(sm - s_hi.astype(F32))
    w2 = w2_ref[...]
    w_hi = _bf(w2)
    w_lo = _bf(w2 - w_hi.astype(F32))
    gl = _dot(s_hi, w_hi) + _dot(s_lo, w_hi) + _dot(s_hi, w_lo) + bg_ref[...]
    glog = jax.nn.log_sigmoid(gl) * (1.0 / D_GATE_NORM)
    cum = _seg_cumsum(glog, D_CHUNK)
    qg_s[...] = q * jnp.exp(cum)
    kg_s[...] = k * jnp.exp(-cum)
    k_s[...] = k
    cum_s[...] = cum

    cmask = cm_ref[...] > 0.0
    klane = lax.broadcasted_iota(jnp.int32, (1, D_KEY), 1) // D_DK
    vlane = lax.broadcasted_iota(jnp.int32, (1, BRANCH_WIDTH), 1) // D_DV
    dmask = dm_ref[...]
    ones_bd = ones_ref[...]
    og = og_ref[...]
    per_group = D_GROUP // D_CHUNK

    cmask4 = jnp.concatenate([cmask] * D_HEADS, axis=0)

    def groups(it, carry):
        gs = range(D_UNROLL)
        rows = [pl.ds(pl.multiple_of((it * D_UNROLL + g) * D_GROUP, D_GROUP), D_GROUP) for g in gs]
        qgg = [qg_s[r, :] for r in rows]
        cc = [cum_s[r, :] for r in rows]
        vg = [_bf(qkv_ref[0, r, 2 * D_KEY:]) for r in rows]
        a = []
        for g in gs:
            zero = jnp.zeros_like(qgg[g])
            qs = _bf(jnp.concatenate([jnp.where(klane == h, qgg[g], zero) for h in range(D_HEADS)], axis=0))
            a.append(_dot_nt(qs, _bf(kg_s[rows[g], :])))
        av = [_dot(_bf(jnp.where(cmask4, a[g], 0.0)), vg[g]) for g in gs]
        chunks = [(g, c) for g in gs for c in range(per_group)]
        clast, delta = {}, {}
        for g, c in chunks:
            sl = slice(c * D_CHUNK, (c + 1) * D_CHUNK)
            ccc = cc[g][sl]
            clast[g, c] = ccc[D_CHUNK - 1:D_CHUNK, :]
            kd = k_s[rows[g], :][sl] * jnp.exp(clast[g, c] - ccc)
            delta[g, c] = dmask * _dot_tn(_bf(kd), vg[g][sl])
        dcol = {}
        for g in gs:
            dec_rows = jnp.concatenate([jnp.broadcast_to(jnp.exp(clast[g, c]), (D_CHUNK, D_KEY))
                                        for c in range(per_group)], axis=0)
            dec_t = dec_rows.T
            for c in range(per_group):
                dcol[g, c] = dec_t[:, c * D_CHUNK:c * D_CHUNK + 1]
        st = st_ref[...]
        sts = {}
        for g, c in chunks:
            sts[g, c] = _bf(st)
            st = dcol[g, c] * st + delta[g, c]
        st_ref[...] = st
        for g in gs:
            o_inter = jnp.concatenate([_dot(_bf(qgg[g][c * D_CHUNK:(c + 1) * D_CHUNK]), sts[g, c])
                                       for c in range(per_group)], axis=0)
            o_intra = av[g][:D_GROUP]
            for h in range(1, D_HEADS):
                o_intra = jnp.where(vlane == h, av[g][h * D_GROUP:(h + 1) * D_GROUP], o_intra)
            o = o_inter + o_intra
            ms = _seg_sum(o * o, ones_bd) * (1.0 / D_DV)
            on = o * lax.rsqrt(ms + EPS) * og
            out_ref[0, rows[g], :] = _bf(on * _silu(z_ref[0, rows[g], :]))
        return carry

    lax.fori_loop(0, tb // (D_GROUP * D_UNROLL), groups, 0)


def _gla(dqkv, z_all, small, w2pad, bg_row, og_row, bsz, seq, tb):
    c = _NPC
    seq_blk = lambda w, col: pl.BlockSpec((1, tb, w), lambda b, j: (b, j, col))
    consts = [jnp.asarray(c["bd"], BF16), jnp.asarray(c["dmask"]), jnp.asarray(c["cmask"])]
    return pl.pallas_call(
        functools.partial(_gla_kernel, tb=tb),
        grid=(bsz, seq // tb),
        in_specs=[seq_blk(2 * BRANCH_WIDTH, 0), seq_blk(BRANCH_WIDTH, 3), seq_blk(SMALL_W, 0),
                  _full_spec((SMALL_W, D_KEY)), _full_spec((1, D_KEY)), _full_spec((1, BRANCH_WIDTH))]
                 + [_full_spec(a.shape) for a in consts],
        out_specs=seq_blk(BRANCH_WIDTH, 0),
        out_shape=jax.ShapeDtypeStruct((bsz, seq, BRANCH_WIDTH), BF16),
        scratch_shapes=[pltpu.VMEM((D_KEY, BRANCH_WIDTH), F32)] + [pltpu.VMEM((tb, D_KEY), F32)] * 4,
        compiler_params=pltpu.CompilerParams(dimension_semantics=("arbitrary", "arbitrary"),
                                             vmem_limit_bytes=VMEM_LIMIT),
        name="gla",
    )(dqkv.reshape(bsz, seq, -1), z_all.reshape(bsz, seq, -1), small.reshape(bsz, seq, -1),
      w2pad, bg_row, og_row, *consts)


def _merge_kernel(x_ref, ng_ref, o0, o1, o2, l0, l1, l2, za_ref, br1, br2, br3,
                  wg_ref, bg_ref, wb_ref, wo_ref, out_ref):
    x = x_ref[...]
    ms = jnp.mean(x * x, axis=-1, keepdims=True)
    hb = _bf(x * lax.rsqrt(ms + EPS) * ng_ref[...])
    br_a = _attn_merge_groups((o0, o1, o2), (l0, l1, l2), za_ref[...])
    merged = jnp.zeros(x.shape, F32)
    for i, br in enumerate((br_a, br1[...], br2[...], br3[...])):
        gate = jax.nn.sigmoid(_dot(hb, wg_ref[i]) + bg_ref[i])
        merged = merged + gate * _dot(br, wb_ref[i])
    out_ref[...] = x + _dot(_bf(merged), wo_ref[...])


def _merge(xf, norm_g, attn_os, attn_lses, z_all, brs, w_gate, b_gate, w_branch, w_out, seq, tm):
    t = xf.shape[0]
    tps = seq // tm
    row = lambda w: pl.BlockSpec((tm, w), lambda i: (i, 0))
    slab = pl.BlockSpec((1, SLABS, tm, LANES), lambda i: (i // tps, 0, i % tps, 0))
    return pl.pallas_call(
        _merge_kernel,
        grid=(t // tm,),
        in_specs=[row(D_MODEL), _full_spec((1, D_MODEL))] + [slab] * (2 * A_N_GROUPS) + [row(BRANCH_WIDTH)]
                 + [row(BRANCH_WIDTH)] * (N_BRANCH - 1)
                 + [_const_spec(w_gate.shape), _full_spec(b_gate.shape), _const_spec(w_branch.shape),
                    _const_spec(w_out.shape)],
        out_specs=row(D_MODEL),
        out_shape=jax.ShapeDtypeStruct((t, D_MODEL), F32),
        compiler_params=pltpu.CompilerParams(dimension_semantics=("arbitrary",),
                                             vmem_limit_bytes=VMEM_LIMIT),
        name="merge",
    )(xf, norm_g, *attn_os, *attn_lses, z_all, *brs, w_gate, b_gate, w_branch, w_out)


def _lane_row(vals, offset, width):
    return jnp.zeros((1, width), F32).at[0, offset:offset + vals.shape[0]].set(vals.astype(F32))


def _rope_rows():
    lane = np.arange(LANES) % HEAD_DIM
    inv_freq = ROPE_THETA ** (-jnp.arange(ROPE_HALF, dtype=F32) / ROPE_HALF)
    invf = jnp.where(lane < ROPE_DIM, inv_freq[lane % ROPE_HALF], 0.0).astype(F32)[None, :]
    sgn = np.where(lane < ROPE_HALF, -1.0, np.where(lane < ROPE_DIM, 1.0, 0.0)).astype(np.float32)[None, :]
    return invf, jnp.asarray(sgn)


def _w_in_layout():
    sizes = (A_QKV, A_QKV, A_QKV, BRANCH_WIDTH, 3 * BRANCH_WIDTH, BRANCH_WIDTH, N_HEADS, N_HEADS,
             3 * BRANCH_WIDTH, BRANCH_WIDTH, N_HEADS, D_KEY, D_KEY, BRANCH_WIDTH, BRANCH_WIDTH, D_GATE_RANK)
    names = ("a_q", "a_k", "a_v", "a_z", "b_qkv", "b_z", "b_beta", "b_a", "c_xbc", "c_z", "c_dt",
             "d_q", "d_k", "d_v", "d_z", "d_g")
    src = dict(zip(names, np.concatenate([[0], np.cumsum(sizes)[:-1]]).tolist()))
    total = int(sum(sizes))
    moves = [(W_AQ, src["a_q"], 3 * A_QKV + BRANCH_WIDTH),
             (W_Z + BRANCH_WIDTH, src["b_z"], BRANCH_WIDTH),
             (W_Z + 2 * BRANCH_WIDTH, src["c_z"], BRANCH_WIDTH),
             (W_Z + 3 * BRANCH_WIDTH, src["d_z"], BRANCH_WIDTH),
             (W_BQKV, src["b_qkv"], 3 * BRANCH_WIDTH),
             (W_CXBC, src["c_xbc"], 3 * BRANCH_WIDTH),
             (W_DQKV, src["d_q"], 2 * BRANCH_WIDTH)]
    small = []
    for lane0, name, width in ((SM_BETA, "b_beta", 2 * N_HEADS), (SM_CDT, "c_dt", N_HEADS),
                               (SM_DG, "d_g", D_GATE_RANK)):
        win = min(src[name] - lane0, total - SMALL_W)
        small.append((lane0 + width, win, (lane0 - (src[name] - win)) % SMALL_W))
    return total, moves, small


def _pack_kernel(w_ref, out_ref):
    _, moves, small = _w_in_layout()
    for dst, src, width in moves:
        out_ref[:, dst:dst + width] = _bf(w_ref[:, src:src + width])
    lane = lax.broadcasted_iota(jnp.int32, (1, SMALL_W), 1)
    blk = jnp.zeros((w_ref.shape[0], SMALL_W), F32)
    for lane_end, win, shift in reversed(small):
        piece = w_ref[:, win:win + SMALL_W]
        if shift:
            piece = pltpu.roll(piece, shift, axis=1)
        blk = jnp.where(lane < lane_end, piece, blk)
    out_ref[:, W_SMALL:W_SMALL + SMALL_W] = _bf(blk)


def _pack_w_in(w_in, layer):
    total, _, _ = _w_in_layout()
    assert w_in.shape[1:] == (D_MODEL, total)
    rows = D_MODEL // 4
    return pl.pallas_call(
        _pack_kernel,
        grid=(D_MODEL // rows,),
        in_specs=[pl.BlockSpec((None, rows, total), lambda i: (layer, i, 0))],
        out_specs=pl.BlockSpec((rows, W_TOTAL), lambda i: (i, 0)),
        out_shape=jax.ShapeDtypeStruct((D_MODEL, W_TOTAL), BF16),
        compiler_params=pltpu.CompilerParams(dimension_semantics=("arbitrary",)),
        name="pack_w_in",
    )(w_in)


def _layer(xf, pos, bsz, seq, p, tm, tb):
    invf, sgn = _rope_rows()
    ones_bd = jnp.asarray(_NPC["bd"], BF16)
    w_all = _pack_w_in(p["w_in_stacked"], p["layer"])
    ng = p["norm_g"][None, :]
    qn = jnp.tile(p["a_q_norm"], A_QKV // HEAD_DIM)[None, :]
    kn = jnp.tile(p["a_k_norm"], A_QKV // HEAD_DIM)[None, :]
    outs = _inproj(xf, pos, ng, w_all, qn, kn, invf, sgn, ones_bd, bsz, seq, tm)
    qs, ks, vs = outs[0:3], outs[3:6], outs[6:9]
    z_all, bqkv, cxbc, dqkv, small = outs[9:]

    os_, lses = [], []
    for gi in range(A_N_GROUPS):
        o, lse = _attn_group(qs[gi], ks[gi], vs[gi], gi)
        os_.append(o)
        lses.append(lse)

    br_b = _gdn(bqkv, z_all, small, p["b_conv_w"],
                _lane_row(p["b_a_log"], SM_BDEC, SMALL_W), _lane_row(p["b_dt_bias"], SM_BDEC, SMALL_W),
                jnp.tile(p["b_out_norm"], N_HEADS)[None, :], bsz, seq, tb).reshape(bsz * seq, BRANCH_WIDTH)

    br_c = _ssd(cxbc, z_all, small, p["c_conv_w"], p["c_conv_b"][None, :],
                _lane_row(p["c_a_log"], SM_CDT, SMALL_W), _lane_row(p["c_dt_bias"], SM_CDT, SMALL_W),
                jnp.repeat(p["c_d_skip"], HEAD_DIM)[None, :], p["c_out_norm"][None, :],
                bsz, seq, tb).reshape(bsz * seq, BRANCH_WIDTH)

    w2pad = jnp.zeros((SMALL_W, D_KEY), F32).at[SM_DG:SM_DG + D_GATE_RANK, :].set(p["d_gate_w2"])
    br_d = _gla(dqkv, z_all, small, w2pad, p["d_gate_b"][None, :],
                jnp.tile(p["d_out_norm"], D_HEADS)[None, :], bsz, seq, tb).reshape(bsz * seq, BRANCH_WIDTH)

    return _merge(xf, ng, os_, lses, z_all, (br_b, br_c, br_d), _bf(p["w_gate"]), p["b_gate"][:, None, :],
                  _bf(p["w_branch"]), _bf(p["w_out"]), seq, tm)


def _pick_tile(n, target):
    t = min(n, target)
    while n % t:
        t //= 2
    return t


def kernel(x, positions, norm_g, w_in, a_q_norm, a_k_norm, b_conv_w, b_a_log, b_dt_bias, b_out_norm,
           c_conv_w, c_conv_b, c_a_log, c_dt_bias, c_d_skip, c_out_norm, d_gate_w2, d_gate_b, d_out_norm,
           w_gate, b_gate, w_branch, w_out):
    bsz, seq, d_model = x.shape
    assert d_model == D_MODEL and seq % (2 * A_BLOCK * A_GROUPS[-1][1]) == 0
    xf = x.reshape(bsz * seq, d_model)
    tm = _pick_tile(seq, 512)
    assert tm % LANES == 0
    pos = positions.reshape(bsz * seq // tm, tm // LANES, LANES).astype(jnp.int32)
    tb = _pick_tile(seq, 512)
    names = ("norm_g", "w_in", "a_q_norm", "a_k_norm", "b_conv_w", "b_a_log", "b_dt_bias", "b_out_norm",
             "c_conv_w", "c_conv_b", "c_a_log", "c_dt_bias", "c_d_skip", "c_out_norm", "d_gate_w2",
             "d_gate_b", "d_out_norm", "w_gate", "b_gate", "w_branch", "w_out")
    stacked = (norm_g, w_in, a_q_norm, a_k_norm, b_conv_w, b_a_log, b_dt_bias, b_out_norm,
               c_conv_w, c_conv_b, c_a_log, c_dt_bias, c_d_skip, c_out_norm, d_gate_w2, d_gate_b,
               d_out_norm, w_gate, b_gate, w_branch, w_out)
    for layer in range(norm_g.shape[0]):
        p = {n: a[layer] for n, a in zip(names, stacked) if n != "w_in"}
        p["w_in_stacked"], p["layer"] = w_in, layer
        xf = _layer(xf, pos, bsz, seq, p, tm, tb)
    return xf.reshape(bsz, seq, d_model)
```

```python
import functools
import math

import numpy as np
import jax
import jax.numpy as jnp
from jax import lax
from jax.experimental import pallas as pl
from jax.experimental.pallas import tpu as pltpu

F32 = jnp.float32
BF16 = jnp.bfloat16

D_MODEL = 1024
N_BRANCH = 4
BRANCH_WIDTH = D_MODEL // N_BRANCH
HEAD_DIM = 64
N_HEADS = BRANCH_WIDTH // HEAD_DIM
EPS = 1e-6
ROPE_DIM = HEAD_DIM // 4
ROPE_HALF = ROPE_DIM // 2
ROPE_THETA = 500000.0
CONV_K = 4
PRO_ROWS = 128

A_GROUPS = ((128, 1), (512, 4), (2048, 16))
A_N_GROUPS = len(A_GROUPS)
A_QKV = A_N_GROUPS * BRANCH_WIDTH
A_BLOCK = 128
A_STEP_ROWS = 1024
A_UNROLL = 4

B_CHUNK = 64
B_UNROLL = 8
C_GROUPS = 2
C_STATE = 128
C_CHUNK = 64
C_UNROLL = 4
D_HEADS = 4
D_KEY = BRANCH_WIDTH // 2
D_DK = D_KEY // D_HEADS
D_DV = BRANCH_WIDTH // D_HEADS
D_GATE_RANK = 16
D_GATE_NORM = 16.0
D_CHUNK = 32
D_GROUP = 128
D_UNROLL = 2

LANES = 128
SUBLANES = 8
SLABS = BRANCH_WIDTH // LANES
SMALL_W = LANES
SM_BETA, SM_BDEC, SM_CDT, SM_DG = 0, 4, 8, 12

W_AQ, W_AK, W_AV = 0, A_QKV, 2 * A_QKV
W_Z = 3 * A_QKV
W_BQKV = W_Z + D_MODEL
W_CXBC = W_BQKV + 3 * BRANCH_WIDTH
W_DQKV = W_CXBC + 3 * BRANCH_WIDTH
W_SMALL = W_DQKV + 2 * BRANCH_WIDTH
W_TOTAL = W_SMALL + SMALL_W

VMEM_LIMIT = 56 * 1024 * 1024


def _bf(x):
    return x.astype(BF16)


def _dot(a, b):
    return jnp.dot(a, b, preferred_element_type=F32)


def _dot_nt(a, b):
    return lax.dot_general(a, b, (((1,), (1,)), ((), ())), preferred_element_type=F32)


def _dot_tn(a, b):
    return lax.dot_general(a, b, (((0,), (0,)), ((), ())), preferred_element_type=F32)


def _expand_heads(x, offset):
    rows = x.shape[0]
    head_lane = lax.broadcasted_iota(jnp.int32, (1, BRANCH_WIDTH), 1) // HEAD_DIM
    out = jnp.broadcast_to(x[:, offset:offset + 1], (rows, BRANCH_WIDTH))
    for h in range(1, N_HEADS):
        out = jnp.where(head_lane == h, jnp.broadcast_to(x[:, offset + h:offset + h + 1], (rows, BRANCH_WIDTH)), out)
    return out


def _expand_heads_mxu(x, sel):
    hi = _bf(x)
    lo = _bf(x - hi.astype(F32))
    return _dot(hi, sel) + _dot(lo, sel)


def _sel_matrix(offset):
    m = np.zeros((SMALL_W, BRANCH_WIDTH), np.float32)
    for h in range(N_HEADS):
        m[offset + h, h * HEAD_DIM:(h + 1) * HEAD_DIM] = 1.0
    return jnp.asarray(m, BF16)


def _seg_sum(x, ones_bd):
    hi = _bf(x)
    lo = _bf(x - hi.astype(F32))
    return _dot(hi, ones_bd) + _dot(lo, ones_bd)


def _seg_cumsum(x, seg):
    pos = lax.broadcasted_iota(jnp.int32, x.shape, 0) & (seg - 1)
    step = 1
    while step < seg:
        x = x + jnp.where(pos >= step, pltpu.roll(x, step, axis=0), 0.0)
        step *= 2
    return x


def _silu(x):
    h = 0.5 * x
    return h + h * jnp.tanh(h)


def _softplus(x):
    return jnp.maximum(x, 0.0) + jnp.log1p(jnp.exp(-jnp.abs(x)))


def _shift_rows(x, tail, s):
    r = pltpu.roll(x, s, axis=0)
    t = pltpu.roll(tail, s, axis=0)
    row = lax.broadcasted_iota(jnp.int32, tail.shape, 0)
    first = jnp.where(row < s, t, r[:SUBLANES])
    return jnp.concatenate([first, r[SUBLANES:]], axis=0)


def _causal_conv(x, tail, w):
    y = x * w[CONV_K - 1:CONV_K, :]
    for s in range(1, CONV_K):
        y = y + _shift_rows(x, tail, s) * w[CONV_K - 1 - s:CONV_K - s, :]
    return y


def _stack(y, bdmask16):
    return jnp.concatenate([_bf(y)] * N_HEADS, axis=0) * bdmask16


def _np_consts():
    r = np.arange(BRANCH_WIDTH)
    bd = (r[:, None] // HEAD_DIM == r[None, :] // HEAD_DIM).astype(np.float32)
    i = np.arange(B_CHUNK)[:, None]
    j = (r % HEAD_DIM)[None, :]
    xor = np.maximum(i ^ j, 1)
    lv = np.where(i > j, np.floor(np.log2(xor)).astype(np.int32), -1).astype(np.int32)
    eye_cat = (i == j).astype(np.float32)
    gmask = ((r[None, :] // C_STATE) == (r[:, None] // HEAD_DIM) // (N_HEADS // C_GROUPS)).astype(np.float32)
    dmask = ((np.arange(D_KEY)[:, None] // D_DK) == (r[None, :] // D_DV)).astype(np.float32)
    t = np.arange(D_GROUP)
    cmask = ((t[:, None] // D_CHUNK == t[None, :] // D_CHUNK) & (t[:, None] >= t[None, :])).astype(np.float32)

    return dict(bd=bd, lv=lv, eye_cat=eye_cat, gmask=gmask, dmask=dmask, cmask=cmask)


_NPC = _np_consts()


def _full_spec(shape):
    nd = len(shape)
    return pl.BlockSpec(shape, lambda *_: (0,) * nd)


def _const_spec(shape):
    nd = len(shape)
    return pl.BlockSpec(shape, lambda *_: (0,) * nd, pipeline_mode=pl.Buffered(1))


def _inproj_kernel(x_ref, pos_ref, ng_ref, w_ref, qn_ref, kn_ref, invf_ref, sgn_ref, ones_ref,
                   q0_ref, q1_ref, q2_ref, k0_ref, k1_ref, k2_ref, v0_ref, v1_ref, v2_ref,
                   z_ref, bqkv_ref, cxbc_ref, dqkv_ref, small_ref, perm_s, *, tm):
    x = x_ref[...]
    ms = jnp.mean(x * x, axis=-1, keepdims=True)
    hb = _bf(x * lax.rsqrt(ms + EPS) * ng_ref[...])

    def proj(lo, width):
        return _dot(hb, w_ref[:, lo:lo + width])

    pos_f = pos_ref[0].astype(F32)
    pos_col = jnp.concatenate([jnp.broadcast_to(pos_f[k:k + 1, :], (LANES, LANES)).T
                               for k in range(tm // LANES)], axis=0)
    ang = pos_col * invf_ref[...]
    cos = jnp.cos(ang)
    sin = jnp.sin(ang) * sgn_ref[...]
    lane = lax.broadcasted_iota(jnp.int32, (1, LANES), 1) & (HEAD_DIM - 1)
    first_half = lane < ROPE_HALF
    ones_bd = ones_ref[...]

    def emit(slabs, out_ref, which, g):
        dil = A_GROUPS[g][1]
        for s, v in enumerate(slabs):
            if dil == 1:
                out_ref[0, 0, :, s * LANES:(s + 1) * LANES] = _bf(v)
            else:
                perm_s[which, g - 1, s] = v
        if dil > 1:
            rows = tm // dil
            for s in range(len(slabs)):
                for r in range(dil):
                    out_ref[0, r, :, s * LANES:(s + 1) * LANES] = _bf(
                        perm_s[which, g - 1, s, pl.ds(r, rows, stride=dil), :])

    def qk_piece(u, ss, which, g):
        gain_ref, out_ref, scale = ((qn_ref, (q0_ref, q1_ref, q2_ref)[g], HEAD_DIM ** -0.5) if which == 0
                                    else (kn_ref, (k0_ref, k1_ref, k2_ref)[g], 1.0))
        gain = gain_ref[:, g * BRANCH_WIDTH:(g + 1) * BRANCH_WIDTH]
        un = u * lax.rsqrt(ss * (1.0 / HEAD_DIM) + EPS) * (gain * scale)
        slabs = []
        for s in range(SLABS):
            v = un[:, s * LANES:(s + 1) * LANES]
            partner = jnp.where(first_half, pltpu.roll(v, LANES - ROPE_HALF, axis=1),
                                pltpu.roll(v, ROPE_HALF, axis=1))
            slabs.append(v * cos + partner * sin)
        emit(slabs, out_ref, which, g)

    def plain_piece(dst_ref, col):
        def store(u):
            dst_ref[:, col:col + u.shape[1]] = u
        return store

    pieces = [("qk", which, g) for which in range(2) for g in range(A_N_GROUPS)]
    pieces += [("v", g) for g in range(A_N_GROUPS)]
    pieces += [("plain", plain_piece(z_ref, i * BRANCH_WIDTH)) for i in range(N_BRANCH)]
    pieces += [("plain", plain_piece(bqkv_ref, i * BRANCH_WIDTH)) for i in range(3)]
    pieces += [("plain", plain_piece(cxbc_ref, i * BRANCH_WIDTH)) for i in range(3)]
    pieces += [("plain", plain_piece(dqkv_ref, i * BRANCH_WIDTH)) for i in range(2)]
    assert len(pieces) * BRANCH_WIDTH == W_SMALL

    def finish(chunk):
        u, kinds = chunk
        sq = {i: _dot(_bf(u[:, i * BRANCH_WIDTH:(i + 1) * BRANCH_WIDTH] * u[:, i * BRANCH_WIDTH:(i + 1) * BRANCH_WIDTH]),
                      ones_bd) for i, kind in enumerate(kinds) if kind[0] == "qk"}
        for i, kind in enumerate(kinds):
            ui = u[:, i * BRANCH_WIDTH:(i + 1) * BRANCH_WIDTH]
            if kind[0] == "qk":
                qk_piece(ui, sq[i], kind[1], kind[2])
            elif kind[0] == "v":
                emit([ui[:, s * LANES:(s + 1) * LANES] for s in range(SLABS)],
                     (v0_ref, v1_ref, v2_ref)[kind[1]], 2, kind[1])
            else:
                kind[1](ui)

    pending = None
    for c0 in range(0, len(pieces), 2):
        kinds = pieces[c0:c0 + 2]
        width = len(kinds) * BRANCH_WIDTH + (SMALL_W if c0 + 2 >= len(pieces) else 0)
        u = proj(c0 * BRANCH_WIDTH, width)
        if c0 + 2 >= len(pieces):
            kinds = kinds + [("plain", plain_piece(small_ref, 0))]
        if pending is not None:
            finish(pending)
        pending = (u, kinds)
    finish(pending)


def _inproj(xf, pos, norm_g, w_all, qn, kn, invf, sgn, ones_bd, bsz, seq, tm):
    t = xf.shape[0]
    tps = seq // tm
    row = lambda w: pl.BlockSpec((tm, w), lambda i: (i, 0))
    dils = [d for _, d in A_GROUPS]
    res_spec = lambda d: pl.BlockSpec((1, d, tm // d, BRANCH_WIDTH), lambda i: (i // tps, 0, i % tps, 0))
    res_shape = lambda d: jax.ShapeDtypeStruct((bsz, d, seq // d, BRANCH_WIDTH), BF16)
    flat = [(D_MODEL, F32), (3 * BRANCH_WIDTH, F32), (3 * BRANCH_WIDTH, F32), (2 * BRANCH_WIDTH, F32),
            (SMALL_W, F32)]
    return pl.pallas_call(
        functools.partial(_inproj_kernel, tm=tm),
        grid=(t // tm,),
        in_specs=[row(D_MODEL), pl.BlockSpec((1, tm // LANES, LANES), lambda i: (i, 0, 0)),
                  _full_spec((1, D_MODEL)), _const_spec((D_MODEL, W_TOTAL)),
                  _full_spec((1, A_QKV)), _full_spec((1, A_QKV)), _full_spec((1, LANES)),
                  _full_spec((1, LANES)), _full_spec((BRANCH_WIDTH, BRANCH_WIDTH))],
        out_specs=[res_spec(d) for d in dils] * 3 + [row(w) for w, _ in flat],
        out_shape=[res_shape(d) for d in dils] * 3 + [jax.ShapeDtypeStruct((t, w), dt) for w, dt in flat],
        scratch_shapes=[pltpu.VMEM((3, A_N_GROUPS - 1, SLABS, tm, LANES), F32)],
        compiler_params=pltpu.CompilerParams(dimension_semantics=("arbitrary",),
                                             vmem_limit_bytes=VMEM_LIMIT),
        name="inproj",
    )(xf, pos, norm_g, w_all, qn, kn, invf, sgn, ones_bd)


def _attn_kernel(q_ref, k_ref, v_ref, bias_ref, o_ref, lse_ref, *, n, rb, dil):
    nb = n // A_BLOCK
    head_lane = lax.broadcasted_iota(jnp.int32, (1, BRANCH_WIDTH), 1) // HEAD_DIM
    shift = nb.bit_length() - 1
    assert nb == 1 << shift and (rb * nb) % A_UNROLL == 0

    def blocks(it, carry):
        units = [it * A_UNROLL + b for b in range(A_UNROLL)]
        rr = [u >> shift for u in units]
        js = [u & (nb - 1) for u in units]
        q0 = [pl.multiple_of(j * A_BLOCK, A_BLOCK) for j in js]
        k0 = [pl.multiple_of(jnp.maximum(j - 1, 0) * A_BLOCK, A_BLOCK) for j in js]
        s = []
        for b in range(A_UNROLL):
            q = q_ref[0, rr[b], pl.ds(q0[b], A_BLOCK), :]
            zero = jnp.zeros_like(q)
            qs = jnp.concatenate([jnp.where(head_lane == h, q, zero) for h in range(N_HEADS)], axis=0)
            s.append(_dot_nt(qs, k_ref[0, rr[b], pl.ds(k0[b], 2 * A_BLOCK), :])
                     + bias_ref[jnp.minimum(js[b], 1)])
        m = [jnp.max(x, axis=-1, keepdims=True) for x in s]
        p = [jnp.exp(x - mx) for x, mx in zip(s, m)]
        l = [jnp.sum(x, axis=-1, keepdims=True) for x in p]
        pv = [_dot(_bf(p[b]), v_ref[0, rr[b], pl.ds(k0[b], 2 * A_BLOCK), :]) * (1.0 / l[b])
              for b in range(A_UNROLL)]
        for b in range(A_UNROLL):
            lse = m[b] + jnp.log(l[b])
            o_acc = pv[b][:A_BLOCK]
            lse_acc = jnp.broadcast_to(lse[:A_BLOCK], (A_BLOCK, BRANCH_WIDTH))
            for h in range(1, N_HEADS):
                o_acc = jnp.where(head_lane == h, pv[b][h * A_BLOCK:(h + 1) * A_BLOCK], o_acc)
                lse_acc = jnp.where(head_lane == h, lse[h * A_BLOCK:(h + 1) * A_BLOCK], lse_acc)
            if dil == 1:
                rows = pl.ds(q0[b], A_BLOCK)
            else:
                rows = pl.ds(q0[b] * dil + pl.program_id(1) * rb + rr[b], A_BLOCK, stride=dil)
            for s in range(SLABS):
                o_ref[0, s, rows, :] = o_acc[:, s * LANES:(s + 1) * LANES]
                lse_ref[0, s, rows, :] = lse_acc[:, s * LANES:(s + 1) * LANES]
        return carry

    lax.fori_loop(0, rb * nb // A_UNROLL, blocks, 0)


def _attn_group(q, k, v, gi):
    window, dil = A_GROUPS[gi]
    bsz, _, n, _ = q.shape
    assert n % (2 * A_BLOCK) == 0
    rb = min(dil, max(1, A_STEP_ROWS // n))
    win = window // dil
    qi = (np.arange(N_HEADS * A_BLOCK) % A_BLOCK)[:, None]
    kj = np.arange(2 * A_BLOCK)[None, :]
    band = lambda off: np.where((qi - kj + off >= 0) & (qi - kj + off <= win), 0.0, -np.inf).astype(np.float32)
    bias = jnp.asarray(np.stack([band(0), band(A_BLOCK)]))
    blk = pl.BlockSpec((1, rb, n, BRANCH_WIDTH), lambda b, r: (b, r, 0, 0))
    seq = n * dil
    out_blk = pl.BlockSpec((1, SLABS, seq, LANES), lambda b, r: (b, 0, 0, 0))
    out_shape = jax.ShapeDtypeStruct((bsz, SLABS, seq, LANES), F32)
    return pl.pallas_call(
        functools.partial(_attn_kernel, n=n, rb=rb, dil=dil),
        grid=(bsz, dil // rb),
        in_specs=[blk] * 3 + [_full_spec(bias.shape)],
        out_specs=[out_blk, out_blk],
        out_shape=[out_shape, out_shape],
        compiler_params=pltpu.CompilerParams(dimension_semantics=("arbitrary", "arbitrary"),
                                             vmem_limit_bytes=VMEM_LIMIT),
        name=f"attn_d{dil}",
    )(q, k, v, bias)


def _attn_merge_groups(o_refs, lse_refs, z):
    slabs = []
    for s in range(SLABS):
        lses = [ref[0, s] for ref in lse_refs]
        m = jnp.maximum(jnp.maximum(lses[0], lses[1]), lses[2])
        es = [jnp.exp(l - m) for l in lses]
        acc = o_refs[0][0, s] * es[0] + o_refs[1][0, s] * es[1] + o_refs[2][0, s] * es[2]
        acc = acc / (es[0] + es[1] + es[2])
        slabs.append(_bf(acc * _silu(z[:, s * LANES:(s + 1) * LANES])))
    return jnp.concatenate(slabs, axis=1)


def _gdn_kernel(qkv_ref, z_ref, small_ref, cw_ref, alog_ref, dtb_ref, og_ref,
                ones_ref, bd_ref, lv_ref, eye_ref, selb_ref,
                out_ref,
                tail_ref, st_ref, q_s, k_s, vb_s, kb_s, ce_s, qe_s, oi_s, corr_s, add_s, *, tb):
    nchunk = tb // B_CHUNK

    @pl.when(pl.program_id(1) == 0)
    def _():
        tail_ref[...] = jnp.zeros_like(tail_ref)
        st_ref[...] = jnp.zeros_like(st_ref)

    ones_bd = ones_ref[...]
    bd16 = ones_bd
    bdmask = bd_ref[...]
    lv = lv_ref[...]
    eye_cat = eye_ref[...]

    def prologue(r, carry):
        rows = pl.ds(pl.multiple_of(r * PRO_ROWS, PRO_ROWS), PRO_ROWS)
        halo = pl.ds(pl.multiple_of(jnp.maximum(r * PRO_ROWS - SUBLANES, 0), SUBLANES), SUBLANES)
        tail = jnp.where(r == 0, tail_ref[...], qkv_ref[0, halo, :])
        sm = small_ref[0, rows, :]
        g = -jnp.exp(alog_ref[...]) * _softplus(sm + dtb_ref[...])
        ce_s[rows, :] = _expand_heads(_seg_cumsum(g, B_CHUNK), SM_BDEC)
        beta = _expand_heads_mxu(jax.nn.sigmoid(sm), selb_ref[...])
        for part in range(3):
            cols = slice(part * BRANCH_WIDTH, (part + 1) * BRANCH_WIDTH)
            y = _silu(_causal_conv(qkv_ref[0, rows, cols], tail[:, cols], cw_ref[:, cols]))
            if part == 0:
                q_s[rows, :] = y * lax.rsqrt(_seg_sum(y * y, ones_bd) + EPS) * (HEAD_DIM ** -0.5)
            elif part == 1:
                kn = y * lax.rsqrt(_seg_sum(y * y, ones_bd) + EPS)
                k_s[rows, :] = kn
                kb_s[rows, :] = kn * beta
            else:
                vb_s[rows, :] = y * beta
        return carry

    lax.fori_loop(0, tb // PRO_ROWS, prologue, 0)
    tail_ref[...] = qkv_ref[0, tb - SUBLANES:, :]

    il = lax.broadcasted_iota(jnp.int32, (B_CHUNK, BRANCH_WIDTH), 0)
    jl = lax.broadcasted_iota(jnp.int32, (B_CHUNK, BRANCH_WIDTH), 1) & (HEAD_DIM - 1)
    causal = il >= jl

    def intra_group(i, carry):
        ks = range(B_UNROLL)
        rows = [pl.ds(pl.multiple_of((i * B_UNROLL + k) * B_CHUNK, B_CHUNK), B_CHUNK) for k in ks]
        cc = [ce_s[r, :] for r in rows]
        kc = [k_s[r, :] for r in rows]
        qc = [q_s[r, :] for r in rows]
        kb = [kb_s[r, :] for r in rows]
        gamma = [jnp.exp(jnp.where(causal, c_ - jnp.sum(c_ * eye_cat, axis=0, keepdims=True), -jnp.inf))
                 for c_ in cc]
        prod = [_dot_nt(_bf(jnp.concatenate([kb[k], qc[k]], axis=0)), _stack(kc[k], bd16)) for k in ks]
        a_kk = [jnp.where(il > jl, prod[k][:B_CHUNK] * gamma[k], 0.0) for k in ks]
        a_qk = [_bf(prod[k][B_CHUNK:] * gamma[k]) for k in ks]
        t = [eye_cat - jnp.where(lv == 0, a, 0.0) for a in a_kk]
        for lb in range(1, int(math.log2(B_CHUNK))):
            x = [_dot(_bf(t[k]), _stack(jnp.where(lv == lb, a_kk[k], 0.0), bd16)) for k in ks]
            t = [t[k] - _dot(_bf(x[k]), _stack(t[k], bd16)) for k in ks]
        ecum = [jnp.exp(c_) for c_ in cc]
        uw = [_dot(_bf(t[k]), jnp.concatenate([_stack(vb_s[rows[k], :], bd16),
                                               _stack(kb[k] * ecum[k], bd16)], axis=1)) for k in ks]
        u = [a[:, :BRANCH_WIDTH] for a in uw]
        w = [a[:, BRANCH_WIDTH:] for a in uw]
        both = [_dot(a_qk[k], jnp.concatenate([_stack(u[k], bd16), _stack(w[k], bd16)], axis=1)) for k in ks]
        k_dec = [_bf(kc[k] * jnp.exp(cc[k][B_CHUNK - 1:B_CHUNK, :] - cc[k])) for k in ks]
        wu = [_dot_tn(_bf(jnp.concatenate([w[k], u[k]], axis=1)), k_dec[k]) for k in ks]
        for k in ks:
            oi_s[rows[k], :] = both[k][:, :BRANCH_WIDTH]
            qe_s[rows[k], :] = qc[k] * ecum[k] - both[k][:, BRANCH_WIDTH:]
            corr_s[i * B_UNROLL + k] = _bf(-(wu[k][:BRANCH_WIDTH] * bdmask))
            add_s[i * B_UNROLL + k] = wu[k][BRANCH_WIDTH:] * bdmask
        return carry

    lax.fori_loop(0, nchunk // B_UNROLL, intra_group, 0)

    og = og_ref[...]
    st = st_ref[...]
    for c in range(nchunk):
        rows = slice(c * B_CHUNK, (c + 1) * B_CHUNK)
        st16 = _bf(st)
        clast = ce_s[(c + 1) * B_CHUNK - 1:(c + 1) * B_CHUNK, :]
        st = jnp.exp(clast) * st + _dot(st16, corr_s[c]) + add_s[c]
        o = _dot_nt(_bf(qe_s[rows, :]), st16) + oi_s[rows, :]
        ms = _seg_sum(o * o, ones_bd) * (1.0 / HEAD_DIM)
        on = o * lax.rsqrt(ms + EPS) * og
        out_ref[0, rows, :] = _bf(on * _silu(z_ref[0, rows, :]))
    st_ref[...] = st


def _gdn(bqkv, z_all, small, conv_w, alog_row, dtb_row, og_row, bsz, seq, tb):
    c = _NPC
    nchunk = tb // B_CHUNK
    assert nchunk % B_UNROLL == 0
    seq_blk = lambda w, col: pl.BlockSpec((1, tb, w), lambda b, j: (b, j, col))
    consts = [jnp.asarray(c["bd"], BF16), jnp.asarray(c["bd"]), jnp.asarray(c["lv"]), jnp.asarray(c["eye_cat"]),
              _sel_matrix(SM_BETA)]
    act = pltpu.VMEM((tb, BRANCH_WIDTH), F32)
    sq = (nchunk, BRANCH_WIDTH, BRANCH_WIDTH)
    return pl.pallas_call(
        functools.partial(_gdn_kernel, tb=tb),
        grid=(bsz, seq // tb),
        in_specs=[seq_blk(3 * BRANCH_WIDTH, 0), seq_blk(BRANCH_WIDTH, 1), seq_blk(SMALL_W, 0),
                  _full_spec(conv_w.shape), _full_spec((1, SMALL_W)), _full_spec((1, SMALL_W)),
                  _full_spec((1, BRANCH_WIDTH))] + [_full_spec(a.shape) for a in consts],
        out_specs=seq_blk(BRANCH_WIDTH, 0),
        out_shape=jax.ShapeDtypeStruct((bsz, seq, BRANCH_WIDTH), BF16),
        scratch_shapes=[pltpu.VMEM((SUBLANES, 3 * BRANCH_WIDTH), F32),
                        pltpu.VMEM((BRANCH_WIDTH, BRANCH_WIDTH), F32)] + [act] * 7
                       + [pltpu.VMEM(sq, BF16), pltpu.VMEM(sq, F32)],
        compiler_params=pltpu.CompilerParams(dimension_semantics=("arbitrary", "arbitrary"),
                                             vmem_limit_bytes=VMEM_LIMIT),
        name="gdn",
    )(bqkv.reshape(bsz, seq, -1), z_all.reshape(bsz, seq, -1), small.reshape(bsz, seq, -1),
      conv_w, alog_row, dtb_row, og_row, *consts)


def _ssd_kernel(xbc_ref, z_ref, small_ref, cw_ref, cb_ref, alog_ref, dtb_ref, dskip_ref, og_ref,
                bd_ref, eye_ref, gm_ref, gmt_ref, seld_ref,
                out_ref,
                tail_ref, st_ref, xs_s, bm_s, cm_s, dt_s, ac_s, *, tb):
    nchunk = tb // C_CHUNK

    @pl.when(pl.program_id(1) == 0)
    def _():
        tail_ref[...] = jnp.zeros_like(tail_ref)
        st_ref[...] = jnp.zeros_like(st_ref)

    bd16 = bd_ref[...]
    eye_cat = eye_ref[...]
    gmask16 = gm_ref[...]
    gmask_t = gmt_ref[...]

    def prologue(r, carry):
        rows = pl.ds(pl.multiple_of(r * PRO_ROWS, PRO_ROWS), PRO_ROWS)
        halo = pl.ds(pl.multiple_of(jnp.maximum(r * PRO_ROWS - SUBLANES, 0), SUBLANES), SUBLANES)
        tail = jnp.where(r == 0, tail_ref[...], xbc_ref[0, halo, :])
        for part, dst in enumerate((xs_s, bm_s, cm_s)):
            cols = slice(part * BRANCH_WIDTH, (part + 1) * BRANCH_WIDTH)
            dst[rows, :] = _silu(_causal_conv(xbc_ref[0, rows, cols], tail[:, cols], cw_ref[:, cols])
                                 + cb_ref[:, cols])
        return carry

    lax.fori_loop(0, tb // PRO_ROWS, prologue, 0)
    tail_ref[...] = xbc_ref[0, tb - SUBLANES:, :]
    dt = _softplus(small_ref[0] + dtb_ref[...])
    a = -jnp.exp(alog_ref[...]) * dt
    dt_s[...] = _expand_heads_mxu(dt, seld_ref[...])
    ac_s[...] = _expand_heads(_seg_cumsum(a, C_CHUNK), SM_CDT)

    il = lax.broadcasted_iota(jnp.int32, (C_CHUNK, BRANCH_WIDTH), 0)
    jl = lax.broadcasted_iota(jnp.int32, (C_CHUNK, BRANCH_WIDTH), 1) & (HEAD_DIM - 1)
    causal = il >= jl
    og = og_ref[...]
    dskip = dskip_ref[...]
    half = BRANCH_WIDTH // C_GROUPS

    def chunk_group(i, carry):
        ks = range(C_UNROLL)
        rows = [pl.ds(pl.multiple_of((i * C_UNROLL + k) * C_CHUNK, C_CHUNK), C_CHUNK) for k in ks]
        xc = [xs_s[r, :] for r in rows]
        ac = [ac_s[r, :] for r in rows]
        cc16 = [_bf(cm_s[r, :]) for r in rows]
        bc16 = [_bf(bm_s[r, :]) for r in rows]
        xdt = [xc[k] * dt_s[rows[k], :] for k in ks]
        seg = [jnp.exp(jnp.where(causal, a - jnp.sum(a * eye_cat, axis=0, keepdims=True), -jnp.inf)) for a in ac]
        scores = [_dot_nt(cc16[k], jnp.concatenate([bc16[k]] * N_HEADS, axis=0) * gmask16) * seg[k] for k in ks]
        y_diag = [_dot(_bf(scores[k]), _stack(xdt[k], bd16)) for k in ks]
        alast = [a[C_CHUNK - 1:C_CHUNK, :] for a in ac]
        delta = [gmask_t * _dot_tn(bc16[k], _bf(xdt[k] * jnp.exp(alast[k] - ac[k]))) for k in ks]
        st = st_ref[...]
        sts = []
        for k in ks:
            sts.append(_bf(st))
            st = jnp.exp(alast[k]) * st + delta[k]
        st_ref[...] = st
        y_off = [_dot(cc16[k], sts[k]) * jnp.exp(ac[k]) for k in ks]
        for k in ks:
            yv = (y_diag[k] + y_off[k] + dskip * xc[k]) * _silu(z_ref[0, rows[k], :])
            for gi in range(C_GROUPS):
                yg = yv[:, gi * half:(gi + 1) * half]
                ms = jnp.mean(yg * yg, axis=-1, keepdims=True)
                out_ref[0, rows[k], gi * half:(gi + 1) * half] = _bf(
                    yg * lax.rsqrt(ms + EPS) * og[:, gi * half:(gi + 1) * half])
        return carry

    lax.fori_loop(0, nchunk // C_UNROLL, chunk_group, 0)


def _ssd(cxbc, z_all, small, conv_w, conv_b, alog_row, dtb_row, dskip_row, og_row, bsz, seq, tb):
    c = _NPC
    assert (tb // C_CHUNK) % C_UNROLL == 0
    seq_blk = lambda w, col: pl.BlockSpec((1, tb, w), lambda b, j: (b, j, col))
    consts = [jnp.asarray(c["bd"], BF16), jnp.asarray(c["eye_cat"]), jnp.asarray(c["gmask"], BF16),
              jnp.asarray(c["gmask"].T.copy()), _sel_matrix(SM_CDT)]
    act = pltpu.VMEM((tb, BRANCH_WIDTH), F32)
    return pl.pallas_call(
        functools.partial(_ssd_kernel, tb=tb),
        grid=(bsz, seq // tb),
        in_specs=[seq_blk(3 * BRANCH_WIDTH, 0), seq_blk(BRANCH_WIDTH, 2), seq_blk(SMALL_W, 0),
                  _full_spec(conv_w.shape), _full_spec((1, 3 * BRANCH_WIDTH)), _full_spec((1, SMALL_W)),
                  _full_spec((1, SMALL_W)), _full_spec((1, BRANCH_WIDTH)), _full_spec((1, BRANCH_WIDTH))]
                 + [_full_spec(a.shape) for a in consts],
        out_specs=seq_blk(BRANCH_WIDTH, 0),
        out_shape=jax.ShapeDtypeStruct((bsz, seq, BRANCH_WIDTH), BF16),
        scratch_shapes=[pltpu.VMEM((SUBLANES, 3 * BRANCH_WIDTH), F32),
                        pltpu.VMEM((BRANCH_WIDTH, BRANCH_WIDTH), F32)] + [act] * 5,
        compiler_params=pltpu.CompilerParams(dimension_semantics=("arbitrary", "arbitrary"),
                                             vmem_limit_bytes=VMEM_LIMIT),
        name="ssd",
    )(cxbc.reshape(bsz, seq, -1), z_all.reshape(bsz, seq, -1), small.reshape(bsz, seq, -1),
      conv_w, conv_b, alog_row, dtb_row, dskip_row, og_row, *consts)


def _gla_kernel(qkv_ref, z_ref, small_ref, w2_ref, bg_ref, og_ref, ones_ref, dm_ref, cm_ref,
                out_ref,
                st_ref, qg_s, kg_s, k_s, cum_s, *, tb):
    @pl.when(pl.program_id(1) == 0)
    def _():
        st_ref[...] = jnp.zeros_like(st_ref)

    qkv = qkv_ref[0]
    q = qkv[:, :D_KEY] * (D_DK ** -0.5)
    k = qkv[:, D_KEY:2 * D_KEY]
    sm = small_ref[0]
    s_hi = _bf(sm)
    s_lo = _bf(sm - s_hi.astype(F32))
    w2 = w2_ref[...]
    w_hi = _bf(w2)
    w_lo = _bf(w2 - w_hi.astype(F32))
    gl = _dot(s_hi, w_hi) + _dot(s_lo, w_hi) + _dot(s_hi, w_lo) + bg_ref[...]
    glog = jax.nn.log_sigmoid(gl) * (1.0 / D_GATE_NORM)
    cum = _seg_cumsum(glog, D_CHUNK)
    qg_s[...] = q * jnp.exp(cum)
    kg_s[...] = k * jnp.exp(-cum)
    k_s[...] = k
    cum_s[...] = cum

    cmask = cm_ref[...] > 0.0
    klane = lax.broadcasted_iota(jnp.int32, (1, D_KEY), 1) // D_DK
    vlane = lax.broadcasted_iota(jnp.int32, (1, BRANCH_WIDTH), 1) // D_DV
    dmask = dm_ref[...]
    ones_bd = ones_ref[...]
    og = og_ref[...]
    per_group = D_GROUP // D_CHUNK

    cmask4 = jnp.concatenate([cmask] * D_HEADS, axis=0)

    def groups(it, carry):
        gs = range(D_UNROLL)
        rows = [pl.ds(pl.multiple_of((it * D_UNROLL + g) * D_GROUP, D_GROUP), D_GROUP) for g in gs]
        qgg = [qg_s[r, :] for r in rows]
        cc = [cum_s[r, :] for r in rows]
        vg = [_bf(qkv_ref[0, r, 2 * D_KEY:]) for r in rows]
        a = []
        for g in gs:
            zero = jnp.zeros_like(qgg[g])
            qs = _bf(jnp.concatenate([jnp.where(klane == h, qgg[g], zero) for h in range(D_HEADS)], axis=0))
            a.append(_dot_nt(qs, _bf(kg_s[rows[g], :])))
        av = [_dot(_bf(jnp.where(cmask4, a[g], 0.0)), vg[g]) for g in gs]
        chunks = [(g, c) for g in gs for c in range(per_group)]
        clast, delta = {}, {}
        for g, c in chunks:
            sl = slice(c * D_CHUNK, (c + 1) * D_CHUNK)
            ccc = cc[g][sl]
            clast[g, c] = ccc[D_CHUNK - 1:D_CHUNK, :]
            kd = k_s[rows[g], :][sl] * jnp.exp(clast[g, c] - ccc)
            delta[g, c] = dmask * _dot_tn(_bf(kd), vg[g][sl])
        dcol = {}
        for g in gs:
            dec_rows = jnp.concatenate([jnp.broadcast_to(jnp.exp(clast[g, c]), (D_CHUNK, D_KEY))
                                        for c in range(per_group)], axis=0)
            dec_t = dec_rows.T
            for c in range(per_group):
                dcol[g, c] = dec_t[:, c * D_CHUNK:c * D_CHUNK + 1]
        st = st_ref[...]
        sts = {}
        for g, c in chunks:
            sts[g, c] = _bf(st)
            st = dcol[g, c] * st + delta[g, c]
        st_ref[...] = st
        for g in gs:
            o_inter = jnp.concatenate([_dot(_bf(qgg[g][c * D_CHUNK:(c + 1) * D_CHUNK]), sts[g, c])
                                       for c in range(per_group)], axis=0)
            o_intra = av[g][:D_GROUP]
            for h in range(1, D_HEADS):
                o_intra = jnp.where(vlane == h, av[g][h * D_GROUP:(h + 1) * D_GROUP], o_intra)
            o = o_inter + o_intra
            ms = _seg_sum(o * o, ones_bd) * (1.0 / D_DV)
            on = o * lax.rsqrt(ms + EPS) * og
            out_ref[0, rows[g], :] = _bf(on * _silu(z_ref[0, rows[g], :]))
        return carry

    lax.fori_loop(0, tb // (D_GROUP * D_UNROLL), groups, 0)


def _gla(dqkv, z_all, small, w2pad, bg_row, og_row, bsz, seq, tb):
    c = _NPC
    seq_blk = lambda w, col: pl.BlockSpec((1, tb, w), lambda b, j: (b, j, col))
    consts = [jnp.asarray(c["bd"], BF16), jnp.asarray(c["dmask"]), jnp.asarray(c["cmask"])]
    return pl.pallas_call(
        functools.partial(_gla_kernel, tb=tb),
        grid=(bsz, seq // tb),
        in_specs=[seq_blk(2 * BRANCH_WIDTH, 0), seq_blk(BRANCH_WIDTH, 3), seq_blk(SMALL_W, 0),
                  _full_spec((SMALL_W, D_KEY)), _full_spec((1, D_KEY)), _full_spec((1, BRANCH_WIDTH))]
                 + [_full_spec(a.shape) for a in consts],
        out_specs=seq_blk(BRANCH_WIDTH, 0),
        out_shape=jax.ShapeDtypeStruct((bsz, seq, BRANCH_WIDTH), BF16),
        scratch_shapes=[pltpu.VMEM((D_KEY, BRANCH_WIDTH), F32)] + [pltpu.VMEM((tb, D_KEY), F32)] * 4,
        compiler_params=pltpu.CompilerParams(dimension_semantics=("arbitrary", "arbitrary"),
                                             vmem_limit_bytes=VMEM_LIMIT),
        name="gla",
    )(dqkv.reshape(bsz, seq, -1), z_all.reshape(bsz, seq, -1), small.reshape(bsz, seq, -1),
      w2pad, bg_row, og_row, *consts)


def _merge_kernel(x_ref, ng_ref, o0, o1, o2, l0, l1, l2, za_ref, br1, br2, br3,
                  wg_ref, bg_ref, wb_ref, wo_ref, out_ref):
    proj = {i: _dot(br[...], wb_ref[i]) for i, br in ((1, br1), (2, br2), (3, br3))}
    x = x_ref[...]
    ms = jnp.mean(x * x, axis=-1, keepdims=True)
    hb = _bf(x * lax.rsqrt(ms + EPS) * ng_ref[...])
    merged = jnp.zeros(x.shape, F32)
    for i in (1, 2, 3):
        merged = merged + jax.nn.sigmoid(_dot(hb, wg_ref[i]) + bg_ref[i]) * proj[i]
    br_a = _attn_merge_groups((o0, o1, o2), (l0, l1, l2), za_ref[...])
    merged = merged + jax.nn.sigmoid(_dot(hb, wg_ref[0]) + bg_ref[0]) * _dot(br_a, wb_ref[0])
    out_ref[...] = x + _dot(_bf(merged), wo_ref[...])


def _merge(xf, norm_g, attn_os, attn_lses, z_all, brs, w_gate, b_gate, w_branch, w_out, seq, tm):
    t = xf.shape[0]
    tps = seq // tm
    row = lambda w: pl.BlockSpec((tm, w), lambda i: (i, 0))
    slab = pl.BlockSpec((1, SLABS, tm, LANES), lambda i: (i // tps, 0, i % tps, 0))
    return pl.pallas_call(
        _merge_kernel,
        grid=(t // tm,),
        in_specs=[row(D_MODEL), _full_spec((1, D_MODEL))] + [slab] * (2 * A_N_GROUPS) + [row(BRANCH_WIDTH)]
                 + [row(BRANCH_WIDTH)] * (N_BRANCH - 1)
                 + [_const_spec(w_gate.shape), _full_spec(b_gate.shape), _const_spec(w_branch.shape),
                    _const_spec(w_out.shape)],
        out_specs=row(D_MODEL),
        out_shape=jax.ShapeDtypeStruct((t, D_MODEL), F32),
        compiler_params=pltpu.CompilerParams(dimension_semantics=("arbitrary",),
                                             vmem_limit_bytes=VMEM_LIMIT),
        name="merge",
    )(xf, norm_g, *attn_os, *attn_lses, z_all, *brs, w_gate, b_gate, w_branch, w_out)


def _lane_row(vals, offset, width):
    return jnp.zeros((1, width), F32).at[0, offset:offset + vals.shape[0]].set(vals.astype(F32))


def _rope_rows():
    lane = np.arange(LANES) % HEAD_DIM
    inv_freq = ROPE_THETA ** (-jnp.arange(ROPE_HALF, dtype=F32) / ROPE_HALF)
    invf = jnp.where(lane < ROPE_DIM, inv_freq[lane % ROPE_HALF], 0.0).astype(F32)[None, :]
    sgn = np.where(lane < ROPE_HALF, -1.0, np.where(lane < ROPE_DIM, 1.0, 0.0)).astype(np.float32)[None, :]
    return invf, jnp.asarray(sgn)


def _w_in_layout():
    sizes = (A_QKV, A_QKV, A_QKV, BRANCH_WIDTH, 3 * BRANCH_WIDTH, BRANCH_WIDTH, N_HEADS, N_HEADS,
             3 * BRANCH_WIDTH, BRANCH_WIDTH, N_HEADS, D_KEY, D_KEY, BRANCH_WIDTH, BRANCH_WIDTH, D_GATE_RANK)
    names = ("a_q", "a_k", "a_v", "a_z", "b_qkv", "b_z", "b_beta", "b_a", "c_xbc", "c_z", "c_dt",
             "d_q", "d_k", "d_v", "d_z", "d_g")
    src = dict(zip(names, np.concatenate([[0], np.cumsum(sizes)[:-1]]).tolist()))
    total = int(sum(sizes))
    moves = [(W_AQ, src["a_q"], 3 * A_QKV + BRANCH_WIDTH),
             (W_Z + BRANCH_WIDTH, src["b_z"], BRANCH_WIDTH),
             (W_Z + 2 * BRANCH_WIDTH, src["c_z"], BRANCH_WIDTH),
             (W_Z + 3 * BRANCH_WIDTH, src["d_z"], BRANCH_WIDTH),
             (W_BQKV, src["b_qkv"], 3 * BRANCH_WIDTH),
             (W_CXBC, src["c_xbc"], 3 * BRANCH_WIDTH),
             (W_DQKV, src["d_q"], 2 * BRANCH_WIDTH)]
    small = []
    for lane0, name, width in ((SM_BETA, "b_beta", 2 * N_HEADS), (SM_CDT, "c_dt", N_HEADS),
                               (SM_DG, "d_g", D_GATE_RANK)):
        win = min(src[name] - lane0, total - SMALL_W)
        small.append((lane0 + width, win, (lane0 - (src[name] - win)) % SMALL_W))
    return total, moves, small


def _pack_kernel(w_ref, out_ref):
    _, moves, small = _w_in_layout()
    for dst, src, width in moves:
        out_ref[:, dst:dst + width] = _bf(w_ref[:, src:src + width])
    lane = lax.broadcasted_iota(jnp.int32, (1, SMALL_W), 1)
    blk = jnp.zeros((w_ref.shape[0], SMALL_W), F32)
    for lane_end, win, shift in reversed(small):
        piece = w_ref[:, win:win + SMALL_W]
        if shift:
            piece = pltpu.roll(piece, shift, axis=1)
        blk = jnp.where(lane < lane_end, piece, blk)
    out_ref[:, W_SMALL:W_SMALL + SMALL_W] = _bf(blk)


def _pack_w_in(w_in, layer):
    total, _, _ = _w_in_layout()
    assert w_in.shape[1:] == (D_MODEL, total)
    rows = D_MODEL // 4
    return pl.pallas_call(
        _pack_kernel,
        grid=(D_MODEL // rows,),
        in_specs=[pl.BlockSpec((None, rows, total), lambda i: (layer, i, 0))],
        out_specs=pl.BlockSpec((rows, W_TOTAL), lambda i: (i, 0)),
        out_shape=jax.ShapeDtypeStruct((D_MODEL, W_TOTAL), BF16),
        compiler_params=pltpu.CompilerParams(dimension_semantics=("arbitrary",)),
        name="pack_w_in",
    )(w_in)


def _layer(xf, pos, bsz, seq, p, tm, tb):
    invf, sgn = _rope_rows()
    ones_bd = jnp.asarray(_NPC["bd"], BF16)
    w_all = _pack_w_in(p["w_in_stacked"], p["layer"])
    ng = p["norm_g"][None, :]
    qn = jnp.tile(p["a_q_norm"], A_QKV // HEAD_DIM)[None, :]
    kn = jnp.tile(p["a_k_norm"], A_QKV // HEAD_DIM)[None, :]
    outs = _inproj(xf, pos, ng, w_all, qn, kn, invf, sgn, ones_bd, bsz, seq, tm)
    qs, ks, vs = outs[0:3], outs[3:6], outs[6:9]
    z_all, bqkv, cxbc, dqkv, small = outs[9:]

    os_, lses = [], []
    for gi in range(A_N_GROUPS):
        o, lse = _attn_group(qs[gi], ks[gi], vs[gi], gi)
        os_.append(o)
        lses.append(lse)

    br_b = _gdn(bqkv, z_all, small, p["b_conv_w"],
                _lane_row(p["b_a_log"], SM_BDEC, SMALL_W), _lane_row(p["b_dt_bias"], SM_BDEC, SMALL_W),
                jnp.tile(p["b_out_norm"], N_HEADS)[None, :], bsz, seq, tb).reshape(bsz * seq, BRANCH_WIDTH)

    br_c = _ssd(cxbc, z_all, small, p["c_conv_w"], p["c_conv_b"][None, :],
                _lane_row(p["c_a_log"], SM_CDT, SMALL_W), _lane_row(p["c_dt_bias"], SM_CDT, SMALL_W),
                jnp.repeat(p["c_d_skip"], HEAD_DIM)[None, :], p["c_out_norm"][None, :],
                bsz, seq, tb).reshape(bsz * seq, BRANCH_WIDTH)

    w2pad = jnp.zeros((SMALL_W, D_KEY), F32).at[SM_DG:SM_DG + D_GATE_RANK, :].set(p["d_gate_w2"])
    br_d = _gla(dqkv, z_all, small, w2pad, p["d_gate_b"][None, :],
                jnp.tile(p["d_out_norm"], D_HEADS)[None, :], bsz, seq, tb).reshape(bsz * seq, BRANCH_WIDTH)

    return _merge(xf, ng, os_, lses, z_all, (br_b, br_c, br_d), _bf(p["w_gate"]), p["b_gate"][:, None, :],
                  _bf(p["w_branch"]), _bf(p["w_out"]), seq, tm)


def _pick_tile(n, target):
    t = min(n, target)
    while n % t:
        t //= 2
    return t


def kernel(x, positions, norm_g, w_in, a_q_norm, a_k_norm, b_conv_w, b_a_log, b_dt_bias, b_out_norm,
           c_conv_w, c_conv_b, c_a_log, c_dt_bias, c_d_skip, c_out_norm, d_gate_w2, d_gate_b, d_out_norm,
           w_gate, b_gate, w_branch, w_out):
    bsz, seq, d_model = x.shape
    assert d_model == D_MODEL and seq % (2 * A_BLOCK * A_GROUPS[-1][1]) == 0
    xf = x.reshape(bsz * seq, d_model)
    tm = _pick_tile(seq, 512)
    assert tm % LANES == 0
    pos = positions.reshape(bsz * seq // tm, tm // LANES, LANES).astype(jnp.int32)
    tb = _pick_tile(seq, 512)
    names = ("norm_g", "w_in", "a_q_norm", "a_k_norm", "b_conv_w", "b_a_log", "b_dt_bias", "b_out_norm",
             "c_conv_w", "c_conv_b", "c_a_log", "c_dt_bias", "c_d_skip", "c_out_norm", "d_gate_w2",
             "d_gate_b", "d_out_norm", "w_gate", "b_gate", "w_branch", "w_out")
    stacked = (norm_g, w_in, a_q_norm, a_k_norm, b_conv_w, b_a_log, b_dt_bias, b_out_norm,
               c_conv_w, c_conv_b, c_a_log, c_dt_bias, c_d_skip, c_out_norm, d_gate_w2, d_gate_b,
               d_out_norm, w_gate, b_gate, w_branch, w_out)
    for layer in range(norm_g.shape[0]):
        p = {n: a[layer] for n, a in zip(names, stacked) if n != "w_in"}
        p["w_in_stacked"], p["layer"] = w_in, layer
        xf = _layer(xf, pos, bsz, seq, p, tm, tb)
    return xf.reshape(bsz, seq, d_model)
```

```python
import functools
import math

import numpy as np
import jax
import jax.numpy as jnp
from jax import lax
from jax.experimental import pallas as pl
from jax.experimental.pallas import tpu as pltpu

F32 = jnp.float32
BF16 = jnp.bfloat16

D_MODEL = 1024
N_BRANCH = 4
BRANCH_WIDTH = D_MODEL // N_BRANCH
HEAD_DIM = 64
N_HEADS = BRANCH_WIDTH // HEAD_DIM
EPS = 1e-6
ROPE_DIM = HEAD_DIM // 4
ROPE_HALF = ROPE_DIM // 2
ROPE_THETA = 500000.0
CONV_K = 4
PRO_ROWS = 128
DENSE_TILE = 512
MIXER_TILE = 1024

A_GROUPS = ((128, 1), (512, 4), (2048, 16))
A_N_GROUPS = len(A_GROUPS)
A_QKV = A_N_GROUPS * BRANCH_WIDTH
A_BLOCK = 128
A_STEP_ROWS = 4096
A_UNROLL = 4

B_CHUNK = 64
B_UNROLL = 8
C_GROUPS = 2
C_STATE = 128
C_CHUNK = 64
C_UNROLL = 4
D_HEADS = 4
D_KEY = BRANCH_WIDTH // 2
D_DK = D_KEY // D_HEADS
D_DV = BRANCH_WIDTH // D_HEADS
D_GATE_RANK = 16
D_GATE_NORM = 16.0
D_CHUNK = 32
D_GROUP = 128
D_UNROLL = 2

LANES = 128
SUBLANES = 8
SLABS = BRANCH_WIDTH // LANES
SMALL_W = LANES
SM_BETA, SM_BDEC, SM_CDT, SM_DG = 0, 4, 8, 12

W_AQ, W_AK, W_AV = 0, A_QKV, 2 * A_QKV
W_Z = 3 * A_QKV
W_BQKV = W_Z + D_MODEL
W_CXBC = W_BQKV + 3 * BRANCH_WIDTH
W_DQKV = W_CXBC + 3 * BRANCH_WIDTH
W_SMALL = W_DQKV + 2 * BRANCH_WIDTH
W_TOTAL = W_SMALL + SMALL_W

VMEM_LIMIT = 56 * 1024 * 1024


def _bf(x):
    return x.astype(BF16)


def _dot(a, b):
    return jnp.dot(a, b, preferred_element_type=F32)


def _dot_nt(a, b):
    return lax.dot_general(a, b, (((1,), (1,)), ((), ())), preferred_element_type=F32)


def _dot_tn(a, b):
    return lax.dot_general(a, b, (((0,), (0,)), ((), ())), preferred_element_type=F32)


def _expand_heads(x, offset):
    rows = x.shape[0]
    head_lane = lax.broadcasted_iota(jnp.int32, (1, BRANCH_WIDTH), 1) // HEAD_DIM
    out = jnp.broadcast_to(x[:, offset:offset + 1], (rows, BRANCH_WIDTH))
    for h in range(1, N_HEADS):
        out = jnp.where(head_lane == h, jnp.broadcast_to(x[:, offset + h:offset + h + 1], (rows, BRANCH_WIDTH)), out)
    return out


def _expand_heads_mxu(x, sel):
    hi = _bf(x)
    lo = _bf(x - hi.astype(F32))
    return _dot(hi, sel) + _dot(lo, sel)


def _sel_matrix(offset):
    m = np.zeros((SMALL_W, BRANCH_WIDTH), np.float32)
    for h in range(N_HEADS):
        m[offset + h, h * HEAD_DIM:(h + 1) * HEAD_DIM] = 1.0
    return jnp.asarray(m, BF16)


def _seg_sum(x, ones_bd):
    hi = _bf(x)
    lo = _bf(x - hi.astype(F32))
    return _dot(hi, ones_bd) + _dot(lo, ones_bd)


def _seg_cumsum(x, seg):
    pos = lax.broadcasted_iota(jnp.int32, x.shape, 0) & (seg - 1)
    step = 1
    while step < seg:
        x = x + jnp.where(pos >= step, pltpu.roll(x, step, axis=0), 0.0)
        step *= 2
    return x


def _silu(x):
    h = 0.5 * x
    return h + h * jnp.tanh(h)


def _softplus(x):
    return jnp.maximum(x, 0.0) + jnp.log1p(jnp.exp(-jnp.abs(x)))


def _shift_rows(x, tail, s):
    r = pltpu.roll(x, s, axis=0)
    t = pltpu.roll(tail, s, axis=0)
    row = lax.broadcasted_iota(jnp.int32, tail.shape, 0)
    first = jnp.where(row < s, t, r[:SUBLANES])
    return jnp.concatenate([first, r[SUBLANES:]], axis=0)


def _causal_conv(x, tail, w):
    y = x * w[CONV_K - 1:CONV_K, :]
    for s in range(1, CONV_K):
        y = y + _shift_rows(x, tail, s) * w[CONV_K - 1 - s:CONV_K - s, :]
    return y


def _stack(y, bdmask16):
    return jnp.concatenate([_bf(y)] * N_HEADS, axis=0) * bdmask16


def _np_consts():
    r = np.arange(BRANCH_WIDTH)
    bd = (r[:, None] // HEAD_DIM == r[None, :] // HEAD_DIM).astype(np.float32)
    i = np.arange(B_CHUNK)[:, None]
    j = (r % HEAD_DIM)[None, :]
    xor = np.maximum(i ^ j, 1)
    lv = np.where(i > j, np.floor(np.log2(xor)).astype(np.int32), -1).astype(np.int32)
    eye_cat = (i == j).astype(np.float32)
    gmask = ((r[None, :] // C_STATE) == (r[:, None] // HEAD_DIM) // (N_HEADS // C_GROUPS)).astype(np.float32)
    dmask = ((np.arange(D_KEY)[:, None] // D_DK) == (r[None, :] // D_DV)).astype(np.float32)
    t = np.arange(D_GROUP)
    cmask = ((t[:, None] // D_CHUNK == t[None, :] // D_CHUNK) & (t[:, None] >= t[None, :])).astype(np.float32)

    return dict(bd=bd, lv=lv, eye_cat=eye_cat, gmask=gmask, dmask=dmask, cmask=cmask)


_NPC = _np_consts()


def _full_spec(shape):
    nd = len(shape)
    return pl.BlockSpec(shape, lambda *_: (0,) * nd)


def _const_spec(shape):
    nd = len(shape)
    return pl.BlockSpec(shape, lambda *_: (0,) * nd, pipeline_mode=pl.Buffered(1))


def _inproj_kernel(x_ref, pos_ref, ng_ref, w_ref, qn_ref, kn_ref, invf_ref, sgn_ref, ones_ref,
                   q0_ref, q1_ref, q2_ref, k0_ref, k1_ref, k2_ref, v0_ref, v1_ref, v2_ref,
                   z_ref, bqkv_ref, cxbc_ref, dqkv_ref, small_ref, perm_s, *, tm):
    x = x_ref[...]
    ms = jnp.mean(x * x, axis=-1, keepdims=True)
    hb = _bf(x * lax.rsqrt(ms + EPS) * ng_ref[...])

    def proj(lo, width):
        return _dot(hb, w_ref[:, lo:lo + width])

    pos_f = pos_ref[0].astype(F32)
    pos_col = jnp.concatenate([jnp.broadcast_to(pos_f[k:k + 1, :], (LANES, LANES)).T
                               for k in range(tm // LANES)], axis=0)
    ang = pos_col * invf_ref[...]
    cos = jnp.cos(ang)
    sin = jnp.sin(ang) * sgn_ref[...]
    lane = lax.broadcasted_iota(jnp.int32, (1, LANES), 1) & (HEAD_DIM - 1)
    first_half = lane < ROPE_HALF
    ones_bd = ones_ref[...]

    def emit(slabs, out_ref, which, g):
        dil = A_GROUPS[g][1]
        for s, v in enumerate(slabs):
            if dil == 1:
                out_ref[0, 0, :, s * LANES:(s + 1) * LANES] = _bf(v)
            else:
                perm_s[which, g - 1, s] = v
        if dil > 1:
            rows = tm // dil
            for s in range(len(slabs)):
                for r in range(dil):
                    out_ref[0, r, :, s * LANES:(s + 1) * LANES] = _bf(
                        perm_s[which, g - 1, s, pl.ds(r, rows, stride=dil), :])

    def qk_piece(u, ss, which, g):
        gain_ref, out_ref, scale = ((qn_ref, (q0_ref, q1_ref, q2_ref)[g], HEAD_DIM ** -0.5) if which == 0
                                    else (kn_ref, (k0_ref, k1_ref, k2_ref)[g], 1.0))
        gain = gain_ref[:, g * BRANCH_WIDTH:(g + 1) * BRANCH_WIDTH]
        un = u * lax.rsqrt(ss * (1.0 / HEAD_DIM) + EPS) * (gain * scale)
        slabs = []
        for s in range(SLABS):
            v = un[:, s * LANES:(s + 1) * LANES]
            partner = jnp.where(first_half, pltpu.roll(v, LANES - ROPE_HALF, axis=1),
                                pltpu.roll(v, ROPE_HALF, axis=1))
            slabs.append(v * cos + partner * sin)
        emit(slabs, out_ref, which, g)

    def plain_piece(dst_ref, col):
        def store(u):
            dst_ref[:, col:col + u.shape[1]] = u
        return store

    pieces = [("qk", which, g) for which in range(2) for g in range(A_N_GROUPS)]
    pieces += [("v", g) for g in range(A_N_GROUPS)]
    pieces += [("plain", plain_piece(z_ref, i * BRANCH_WIDTH)) for i in range(N_BRANCH)]
    pieces += [("plain", plain_piece(bqkv_ref, i * BRANCH_WIDTH)) for i in range(3)]
    pieces += [("plain", plain_piece(cxbc_ref, i * BRANCH_WIDTH)) for i in range(3)]
    pieces += [("plain", plain_piece(dqkv_ref, i * BRANCH_WIDTH)) for i in range(2)]
    assert len(pieces) * BRANCH_WIDTH == W_SMALL

    def finish(chunk):
        u, kinds = chunk
        sq = {i: _dot(_bf(u[:, i * BRANCH_WIDTH:(i + 1) * BRANCH_WIDTH] * u[:, i * BRANCH_WIDTH:(i + 1) * BRANCH_WIDTH]),
                      ones_bd) for i, kind in enumerate(kinds) if kind[0] == "qk"}
        for i, kind in enumerate(kinds):
            ui = u[:, i * BRANCH_WIDTH:(i + 1) * BRANCH_WIDTH]
            if kind[0] == "qk":
                qk_piece(ui, sq[i], kind[1], kind[2])
            elif kind[0] == "v":
                emit([ui[:, s * LANES:(s + 1) * LANES] for s in range(SLABS)],
                     (v0_ref, v1_ref, v2_ref)[kind[1]], 2, kind[1])
            else:
                kind[1](ui)

    pending = None
    for c0 in range(0, len(pieces), 2):
        kinds = pieces[c0:c0 + 2]
        width = len(kinds) * BRANCH_WIDTH + (SMALL_W if c0 + 2 >= len(pieces) else 0)
        u = proj(c0 * BRANCH_WIDTH, width)
        if c0 + 2 >= len(pieces):
            kinds = kinds + [("plain", plain_piece(small_ref, 0))]
        if pending is not None:
            finish(pending)
        pending = (u, kinds)
    finish(pending)


def _inproj(xf, pos, norm_g, w_all, qn, kn, invf, sgn, ones_bd, bsz, seq, tm):
    t = xf.shape[0]
    tps = seq // tm
    row = lambda w: pl.BlockSpec((tm, w), lambda i: (i, 0))
    dils = [d for _, d in A_GROUPS]
    res_spec = lambda d: pl.BlockSpec((1, d, tm // d, BRANCH_WIDTH), lambda i: (i // tps, 0, i % tps, 0))
    res_shape = lambda d: jax.ShapeDtypeStruct((bsz, d, seq // d, BRANCH_WIDTH), BF16)
    flat = [(D_MODEL, F32), (3 * BRANCH_WIDTH, F32), (3 * BRANCH_WIDTH, F32), (2 * BRANCH_WIDTH, F32),
            (SMALL_W, F32)]
    return pl.pallas_call(
        functools.partial(_inproj_kernel, tm=tm),
        grid=(t // tm,),
        in_specs=[row(D_MODEL), pl.BlockSpec((1, tm // LANES, LANES), lambda i: (i, 0, 0)),
                  _full_spec((1, D_MODEL)), _const_spec((D_MODEL, W_TOTAL)),
                  _full_spec((1, A_QKV)), _full_spec((1, A_QKV)), _full_spec((1, LANES)),
                  _full_spec((1, LANES)), _full_spec((BRANCH_WIDTH, BRANCH_WIDTH))],
        out_specs=[res_spec(d) for d in dils] * 3 + [row(w) for w, _ in flat],
        out_shape=[res_shape(d) for d in dils] * 3 + [jax.ShapeDtypeStruct((t, w), dt) for w, dt in flat],
        scratch_shapes=[pltpu.VMEM((3, A_N_GROUPS - 1, SLABS, tm, LANES), F32)],
        compiler_params=pltpu.CompilerParams(dimension_semantics=("arbitrary",),
                                             vmem_limit_bytes=VMEM_LIMIT),
        name="inproj",
    )(xf, pos, norm_g, w_all, qn, kn, invf, sgn, ones_bd)


def _attn_kernel(q_ref, k_ref, v_ref, bias_ref, o_ref, lse_ref, *, n, rb, dil):
    nb = n // A_BLOCK
    head_lane = lax.broadcasted_iota(jnp.int32, (1, BRANCH_WIDTH), 1) // HEAD_DIM
    shift = nb.bit_length() - 1
    assert nb == 1 << shift and (rb * nb) % A_UNROLL == 0

    def blocks(it, carry):
        units = [it * A_UNROLL + b for b in range(A_UNROLL)]
        rr = [u >> shift for u in units]
        js = [u & (nb - 1) for u in units]
        q0 = [pl.multiple_of(j * A_BLOCK, A_BLOCK) for j in js]
        k0 = [pl.multiple_of(jnp.maximum(j - 1, 0) * A_BLOCK, A_BLOCK) for j in js]
        s = []
        for b in range(A_UNROLL):
            q = q_ref[0, rr[b], pl.ds(q0[b], A_BLOCK), :]
            zero = jnp.zeros_like(q)
            qs = jnp.concatenate([jnp.where(head_lane == h, q, zero) for h in range(N_HEADS)], axis=0)
            s.append(_dot_nt(qs, k_ref[0, rr[b], pl.ds(k0[b], 2 * A_BLOCK), :])
                     + bias_ref[jnp.minimum(js[b], 1)])
        m = [jnp.max(x, axis=-1, keepdims=True) for x in s]
        p = [jnp.exp(x - mx) for x, mx in zip(s, m)]
        l = [jnp.sum(x, axis=-1, keepdims=True) for x in p]
        pv = [_dot(_bf(p[b]), v_ref[0, rr[b], pl.ds(k0[b], 2 * A_BLOCK), :]) * (1.0 / l[b])
              for b in range(A_UNROLL)]
        for b in range(A_UNROLL):
            lse = m[b] + jnp.log(l[b])
            o_acc = pv[b][:A_BLOCK]
            lse_acc = jnp.broadcast_to(lse[:A_BLOCK], (A_BLOCK, BRANCH_WIDTH))
            for h in range(1, N_HEADS):
                o_acc = jnp.where(head_lane == h, pv[b][h * A_BLOCK:(h + 1) * A_BLOCK], o_acc)
                lse_acc = jnp.where(head_lane == h, lse[h * A_BLOCK:(h + 1) * A_BLOCK], lse_acc)
            if dil == 1:
                rows = pl.ds(q0[b], A_BLOCK)
            else:
                rows = pl.ds(q0[b] * dil + pl.program_id(1) * rb + rr[b], A_BLOCK, stride=dil)
            for s in range(SLABS):
                o_ref[0, s, rows, :] = o_acc[:, s * LANES:(s + 1) * LANES]
                lse_ref[0, s, rows, :] = lse_acc[:, s * LANES:(s + 1) * LANES]
        return carry

    lax.fori_loop(0, rb * nb // A_UNROLL, blocks, 0)


def _attn_group(q, k, v, gi):
    window, dil = A_GROUPS[gi]
    bsz, _, n, _ = q.shape
    assert n % (2 * A_BLOCK) == 0
    rb = min(dil, max(1, A_STEP_ROWS // n))
    win = window // dil
    qi = (np.arange(N_HEADS * A_BLOCK) % A_BLOCK)[:, None]
    kj = np.arange(2 * A_BLOCK)[None, :]
    band = lambda off: np.where((qi - kj + off >= 0) & (qi - kj + off <= win), 0.0, -np.inf).astype(np.float32)
    bias = jnp.asarray(np.stack([band(0), band(A_BLOCK)]))
    blk = pl.BlockSpec((1, rb, n, BRANCH_WIDTH), lambda b, r: (b, r, 0, 0))
    seq = n * dil
    out_blk = pl.BlockSpec((1, SLABS, seq, LANES), lambda b, r: (b, 0, 0, 0))
    out_shape = jax.ShapeDtypeStruct((bsz, SLABS, seq, LANES), F32)
    return pl.pallas_call(
        functools.partial(_attn_kernel, n=n, rb=rb, dil=dil),
        grid=(bsz, dil // rb),
        in_specs=[blk] * 3 + [_full_spec(bias.shape)],
        out_specs=[out_blk, out_blk],
        out_shape=[out_shape, out_shape],
        compiler_params=pltpu.CompilerParams(dimension_semantics=("arbitrary", "arbitrary"),
                                             vmem_limit_bytes=VMEM_LIMIT),
        name=f"attn_d{dil}",
    )(q, k, v, bias)


def _attn_merge_groups(o_refs, lse_refs, z):
    slabs = []
    for s in range(SLABS):
        lses = [ref[0, s] for ref in lse_refs]
        m = jnp.maximum(jnp.maximum(lses[0], lses[1]), lses[2])
        es = [jnp.exp(l - m) for l in lses]
        acc = o_refs[0][0, s] * es[0] + o_refs[1][0, s] * es[1] + o_refs[2][0, s] * es[2]
        acc = acc / (es[0] + es[1] + es[2])
        slabs.append(_bf(acc * _silu(z[:, s * LANES:(s + 1) * LANES])))
    return jnp.concatenate(slabs, axis=1)


def _gdn_kernel(qkv_ref, z_ref, small_ref, cw_ref, alog_ref, dtb_ref, og_ref,
                ones_ref, bd_ref, lv_ref, eye_ref, selb_ref,
                out_ref,
                tail_ref, st_ref, q_s, k_s, vb_s, kb_s, ce_s, qe_s, oi_s, corr_s, add_s, *, tb):
    nchunk = tb // B_CHUNK

    @pl.when(pl.program_id(1) == 0)
    def _():
        tail_ref[...] = jnp.zeros_like(tail_ref)
        st_ref[...] = jnp.zeros_like(st_ref)

    ones_bd = ones_ref[...]
    bd16 = ones_bd
    bdmask = bd_ref[...]
    lv = lv_ref[...]
    eye_cat = eye_ref[...]

    def prologue(r, carry):
        rows = pl.ds(pl.multiple_of(r * PRO_ROWS, PRO_ROWS), PRO_ROWS)
        halo = pl.ds(pl.multiple_of(jnp.maximum(r * PRO_ROWS - SUBLANES, 0), SUBLANES), SUBLANES)
        tail = jnp.where(r == 0, tail_ref[...], qkv_ref[0, halo, :])
        sm = small_ref[0, rows, :]
        g = -jnp.exp(alog_ref[...]) * _softplus(sm + dtb_ref[...])
        ce_s[rows, :] = _expand_heads(_seg_cumsum(g, B_CHUNK), SM_BDEC)
        beta = _expand_heads_mxu(jax.nn.sigmoid(sm), selb_ref[...])
        for part in range(3):
            cols = slice(part * BRANCH_WIDTH, (part + 1) * BRANCH_WIDTH)
            y = _silu(_causal_conv(qkv_ref[0, rows, cols], tail[:, cols], cw_ref[:, cols]))
            if part == 0:
                q_s[rows, :] = y * lax.rsqrt(_seg_sum(y * y, ones_bd) + EPS) * (HEAD_DIM ** -0.5)
            elif part == 1:
                kn = y * lax.rsqrt(_seg_sum(y * y, ones_bd) + EPS)
                k_s[rows, :] = kn
                kb_s[rows, :] = kn * beta
            else:
                vb_s[rows, :] = y * beta
        return carry

    lax.fori_loop(0, tb // PRO_ROWS, prologue, 0)
    tail_ref[...] = qkv_ref[0, tb - SUBLANES:, :]

    il = lax.broadcasted_iota(jnp.int32, (B_CHUNK, BRANCH_WIDTH), 0)
    jl = lax.broadcasted_iota(jnp.int32, (B_CHUNK, BRANCH_WIDTH), 1) & (HEAD_DIM - 1)
    causal = il >= jl

    def intra_group(i, carry):
        ks = range(B_UNROLL)
        rows = [pl.ds(pl.multiple_of((i * B_UNROLL + k) * B_CHUNK, B_CHUNK), B_CHUNK) for k in ks]
        cc = [ce_s[r, :] for r in rows]
        kc = [k_s[r, :] for r in rows]
        qc = [q_s[r, :] for r in rows]
        kb = [kb_s[r, :] for r in rows]
        gamma = [jnp.exp(jnp.where(causal, c_ - jnp.sum(c_ * eye_cat, axis=0, keepdims=True), -jnp.inf))
                 for c_ in cc]
        prod = [_dot_nt(_bf(jnp.concatenate([kb[k], qc[k]], axis=0)), _stack(kc[k], bd16)) for k in ks]
        a_kk = [jnp.where(il > jl, prod[k][:B_CHUNK] * gamma[k], 0.0) for k in ks]
        a_qk = [_bf(prod[k][B_CHUNK:] * gamma[k]) for k in ks]
        t = [eye_cat - jnp.where(lv == 0, a, 0.0) for a in a_kk]
        for lb in range(1, int(math.log2(B_CHUNK))):
            x = [_dot(_bf(t[k]), _stack(jnp.where(lv == lb, a_kk[k], 0.0), bd16)) for k in ks]
            t = [t[k] - _dot(_bf(x[k]), _stack(t[k], bd16)) for k in ks]
        ecum = [jnp.exp(c_) for c_ in cc]
        uw = [_dot(_bf(t[k]), jnp.concatenate([_stack(vb_s[rows[k], :], bd16),
                                               _stack(kb[k] * ecum[k], bd16)], axis=1)) for k in ks]
        u = [a[:, :BRANCH_WIDTH] for a in uw]
        w = [a[:, BRANCH_WIDTH:] for a in uw]
        both = [_dot(a_qk[k], jnp.concatenate([_stack(u[k], bd16), _stack(w[k], bd16)], axis=1)) for k in ks]
        k_dec = [_bf(kc[k] * jnp.exp(cc[k][B_CHUNK - 1:B_CHUNK, :] - cc[k])) for k in ks]
        wu = [_dot_tn(_bf(jnp.concatenate([w[k], u[k]], axis=1)), k_dec[k]) for k in ks]
        for k in ks:
            oi_s[rows[k], :] = both[k][:, :BRANCH_WIDTH]
            qe_s[rows[k], :] = qc[k] * ecum[k] - both[k][:, BRANCH_WIDTH:]
            corr_s[i * B_UNROLL + k] = _bf(-(wu[k][:BRANCH_WIDTH] * bdmask))
            add_s[i * B_UNROLL + k] = wu[k][BRANCH_WIDTH:] * bdmask
        return carry

    lax.fori_loop(0, nchunk // B_UNROLL, intra_group, 0)

    og = og_ref[...]
    st = st_ref[...]
    for c in range(nchunk):
        rows = slice(c * B_CHUNK, (c + 1) * B_CHUNK)
        st16 = _bf(st)
        clast = ce_s[(c + 1) * B_CHUNK - 1:(c + 1) * B_CHUNK, :]
        st = jnp.exp(clast) * st + _dot(st16, corr_s[c]) + add_s[c]
        o = _dot_nt(_bf(qe_s[rows, :]), st16) + oi_s[rows, :]
        ms = _seg_sum(o * o, ones_bd) * (1.0 / HEAD_DIM)
        on = o * lax.rsqrt(ms + EPS) * og
        out_ref[0, rows, :] = _bf(on * _silu(z_ref[0, rows, :]))
    st_ref[...] = st


def _gdn(bqkv, z_all, small, conv_w, alog_row, dtb_row, og_row, bsz, seq, tb):
    c = _NPC
    nchunk = tb // B_CHUNK
    assert nchunk % B_UNROLL == 0
    seq_blk = lambda w, col: pl.BlockSpec((1, tb, w), lambda b, j: (b, j, col))
    consts = [jnp.asarray(c["bd"], BF16), jnp.asarray(c["bd"]), jnp.asarray(c["lv"]), jnp.asarray(c["eye_cat"]),
              _sel_matrix(SM_BETA)]
    act = pltpu.VMEM((tb, BRANCH_WIDTH), F32)
    sq = (nchunk, BRANCH_WIDTH, BRANCH_WIDTH)
    return pl.pallas_call(
        functools.partial(_gdn_kernel, tb=tb),
        grid=(bsz, seq // tb),
        in_specs=[seq_blk(3 * BRANCH_WIDTH, 0), seq_blk(BRANCH_WIDTH, 1), seq_blk(SMALL_W, 0),
                  _full_spec(conv_w.shape), _full_spec((1, SMALL_W)), _full_spec((1, SMALL_W)),
                  _full_spec((1, BRANCH_WIDTH))] + [_full_spec(a.shape) for a in consts],
        out_specs=seq_blk(BRANCH_WIDTH, 0),
        out_shape=jax.ShapeDtypeStruct((bsz, seq, BRANCH_WIDTH), BF16),
        scratch_shapes=[pltpu.VMEM((SUBLANES, 3 * BRANCH_WIDTH), F32),
                        pltpu.VMEM((BRANCH_WIDTH, BRANCH_WIDTH), F32)] + [act] * 7
                       + [pltpu.VMEM(sq, BF16), pltpu.VMEM(sq, F32)],
        compiler_params=pltpu.CompilerParams(dimension_semantics=("arbitrary", "arbitrary"),
                                             vmem_limit_bytes=VMEM_LIMIT),
        name="gdn",
    )(bqkv.reshape(bsz, seq, -1), z_all.reshape(bsz, seq, -1), small.reshape(bsz, seq, -1),
      conv_w, alog_row, dtb_row, og_row, *consts)


def _ssd_kernel(xbc_ref, z_ref, small_ref, cw_ref, cb_ref, alog_ref, dtb_ref, dskip_ref, og_ref,
                bd_ref, eye_ref, gm_ref, gmt_ref, seld_ref,
                out_ref,
                tail_ref, st_ref, xs_s, bm_s, cm_s, dt_s, ac_s, *, tb):
    nchunk = tb // C_CHUNK

    @pl.when(pl.program_id(1) == 0)
    def _():
        tail_ref[...] = jnp.zeros_like(tail_ref)
        st_ref[...] = jnp.zeros_like(st_ref)

    bd16 = bd_ref[...]
    eye_cat = eye_ref[...]
    gmask16 = gm_ref[...]
    gmask_t = gmt_ref[...]

    def prologue(r, carry):
        rows = pl.ds(pl.multiple_of(r * PRO_ROWS, PRO_ROWS), PRO_ROWS)
        halo = pl.ds(pl.multiple_of(jnp.maximum(r * PRO_ROWS - SUBLANES, 0), SUBLANES), SUBLANES)
        tail = jnp.where(r == 0, tail_ref[...], xbc_ref[0, halo, :])
        for part, dst in enumerate((xs_s, bm_s, cm_s)):
            cols = slice(part * BRANCH_WIDTH, (part + 1) * BRANCH_WIDTH)
            dst[rows, :] = _silu(_causal_conv(xbc_ref[0, rows, cols], tail[:, cols], cw_ref[:, cols])
                                 + cb_ref[:, cols])
        return carry

    lax.fori_loop(0, tb // PRO_ROWS, prologue, 0)
    tail_ref[...] = xbc_ref[0, tb - SUBLANES:, :]
    dt = _softplus(small_ref[0] + dtb_ref[...])
    a = -jnp.exp(alog_ref[...]) * dt
    dt_s[...] = _expand_heads_mxu(dt, seld_ref[...])
    ac_s[...] = _expand_heads(_seg_cumsum(a, C_CHUNK), SM_CDT)

    il = lax.broadcasted_iota(jnp.int32, (C_CHUNK, BRANCH_WIDTH), 0)
    jl = lax.broadcasted_iota(jnp.int32, (C_CHUNK, BRANCH_WIDTH), 1) & (HEAD_DIM - 1)
    causal = il >= jl
    og = og_ref[...]
    dskip = dskip_ref[...]
    half = BRANCH_WIDTH // C_GROUPS

    def chunk_group(i, carry):
        ks = range(C_UNROLL)
        rows = [pl.ds(pl.multiple_of((i * C_UNROLL + k) * C_CHUNK, C_CHUNK), C_CHUNK) for k in ks]
        xc = [xs_s[r, :] for r in rows]
        ac = [ac_s[r, :] for r in rows]
        cc16 = [_bf(cm_s[r, :]) for r in rows]
        bc16 = [_bf(bm_s[r, :]) for r in rows]
        xdt = [xc[k] * dt_s[rows[k], :] for k in ks]
        seg = [jnp.exp(jnp.where(causal, a - jnp.sum(a * eye_cat, axis=0, keepdims=True), -jnp.inf)) for a in ac]
        scores = [_dot_nt(cc16[k], jnp.concatenate([bc16[k]] * N_HEADS, axis=0) * gmask16) * seg[k] for k in ks]
        y_diag = [_dot(_bf(scores[k]), _stack(xdt[k], bd16)) for k in ks]
        alast = [a[C_CHUNK - 1:C_CHUNK, :] for a in ac]
        delta = [gmask_t * _dot_tn(bc16[k], _bf(xdt[k] * jnp.exp(alast[k] - ac[k]))) for k in ks]
        st = st_ref[...]
        sts = []
        for k in ks:
            sts.append(_bf(st))
            st = jnp.exp(alast[k]) * st + delta[k]
        st_ref[...] = st
        y_off = [_dot(cc16[k], sts[k]) * jnp.exp(ac[k]) for k in ks]
        for k in ks:
            yv = (y_diag[k] + y_off[k] + dskip * xc[k]) * _silu(z_ref[0, rows[k], :])
            for gi in range(C_GROUPS):
                yg = yv[:, gi * half:(gi + 1) * half]
                ms = jnp.mean(yg * yg, axis=-1, keepdims=True)
                out_ref[0, rows[k], gi * half:(gi + 1) * half] = _bf(
                    yg * lax.rsqrt(ms + EPS) * og[:, gi * half:(gi + 1) * half])
        return carry

    lax.fori_loop(0, nchunk // C_UNROLL, chunk_group, 0)


def _ssd(cxbc, z_all, small, conv_w, conv_b, alog_row, dtb_row, dskip_row, og_row, bsz, seq, tb):
    c = _NPC
    assert (tb // C_CHUNK) % C_UNROLL == 0
    seq_blk = lambda w, col: pl.BlockSpec((1, tb, w), lambda b, j: (b, j, col))
    consts = [jnp.asarray(c["bd"], BF16), jnp.asarray(c["eye_cat"]), jnp.asarray(c["gmask"], BF16),
              jnp.asarray(c["gmask"].T.copy()), _sel_matrix(SM_CDT)]
    act = pltpu.VMEM((tb, BRANCH_WIDTH), F32)
    return pl.pallas_call(
        functools.partial(_ssd_kernel, tb=tb),
        grid=(bsz, seq // tb),
        in_specs=[seq_blk(3 * BRANCH_WIDTH, 0), seq_blk(BRANCH_WIDTH, 2), seq_blk(SMALL_W, 0),
                  _full_spec(conv_w.shape), _full_spec((1, 3 * BRANCH_WIDTH)), _full_spec((1, SMALL_W)),
                  _full_spec((1, SMALL_W)), _full_spec((1, BRANCH_WIDTH)), _full_spec((1, BRANCH_WIDTH))]
                 + [_full_spec(a.shape) for a in consts],
        out_specs=seq_blk(BRANCH_WIDTH, 0),
        out_shape=jax.ShapeDtypeStruct((bsz, seq, BRANCH_WIDTH), BF16),
        scratch_shapes=[pltpu.VMEM((SUBLANES, 3 * BRANCH_WIDTH), F32),
                        pltpu.VMEM((BRANCH_WIDTH, BRANCH_WIDTH), F32)] + [act] * 5,
        compiler_params=pltpu.CompilerParams(dimension_semantics=("arbitrary", "arbitrary"),
                                             vmem_limit_bytes=VMEM_LIMIT),
        name="ssd",
    )(cxbc.reshape(bsz, seq, -1), z_all.reshape(bsz, seq, -1), small.reshape(bsz, seq, -1),
      conv_w, conv_b, alog_row, dtb_row, dskip_row, og_row, *consts)


def _gla_kernel(qkv_ref, z_ref, small_ref, w2_ref, bg_ref, og_ref, ones_ref, dm_ref, cm_ref,
                out_ref,
                st_ref, qg_s, kg_s, k_s, cum_s, *, tb):
    @pl.when(pl.program_id(1) == 0)
    def _():
        st_ref[...] = jnp.zeros_like(st_ref)

    qkv = qkv_ref[0]
    q = qkv[:, :D_KEY] * (D_DK ** -0.5)
    k = qkv[:, D_KEY:2 * D_KEY]
    sm = small_ref[0]
    s_hi = _bf(sm)
    s_lo = _bf(sm - s_hi.astype(F32))
    w2 = w2_ref[...]
    w_hi = _bf(w2)
    w_lo = _bf(w2 - w_hi.astype(F32))
    gl = _dot(s_hi, w_hi) + _dot(s_lo, w_hi) + _dot(s_hi, w_lo) + bg_ref[...]
    glog = jax.nn.log_sigmoid(gl) * (1.0 / D_GATE_NORM)
    cum = _seg_cumsum(glog, D_CHUNK)
    qg_s[...] = q * jnp.exp(cum)
    kg_s[...] = k * jnp.exp(-cum)
    k_s[...] = k
    cum_s[...] = cum

    cmask = cm_ref[...] > 0.0
    klane = lax.broadcasted_iota(jnp.int32, (1, D_KEY), 1) // D_DK
    vlane = lax.broadcasted_iota(jnp.int32, (1, BRANCH_WIDTH), 1) // D_DV
    dmask = dm_ref[...]
    ones_bd = ones_ref[...]
    og = og_ref[...]
    per_group = D_GROUP // D_CHUNK

    cmask4 = jnp.concatenate([cmask] * D_HEADS, axis=0)

    def groups(it, carry):
        gs = range(D_UNROLL)
        rows = [pl.ds(pl.multiple_of((it * D_UNROLL + g) * D_GROUP, D_GROUP), D_GROUP) for g in gs]
        qgg = [qg_s[r, :] for r in rows]
        cc = [cum_s[r, :] for r in rows]
        vg = [_bf(qkv_ref[0, r, 2 * D_KEY:]) for r in rows]
        a = []
        for g in gs:
            zero = jnp.zeros_like(qgg[g])
            qs = _bf(jnp.concatenate([jnp.where(klane == h, qgg[g], zero) for h in range(D_HEADS)], axis=0))
            a.append(_dot_nt(qs, _bf(kg_s[rows[g], :])))
        av = [_dot(_bf(jnp.where(cmask4, a[g], 0.0)), vg[g]) for g in gs]
        chunks = [(g, c) for g in gs for c in range(per_group)]
        clast, delta = {}, {}
        for g, c in chunks:
            sl = slice(c * D_CHUNK, (c + 1) * D_CHUNK)
            ccc = cc[g][sl]
            clast[g, c] = ccc[D_CHUNK - 1:D_CHUNK, :]
            kd = k_s[rows[g], :][sl] * jnp.exp(clast[g, c] - ccc)
            delta[g, c] = dmask * _dot_tn(_bf(kd), vg[g][sl])
        dcol = {}
        for g in gs:
            dec_rows = jnp.concatenate([jnp.broadcast_to(jnp.exp(clast[g, c]), (D_CHUNK, D_KEY))
                                        for c in range(per_group)], axis=0)
            dec_t = dec_rows.T
            for c in range(per_group):
                dcol[g, c] = dec_t[:, c * D_CHUNK:c * D_CHUNK + 1]
        st = st_ref[...]
        sts = {}
        for g, c in chunks:
            sts[g, c] = _bf(st)
            st = dcol[g, c] * st + delta[g, c]
        st_ref[...] = st
        for g in gs:
            o_inter = jnp.concatenate([_dot(_bf(qgg[g][c * D_CHUNK:(c + 1) * D_CHUNK]), sts[g, c])
                                       for c in range(per_group)], axis=0)
            o_intra = av[g][:D_GROUP]
            for h in range(1, D_HEADS):
                o_intra = jnp.where(vlane == h, av[g][h * D_GROUP:(h + 1) * D_GROUP], o_intra)
            o = o_inter + o_intra
            ms = _seg_sum(o * o, ones_bd) * (1.0 / D_DV)
            on = o * lax.rsqrt(ms + EPS) * og
            out_ref[0, rows[g], :] = _bf(on * _silu(z_ref[0, rows[g], :]))
        return carry

    lax.fori_loop(0, tb // (D_GROUP * D_UNROLL), groups, 0)


def _gla(dqkv, z_all, small, w2pad, bg_row, og_row, bsz, seq, tb):
    c = _NPC
    seq_blk = lambda w, col: pl.BlockSpec((1, tb, w), lambda b, j: (b, j, col))
    consts = [jnp.asarray(c["bd"], BF16), jnp.asarray(c["dmask"]), jnp.asarray(c["cmask"])]
    return pl.pallas_call(
        functools.partial(_gla_kernel, tb=tb),
        grid=(bsz, seq // tb),
        in_specs=[seq_blk(2 * BRANCH_WIDTH, 0), seq_blk(BRANCH_WIDTH, 3), seq_blk(SMALL_W, 0),
                  _full_spec((SMALL_W, D_KEY)), _full_spec((1, D_KEY)), _full_spec((1, BRANCH_WIDTH))]
                 + [_full_spec(a.shape) for a in consts],
        out_specs=seq_blk(BRANCH_WIDTH, 0),
        out_shape=jax.ShapeDtypeStruct((bsz, seq, BRANCH_WIDTH), BF16),
        scratch_shapes=[pltpu.VMEM((D_KEY, BRANCH_WIDTH), F32)] + [pltpu.VMEM((tb, D_KEY), F32)] * 4,
        compiler_params=pltpu.CompilerParams(dimension_semantics=("arbitrary", "arbitrary"),
                                             vmem_limit_bytes=VMEM_LIMIT),
        name="gla",
    )(dqkv.reshape(bsz, seq, -1), z_all.reshape(bsz, seq, -1), small.reshape(bsz, seq, -1),
      w2pad, bg_row, og_row, *consts)


def _merge_kernel(x_ref, ng_ref, o0, o1, o2, l0, l1, l2, za_ref, br1, br2, br3,
                  wg_ref, bg_ref, wb_ref, wo_ref, out_ref):
    x = x_ref[...]
    ms = jnp.mean(x * x, axis=-1, keepdims=True)
    hb = _bf(x * lax.rsqrt(ms + EPS) * ng_ref[...])
    br_a = _attn_merge_groups((o0, o1, o2), (l0, l1, l2), za_ref[...])
    merged = jnp.zeros(x.shape, F32)
    for i, br in enumerate((br_a, br1[...], br2[...], br3[...])):
        gate = jax.nn.sigmoid(_dot(hb, wg_ref[i]) + bg_ref[i])
        merged = merged + gate * _dot(br, wb_ref[i])
    out_ref[...] = x + _dot(_bf(merged), wo_ref[...])


def _merge(xf, norm_g, attn_os, attn_lses, z_all, brs, w_gate, b_gate, w_branch, w_out, seq, tm):
    t = xf.shape[0]
    tps = seq // tm
    row = lambda w: pl.BlockSpec((tm, w), lambda i: (i, 0))
    slab = pl.BlockSpec((1, SLABS, tm, LANES), lambda i: (i // tps, 0, i % tps, 0))
    return pl.pallas_call(
        _merge_kernel,
        grid=(t // tm,),
        in_specs=[row(D_MODEL), _full_spec((1, D_MODEL))] + [slab] * (2 * A_N_GROUPS) + [row(BRANCH_WIDTH)]
                 + [row(BRANCH_WIDTH)] * (N_BRANCH - 1)
                 + [_const_spec(w_gate.shape), _full_spec(b_gate.shape), _const_spec(w_branch.shape),
                    _const_spec(w_out.shape)],
        out_specs=row(D_MODEL),
        out_shape=jax.ShapeDtypeStruct((t, D_MODEL), F32),
        compiler_params=pltpu.CompilerParams(dimension_semantics=("arbitrary",),
                                             vmem_limit_bytes=VMEM_LIMIT),
        name="merge",
    )(xf, norm_g, *attn_os, *attn_lses, z_all, *brs, w_gate, b_gate, w_branch, w_out)


def _lane_row(vals, offset, width):
    return jnp.zeros((1, width), F32).at[0, offset:offset + vals.shape[0]].set(vals.astype(F32))


def _rope_rows():
    lane = np.arange(LANES) % HEAD_DIM
    inv_freq = ROPE_THETA ** (-jnp.arange(ROPE_HALF, dtype=F32) / ROPE_HALF)
    invf = jnp.where(lane < ROPE_DIM, inv_freq[lane % ROPE_HALF], 0.0).astype(F32)[None, :]
    sgn = np.where(lane < ROPE_HALF, -1.0, np.where(lane < ROPE_DIM, 1.0, 0.0)).astype(np.float32)[None, :]
    return invf, jnp.asarray(sgn)


def _w_in_layout():
    sizes = (A_QKV, A_QKV, A_QKV, BRANCH_WIDTH, 3 * BRANCH_WIDTH, BRANCH_WIDTH, N_HEADS, N_HEADS,
             3 * BRANCH_WIDTH, BRANCH_WIDTH, N_HEADS, D_KEY, D_KEY, BRANCH_WIDTH, BRANCH_WIDTH, D_GATE_RANK)
    names = ("a_q", "a_k", "a_v", "a_z", "b_qkv", "b_z", "b_beta", "b_a", "c_xbc", "c_z", "c_dt",
             "d_q", "d_k", "d_v", "d_z", "d_g")
    src = dict(zip(names, np.concatenate([[0], np.cumsum(sizes)[:-1]]).tolist()))
    total = int(sum(sizes))
    moves = [(W_AQ, src["a_q"], 3 * A_QKV + BRANCH_WIDTH),
             (W_Z + BRANCH_WIDTH, src["b_z"], BRANCH_WIDTH),
             (W_Z + 2 * BRANCH_WIDTH, src["c_z"], BRANCH_WIDTH),
             (W_Z + 3 * BRANCH_WIDTH, src["d_z"], BRANCH_WIDTH),
             (W_BQKV, src["b_qkv"], 3 * BRANCH_WIDTH),
             (W_CXBC, src["c_xbc"], 3 * BRANCH_WIDTH),
             (W_DQKV, src["d_q"], 2 * BRANCH_WIDTH)]
    small = []
    for lane0, name, width in ((SM_BETA, "b_beta", 2 * N_HEADS), (SM_CDT, "c_dt", N_HEADS),
                               (SM_DG, "d_g", D_GATE_RANK)):
        win = min(src[name] - lane0, total - SMALL_W)
        small.append((lane0 + width, win, (lane0 - (src[name] - win)) % SMALL_W))
    return total, moves, small


def _pack_kernel(w_ref, out_ref):
    _, moves, small = _w_in_layout()
    for dst, src, width in moves:
        out_ref[:, dst:dst + width] = _bf(w_ref[:, src:src + width])
    lane = lax.broadcasted_iota(jnp.int32, (1, SMALL_W), 1)
    blk = jnp.zeros((w_ref.shape[0], SMALL_W), F32)
    for lane_end, win, shift in reversed(small):
        piece = w_ref[:, win:win + SMALL_W]
        if shift:
            piece = pltpu.roll(piece, shift, axis=1)
        blk = jnp.where(lane < lane_end, piece, blk)
    out_ref[:, W_SMALL:W_SMALL + SMALL_W] = _bf(blk)


def _pack_w_in(w_in, layer):
    total, _, _ = _w_in_layout()
    assert w_in.shape[1:] == (D_MODEL, total)
    rows = D_MODEL // 4
    return pl.pallas_call(
        _pack_kernel,
        grid=(D_MODEL // rows,),
        in_specs=[pl.BlockSpec((None, rows, total), lambda i: (layer, i, 0))],
        out_specs=pl.BlockSpec((rows, W_TOTAL), lambda i: (i, 0)),
        out_shape=jax.ShapeDtypeStruct((D_MODEL, W_TOTAL), BF16),
        compiler_params=pltpu.CompilerParams(dimension_semantics=("arbitrary",)),
        name="pack_w_in",
    )(w_in)


def _layer(xf, pos, bsz, seq, p, tm, tb):
    invf, sgn = _rope_rows()
    ones_bd = jnp.asarray(_NPC["bd"], BF16)
    w_all = _pack_w_in(p["w_in_stacked"], p["layer"])
    ng = p["norm_g"][None, :]
    qn = jnp.tile(p["a_q_norm"], A_QKV // HEAD_DIM)[None, :]
    kn = jnp.tile(p["a_k_norm"], A_QKV // HEAD_DIM)[None, :]
    outs = _inproj(xf, pos, ng, w_all, qn, kn, invf, sgn, ones_bd, bsz, seq, tm)
    qs, ks, vs = outs[0:3], outs[3:6], outs[6:9]
    z_all, bqkv, cxbc, dqkv, small = outs[9:]

    os_, lses = [], []
    for gi in range(A_N_GROUPS):
        o, lse = _attn_group(qs[gi], ks[gi], vs[gi], gi)
        os_.append(o)
        lses.append(lse)

    br_b = _gdn(bqkv, z_all, small, p["b_conv_w"],
                _lane_row(p["b_a_log"], SM_BDEC, SMALL_W), _lane_row(p["b_dt_bias"], SM_BDEC, SMALL_W),
                jnp.tile(p["b_out_norm"], N_HEADS)[None, :], bsz, seq, tb).reshape(bsz * seq, BRANCH_WIDTH)

    br_c = _ssd(cxbc, z_all, small, p["c_conv_w"], p["c_conv_b"][None, :],
                _lane_row(p["c_a_log"], SM_CDT, SMALL_W), _lane_row(p["c_dt_bias"], SM_CDT, SMALL_W),
                jnp.repeat(p["c_d_skip"], HEAD_DIM)[None, :], p["c_out_norm"][None, :],
                bsz, seq, tb).reshape(bsz * seq, BRANCH_WIDTH)

    w2pad = jnp.zeros((SMALL_W, D_KEY), F32).at[SM_DG:SM_DG + D_GATE_RANK, :].set(p["d_gate_w2"])
    br_d = _gla(dqkv, z_all, small, w2pad, p["d_gate_b"][None, :],
                jnp.tile(p["d_out_norm"], D_HEADS)[None, :], bsz, seq, tb).reshape(bsz * seq, BRANCH_WIDTH)

    return _merge(xf, ng, os_, lses, z_all, (br_b, br_c, br_d), _bf(p["w_gate"]), p["b_gate"][:, None, :],
                  _bf(p["w_branch"]), _bf(p["w_out"]), seq, tm)


def _pick_tile(n, target):
    t = min(n, target)
    while n % t:
        t //= 2
    return t


def kernel(x, positions, norm_g, w_in, a_q_norm, a_k_norm, b_conv_w, b_a_log, b_dt_bias, b_out_norm,
           c_conv_w, c_conv_b, c_a_log, c_dt_bias, c_d_skip, c_out_norm, d_gate_w2, d_gate_b, d_out_norm,
           w_gate, b_gate, w_branch, w_out):
    bsz, seq, d_model = x.shape
    assert d_model == D_MODEL and seq % (2 * A_BLOCK * A_GROUPS[-1][1]) == 0
    xf = x.reshape(bsz * seq, d_model)
    tm = _pick_tile(seq, DENSE_TILE)
    assert tm % LANES == 0
    pos = positions.reshape(bsz * seq // tm, tm // LANES, LANES).astype(jnp.int32)
    tb = _pick_tile(seq, MIXER_TILE)
    names = ("norm_g", "w_in", "a_q_norm", "a_k_norm", "b_conv_w", "b_a_log", "b_dt_bias", "b_out_norm",
             "c_conv_w", "c_conv_b", "c_a_log", "c_dt_bias", "c_d_skip", "c_out_norm", "d_gate_w2",
             "d_gate_b", "d_out_norm", "w_gate", "b_gate", "w_branch", "w_out")
    stacked = (norm_g, w_in, a_q_norm, a_k_norm, b_conv_w, b_a_log, b_dt_bias, b_out_norm,
               c_conv_w, c_conv_b, c_a_log, c_dt_bias, c_d_skip, c_out_norm, d_gate_w2, d_gate_b,
               d_out_norm, w_gate, b_gate, w_branch, w_out)
    for layer in range(norm_g.shape[0]):
        p = {n: a[layer] for n, a in zip(names, stacked) if n != "w_in"}
        p["w_in_stacked"], p["layer"] = w_in, layer
        xf = _layer(xf, pos, bsz, seq, p, tm, tb)
    return xf.reshape(bsz, seq, d_model)
```

```python
import functools
import math

import numpy as np
import jax
import jax.numpy as jnp
from jax import lax
from jax.experimental import pallas as pl
from jax.experimental.pallas import tpu as pltpu

F32 = jnp.float32
BF16 = jnp.bfloat16

D_MODEL = 1024
N_BRANCH = 4
BRANCH_WIDTH = D_MODEL // N_BRANCH
HEAD_DIM = 64
N_HEADS = BRANCH_WIDTH // HEAD_DIM
EPS = 1e-6
ROPE_DIM = HEAD_DIM // 4
ROPE_HALF = ROPE_DIM // 2
ROPE_THETA = 500000.0
CONV_K = 4
PRO_ROWS = 128
DENSE_TILE = 512
MIXER_TILE = 1024
LIGHT_MIXER_TILE = 2048

A_GROUPS = ((128, 1), (512, 4), (2048, 16))
A_N_GROUPS = len(A_GROUPS)
A_QKV = A_N_GROUPS * BRANCH_WIDTH
A_BLOCK = 128
A_STEP_ROWS = 4096
A_UNROLL = 4

B_CHUNK = 64
B_UNROLL = 8
C_GROUPS = 2
C_STATE = 128
C_CHUNK = 64
C_UNROLL = 4
D_HEADS = 4
D_KEY = BRANCH_WIDTH // 2
D_DK = D_KEY // D_HEADS
D_DV = BRANCH_WIDTH // D_HEADS
D_GATE_RANK = 16
D_GATE_NORM = 16.0
D_CHUNK = 32
D_GROUP = 128
D_UNROLL = 2

LANES = 128
SUBLANES = 8
SLABS = BRANCH_WIDTH // LANES
SMALL_W = LANES
SM_BETA, SM_BDEC, SM_CDT, SM_DG = 0, 4, 8, 12

W_AQ, W_AK, W_AV = 0, A_QKV, 2 * A_QKV
W_Z = 3 * A_QKV
W_BQKV = W_Z + D_MODEL
W_CXBC = W_BQKV + 3 * BRANCH_WIDTH
W_DQKV = W_CXBC + 3 * BRANCH_WIDTH
W_SMALL = W_DQKV + 2 * BRANCH_WIDTH
W_TOTAL = W_SMALL + SMALL_W

VMEM_LIMIT = 56 * 1024 * 1024


def _bf(x):
    return x.astype(BF16)


def _dot(a, b):
    return jnp.dot(a, b, preferred_element_type=F32)


def _dot_nt(a, b):
    return lax.dot_general(a, b, (((1,), (1,)), ((), ())), preferred_element_type=F32)


def _dot_tn(a, b):
    return lax.dot_general(a, b, (((0,), (0,)), ((), ())), preferred_element_type=F32)


def _expand_heads(x, offset):
    rows = x.shape[0]
    head_lane = lax.broadcasted_iota(jnp.int32, (1, BRANCH_WIDTH), 1) // HEAD_DIM
    out = jnp.broadcast_to(x[:, offset:offset + 1], (rows, BRANCH_WIDTH))
    for h in range(1, N_HEADS):
        out = jnp.where(head_lane == h, jnp.broadcast_to(x[:, offset + h:offset + h + 1], (rows, BRANCH_WIDTH)), out)
    return out


def _expand_heads_mxu(x, sel):
    hi = _bf(x)
    lo = _bf(x - hi.astype(F32))
    return _dot(hi, sel) + _dot(lo, sel)


def _sel_matrix(offset):
    m = np.zeros((SMALL_W, BRANCH_WIDTH), np.float32)
    for h in range(N_HEADS):
        m[offset + h, h * HEAD_DIM:(h + 1) * HEAD_DIM] = 1.0
    return jnp.asarray(m, BF16)


def _seg_sum(x, ones_bd):
    hi = _bf(x)
    lo = _bf(x - hi.astype(F32))
    return _dot(hi, ones_bd) + _dot(lo, ones_bd)


def _seg_cumsum(x, seg):
    pos = lax.broadcasted_iota(jnp.int32, x.shape, 0) & (seg - 1)
    step = 1
    while step < seg:
        x = x + jnp.where(pos >= step, pltpu.roll(x, step, axis=0), 0.0)
        step *= 2
    return x


def _silu(x):
    h = 0.5 * x
    return h + h * jnp.tanh(h)


def _softplus(x):
    return jnp.maximum(x, 0.0) + jnp.log1p(jnp.exp(-jnp.abs(x)))


def _shift_rows(x, tail, s):
    r = pltpu.roll(x, s, axis=0)
    t = pltpu.roll(tail, s, axis=0)
    row = lax.broadcasted_iota(jnp.int32, tail.shape, 0)
    first = jnp.where(row < s, t, r[:SUBLANES])
    return jnp.concatenate([first, r[SUBLANES:]], axis=0)


def _causal_conv(x, tail, w):
    y = x * w[CONV_K - 1:CONV_K, :]
    for s in range(1, CONV_K):
        y = y + _shift_rows(x, tail, s) * w[CONV_K - 1 - s:CONV_K - s, :]
    return y


def _stack(y, bdmask16):
    return jnp.concatenate([_bf(y)] * N_HEADS, axis=0) * bdmask16


def _np_consts():
    r = np.arange(BRANCH_WIDTH)
    bd = (r[:, None] // HEAD_DIM == r[None, :] // HEAD_DIM).astype(np.float32)
    i = np.arange(B_CHUNK)[:, None]
    j = (r % HEAD_DIM)[None, :]
    xor = np.maximum(i ^ j, 1)
    lv = np.where(i > j, np.floor(np.log2(xor)).astype(np.int32), -1).astype(np.int32)
    eye_cat = (i == j).astype(np.float32)
    gmask = ((r[None, :] // C_STATE) == (r[:, None] // HEAD_DIM) // (N_HEADS // C_GROUPS)).astype(np.float32)
    dmask = ((np.arange(D_KEY)[:, None] // D_DK) == (r[None, :] // D_DV)).astype(np.float32)
    t = np.arange(D_GROUP)
    cmask = ((t[:, None] // D_CHUNK == t[None, :] // D_CHUNK) & (t[:, None] >= t[None, :])).astype(np.float32)

    return dict(bd=bd, lv=lv, eye_cat=eye_cat, gmask=gmask, dmask=dmask, cmask=cmask)


_NPC = _np_consts()


def _full_spec(shape):
    nd = len(shape)
    return pl.BlockSpec(shape, lambda *_: (0,) * nd)


def _const_spec(shape):
    nd = len(shape)
    return pl.BlockSpec(shape, lambda *_: (0,) * nd, pipeline_mode=pl.Buffered(1))


def _inproj_kernel(x_ref, pos_ref, ng_ref, w_ref, qn_ref, kn_ref, invf_ref, sgn_ref, ones_ref,
                   q0_ref, q1_ref, q2_ref, k0_ref, k1_ref, k2_ref, v0_ref, v1_ref, v2_ref,
                   z_ref, bqkv_ref, cxbc_ref, dqkv_ref, small_ref, perm_s, *, tm):
    x = x_ref[...]
    ms = jnp.mean(x * x, axis=-1, keepdims=True)
    hb = _bf(x * lax.rsqrt(ms + EPS) * ng_ref[...])

    def proj(lo, width):
        return _dot(hb, w_ref[:, lo:lo + width])

    pos_f = pos_ref[0].astype(F32)
    pos_col = jnp.concatenate([jnp.broadcast_to(pos_f[k:k + 1, :], (LANES, LANES)).T
                               for k in range(tm // LANES)], axis=0)
    ang = pos_col * invf_ref[...]
    cos = jnp.cos(ang)
    sin = jnp.sin(ang) * sgn_ref[...]
    lane = lax.broadcasted_iota(jnp.int32, (1, LANES), 1) & (HEAD_DIM - 1)
    first_half = lane < ROPE_HALF
    ones_bd = ones_ref[...]

    def emit(slabs, out_ref, which, g):
        dil = A_GROUPS[g][1]
        for s, v in enumerate(slabs):
            if dil == 1:
                out_ref[0, 0, :, s * LANES:(s + 1) * LANES] = _bf(v)
            else:
                perm_s[which, g - 1, s] = v
        if dil > 1:
            rows = tm // dil
            for s in range(len(slabs)):
                for r in range(dil):
                    out_ref[0, r, :, s * LANES:(s + 1) * LANES] = _bf(
                        perm_s[which, g - 1, s, pl.ds(r, rows, stride=dil), :])

    def qk_piece(u, ss, which, g):
        gain_ref, out_ref, scale = ((qn_ref, (q0_ref, q1_ref, q2_ref)[g], HEAD_DIM ** -0.5) if which == 0
                                    else (kn_ref, (k0_ref, k1_ref, k2_ref)[g], 1.0))
        gain = gain_ref[:, g * BRANCH_WIDTH:(g + 1) * BRANCH_WIDTH]
        un = u * lax.rsqrt(ss * (1.0 / HEAD_DIM) + EPS) * (gain * scale)
        slabs = []
        for s in range(SLABS):
            v = un[:, s * LANES:(s + 1) * LANES]
            partner = jnp.where(first_half, pltpu.roll(v, LANES - ROPE_HALF, axis=1),
                                pltpu.roll(v, ROPE_HALF, axis=1))
            slabs.append(v * cos + partner * sin)
        emit(slabs, out_ref, which, g)

    def plain_piece(dst_ref, col):
        def store(u):
            dst_ref[:, col:col + u.shape[1]] = u
        return store

    pieces = [("qk", which, g) for which in range(2) for g in range(A_N_GROUPS)]
    pieces += [("v", g) for g in range(A_N_GROUPS)]
    pieces += [("plain", plain_piece(z_ref, i * BRANCH_WIDTH)) for i in range(N_BRANCH)]
    pieces += [("plain", plain_piece(bqkv_ref, i * BRANCH_WIDTH)) for i in range(3)]
    pieces += [("plain", plain_piece(cxbc_ref, i * BRANCH_WIDTH)) for i in range(3)]
    pieces += [("plain", plain_piece(dqkv_ref, i * BRANCH_WIDTH)) for i in range(2)]
    assert len(pieces) * BRANCH_WIDTH == W_SMALL

    def finish(chunk):
        u, kinds = chunk
        sq = {i: _dot(_bf(u[:, i * BRANCH_WIDTH:(i + 1) * BRANCH_WIDTH] * u[:, i * BRANCH_WIDTH:(i + 1) * BRANCH_WIDTH]),
                      ones_bd) for i, kind in enumerate(kinds) if kind[0] == "qk"}
        for i, kind in enumerate(kinds):
            ui = u[:, i * BRANCH_WIDTH:(i + 1) * BRANCH_WIDTH]
            if kind[0] == "qk":
                qk_piece(ui, sq[i], kind[1], kind[2])
            elif kind[0] == "v":
                emit([ui[:, s * LANES:(s + 1) * LANES] for s in range(SLABS)],
                     (v0_ref, v1_ref, v2_ref)[kind[1]], 2, kind[1])
            else:
                kind[1](ui)

    pending = None
    for c0 in range(0, len(pieces), 2):
        kinds = pieces[c0:c0 + 2]
        width = len(kinds) * BRANCH_WIDTH + (SMALL_W if c0 + 2 >= len(pieces) else 0)
        u = proj(c0 * BRANCH_WIDTH, width)
        if c0 + 2 >= len(pieces):
            kinds = kinds + [("plain", plain_piece(small_ref, 0))]
        if pending is not None:
            finish(pending)
        pending = (u, kinds)
    finish(pending)


def _inproj(xf, pos, norm_g, w_all, qn, kn, invf, sgn, ones_bd, bsz, seq, tm):
    t = xf.shape[0]
    tps = seq // tm
    row = lambda w: pl.BlockSpec((tm, w), lambda i: (i, 0))
    dils = [d for _, d in A_GROUPS]
    res_spec = lambda d: pl.BlockSpec((1, d, tm // d, BRANCH_WIDTH), lambda i: (i // tps, 0, i % tps, 0))
    res_shape = lambda d: jax.ShapeDtypeStruct((bsz, d, seq // d, BRANCH_WIDTH), BF16)
    flat = [(D_MODEL, F32), (3 * BRANCH_WIDTH, F32), (3 * BRANCH_WIDTH, F32), (2 * BRANCH_WIDTH, F32),
            (SMALL_W, F32)]
    return pl.pallas_call(
        functools.partial(_inproj_kernel, tm=tm),
        grid=(t // tm,),
        in_specs=[row(D_MODEL), pl.BlockSpec((1, tm // LANES, LANES), lambda i: (i, 0, 0)),
                  _full_spec((1, D_MODEL)), _const_spec((D_MODEL, W_TOTAL)),
                  _full_spec((1, A_QKV)), _full_spec((1, A_QKV)), _full_spec((1, LANES)),
                  _full_spec((1, LANES)), _full_spec((BRANCH_WIDTH, BRANCH_WIDTH))],
        out_specs=[res_spec(d) for d in dils] * 3 + [row(w) for w, _ in flat],
        out_shape=[res_shape(d) for d in dils] * 3 + [jax.ShapeDtypeStruct((t, w), dt) for w, dt in flat],
        scratch_shapes=[pltpu.VMEM((3, A_N_GROUPS - 1, SLABS, tm, LANES), F32)],
        compiler_params=pltpu.CompilerParams(dimension_semantics=("arbitrary",),
                                             vmem_limit_bytes=VMEM_LIMIT),
        name="inproj",
    )(xf, pos, norm_g, w_all, qn, kn, invf, sgn, ones_bd)


def _attn_kernel(q_ref, k_ref, v_ref, bias_ref, o_ref, lse_ref, *, n, rb, dil):
    nb = n // A_BLOCK
    head_lane = lax.broadcasted_iota(jnp.int32, (1, BRANCH_WIDTH), 1) // HEAD_DIM
    shift = nb.bit_length() - 1
    assert nb == 1 << shift and (rb * nb) % A_UNROLL == 0

    def blocks(it, carry):
        units = [it * A_UNROLL + b for b in range(A_UNROLL)]
        rr = [u >> shift for u in units]
        js = [u & (nb - 1) for u in units]
        q0 = [pl.multiple_of(j * A_BLOCK, A_BLOCK) for j in js]
        k0 = [pl.multiple_of(jnp.maximum(j - 1, 0) * A_BLOCK, A_BLOCK) for j in js]
        s = []
        for b in range(A_UNROLL):
            q = q_ref[0, rr[b], pl.ds(q0[b], A_BLOCK), :]
            zero = jnp.zeros_like(q)
            qs = jnp.concatenate([jnp.where(head_lane == h, q, zero) for h in range(N_HEADS)], axis=0)
            s.append(_dot_nt(qs, k_ref[0, rr[b], pl.ds(k0[b], 2 * A_BLOCK), :])
                     + bias_ref[jnp.minimum(js[b], 1)])
        m = [jnp.max(x, axis=-1, keepdims=True) for x in s]
        p = [jnp.exp(x - mx) for x, mx in zip(s, m)]
        l = [jnp.sum(x, axis=-1, keepdims=True) for x in p]
        pv = [_dot(_bf(p[b]), v_ref[0, rr[b], pl.ds(k0[b], 2 * A_BLOCK), :]) * (1.0 / l[b])
              for b in range(A_UNROLL)]
        for b in range(A_UNROLL):
            lse = m[b] + jnp.log(l[b])
            o_acc = pv[b][:A_BLOCK]
            lse_acc = jnp.broadcast_to(lse[:A_BLOCK], (A_BLOCK, BRANCH_WIDTH))
            for h in range(1, N_HEADS):
                o_acc = jnp.where(head_lane == h, pv[b][h * A_BLOCK:(h + 1) * A_BLOCK], o_acc)
                lse_acc = jnp.where(head_lane == h, lse[h * A_BLOCK:(h + 1) * A_BLOCK], lse_acc)
            if dil == 1:
                rows = pl.ds(q0[b], A_BLOCK)
            else:
                rows = pl.ds(q0[b] * dil + pl.program_id(1) * rb + rr[b], A_BLOCK, stride=dil)
            for s in range(SLABS):
                o_ref[0, s, rows, :] = o_acc[:, s * LANES:(s + 1) * LANES]
                lse_ref[0, s, rows, :] = lse_acc[:, s * LANES:(s + 1) * LANES]
        return carry

    lax.fori_loop(0, rb * nb // A_UNROLL, blocks, 0)


def _attn_group(q, k, v, gi):
    window, dil = A_GROUPS[gi]
    bsz, _, n, _ = q.shape
    assert n % (2 * A_BLOCK) == 0
    rb = min(dil, max(1, A_STEP_ROWS // n))
    win = window // dil
    qi = (np.arange(N_HEADS * A_BLOCK) % A_BLOCK)[:, None]
    kj = np.arange(2 * A_BLOCK)[None, :]
    band = lambda off: np.where((qi - kj + off >= 0) & (qi - kj + off <= win), 0.0, -np.inf).astype(np.float32)
    bias = jnp.asarray(np.stack([band(0), band(A_BLOCK)]))
    blk = pl.BlockSpec((1, rb, n, BRANCH_WIDTH), lambda b, r: (b, r, 0, 0))
    seq = n * dil
    out_blk = pl.BlockSpec((1, SLABS, seq, LANES), lambda b, r: (b, 0, 0, 0))
    out_shape = jax.ShapeDtypeStruct((bsz, SLABS, seq, LANES), F32)
    return pl.pallas_call(
        functools.partial(_attn_kernel, n=n, rb=rb, dil=dil),
        grid=(bsz, dil // rb),
        in_specs=[blk] * 3 + [_full_spec(bias.shape)],
        out_specs=[out_blk, out_blk],
        out_shape=[out_shape, out_shape],
        compiler_params=pltpu.CompilerParams(dimension_semantics=("arbitrary", "arbitrary"),
                                             vmem_limit_bytes=VMEM_LIMIT),
        name=f"attn_d{dil}",
    )(q, k, v, bias)


def _attn_merge_groups(o_refs, lse_refs, z):
    slabs = []
    for s in range(SLABS):
        lses = [ref[0, s] for ref in lse_refs]
        m = jnp.maximum(jnp.maximum(lses[0], lses[1]), lses[2])
        es = [jnp.exp(l - m) for l in lses]
        acc = o_refs[0][0, s] * es[0] + o_refs[1][0, s] * es[1] + o_refs[2][0, s] * es[2]
        acc = acc / (es[0] + es[1] + es[2])
        slabs.append(_bf(acc * _silu(z[:, s * LANES:(s + 1) * LANES])))
    return jnp.concatenate(slabs, axis=1)


def _gdn_kernel(qkv_ref, z_ref, small_ref, cw_ref, alog_ref, dtb_ref, og_ref,
                ones_ref, bd_ref, lv_ref, eye_ref, selb_ref,
                out_ref,
                tail_ref, st_ref, q_s, k_s, vb_s, kb_s, ce_s, qe_s, oi_s, corr_s, add_s, *, tb):
    nchunk = tb // B_CHUNK

    @pl.when(pl.program_id(1) == 0)
    def _():
        tail_ref[...] = jnp.zeros_like(tail_ref)
        st_ref[...] = jnp.zeros_like(st_ref)

    ones_bd = ones_ref[...]
    bd16 = ones_bd
    bdmask = bd_ref[...]
    lv = lv_ref[...]
    eye_cat = eye_ref[...]

    def prologue(r, carry):
        rows = pl.ds(pl.multiple_of(r * PRO_ROWS, PRO_ROWS), PRO_ROWS)
        halo = pl.ds(pl.multiple_of(jnp.maximum(r * PRO_ROWS - SUBLANES, 0), SUBLANES), SUBLANES)
        tail = jnp.where(r == 0, tail_ref[...], qkv_ref[0, halo, :])
        sm = small_ref[0, rows, :]
        g = -jnp.exp(alog_ref[...]) * _softplus(sm + dtb_ref[...])
        ce_s[rows, :] = _expand_heads(_seg_cumsum(g, B_CHUNK), SM_BDEC)
        beta = _expand_heads_mxu(jax.nn.sigmoid(sm), selb_ref[...])
        for part in range(3):
            cols = slice(part * BRANCH_WIDTH, (part + 1) * BRANCH_WIDTH)
            y = _silu(_causal_conv(qkv_ref[0, rows, cols], tail[:, cols], cw_ref[:, cols]))
            if part == 0:
                q_s[rows, :] = y * lax.rsqrt(_seg_sum(y * y, ones_bd) + EPS) * (HEAD_DIM ** -0.5)
            elif part == 1:
                kn = y * lax.rsqrt(_seg_sum(y * y, ones_bd) + EPS)
                k_s[rows, :] = kn
                kb_s[rows, :] = kn * beta
            else:
                vb_s[rows, :] = y * beta
        return carry

    lax.fori_loop(0, tb // PRO_ROWS, prologue, 0)
    tail_ref[...] = qkv_ref[0, tb - SUBLANES:, :]

    il = lax.broadcasted_iota(jnp.int32, (B_CHUNK, BRANCH_WIDTH), 0)
    jl = lax.broadcasted_iota(jnp.int32, (B_CHUNK, BRANCH_WIDTH), 1) & (HEAD_DIM - 1)
    causal = il >= jl

    def intra_group(i, carry):
        ks = range(B_UNROLL)
        rows = [pl.ds(pl.multiple_of((i * B_UNROLL + k) * B_CHUNK, B_CHUNK), B_CHUNK) for k in ks]
        cc = [ce_s[r, :] for r in rows]
        kc = [k_s[r, :] for r in rows]
        qc = [q_s[r, :] for r in rows]
        kb = [kb_s[r, :] for r in rows]
        gamma = [jnp.exp(jnp.where(causal, c_ - jnp.sum(c_ * eye_cat, axis=0, keepdims=True), -jnp.inf))
                 for c_ in cc]
        prod = [_dot_nt(_bf(jnp.concatenate([kb[k], qc[k]], axis=0)), _stack(kc[k], bd16)) for k in ks]
        a_kk = [jnp.where(il > jl, prod[k][:B_CHUNK] * gamma[k], 0.0) for k in ks]
        a_qk = [_bf(prod[k][B_CHUNK:] * gamma[k]) for k in ks]
        t = [eye_cat - jnp.where(lv == 0, a, 0.0) for a in a_kk]
        for lb in range(1, int(math.log2(B_CHUNK))):
            x = [_dot(_bf(t[k]), _stack(jnp.where(lv == lb, a_kk[k], 0.0), bd16)) for k in ks]
            t = [t[k] - _dot(_bf(x[k]), _stack(t[k], bd16)) for k in ks]
        ecum = [jnp.exp(c_) for c_ in cc]
        uw = [_dot(_bf(t[k]), jnp.concatenate([_stack(vb_s[rows[k], :], bd16),
                                               _stack(kb[k] * ecum[k], bd16)], axis=1)) for k in ks]
        u = [a[:, :BRANCH_WIDTH] for a in uw]
        w = [a[:, BRANCH_WIDTH:] for a in uw]
        both = [_dot(a_qk[k], jnp.concatenate([_stack(u[k], bd16), _stack(w[k], bd16)], axis=1)) for k in ks]
        k_dec = [_bf(kc[k] * jnp.exp(cc[k][B_CHUNK - 1:B_CHUNK, :] - cc[k])) for k in ks]
        wu = [_dot_tn(_bf(jnp.concatenate([w[k], u[k]], axis=1)), k_dec[k]) for k in ks]
        for k in ks:
            oi_s[rows[k], :] = both[k][:, :BRANCH_WIDTH]
            qe_s[rows[k], :] = qc[k] * ecum[k] - both[k][:, BRANCH_WIDTH:]
            corr_s[i * B_UNROLL + k] = _bf(-(wu[k][:BRANCH_WIDTH] * bdmask))
            add_s[i * B_UNROLL + k] = wu[k][BRANCH_WIDTH:] * bdmask
        return carry

    lax.fori_loop(0, nchunk // B_UNROLL, intra_group, 0)

    og = og_ref[...]
    st = st_ref[...]
    for c in range(nchunk):
        rows = slice(c * B_CHUNK, (c + 1) * B_CHUNK)
        st16 = _bf(st)
        clast = ce_s[(c + 1) * B_CHUNK - 1:(c + 1) * B_CHUNK, :]
        st = jnp.exp(clast) * st + _dot(st16, corr_s[c]) + add_s[c]
        o = _dot_nt(_bf(qe_s[rows, :]), st16) + oi_s[rows, :]
        ms = _seg_sum(o * o, ones_bd) * (1.0 / HEAD_DIM)
        on = o * lax.rsqrt(ms + EPS) * og
        out_ref[0, rows, :] = _bf(on * _silu(z_ref[0, rows, :]))
    st_ref[...] = st


def _gdn(bqkv, z_all, small, conv_w, alog_row, dtb_row, og_row, bsz, seq, tb):
    c = _NPC
    nchunk = tb // B_CHUNK
    assert nchunk % B_UNROLL == 0
    seq_blk = lambda w, col: pl.BlockSpec((1, tb, w), lambda b, j: (b, j, col))
    consts = [jnp.asarray(c["bd"], BF16), jnp.asarray(c["bd"]), jnp.asarray(c["lv"]), jnp.asarray(c["eye_cat"]),
              _sel_matrix(SM_BETA)]
    act = pltpu.VMEM((tb, BRANCH_WIDTH), F32)
    sq = (nchunk, BRANCH_WIDTH, BRANCH_WIDTH)
    return pl.pallas_call(
        functools.partial(_gdn_kernel, tb=tb),
        grid=(bsz, seq // tb),
        in_specs=[seq_blk(3 * BRANCH_WIDTH, 0), seq_blk(BRANCH_WIDTH, 1), seq_blk(SMALL_W, 0),
                  _full_spec(conv_w.shape), _full_spec((1, SMALL_W)), _full_spec((1, SMALL_W)),
                  _full_spec((1, BRANCH_WIDTH))] + [_full_spec(a.shape) for a in consts],
        out_specs=seq_blk(BRANCH_WIDTH, 0),
        out_shape=jax.ShapeDtypeStruct((bsz, seq, BRANCH_WIDTH), BF16),
        scratch_shapes=[pltpu.VMEM((SUBLANES, 3 * BRANCH_WIDTH), F32),
                        pltpu.VMEM((BRANCH_WIDTH, BRANCH_WIDTH), F32)] + [act] * 7
                       + [pltpu.VMEM(sq, BF16), pltpu.VMEM(sq, F32)],
        compiler_params=pltpu.CompilerParams(dimension_semantics=("arbitrary", "arbitrary"),
                                             vmem_limit_bytes=VMEM_LIMIT),
        name="gdn",
    )(bqkv.reshape(bsz, seq, -1), z_all.reshape(bsz, seq, -1), small.reshape(bsz, seq, -1),
      conv_w, alog_row, dtb_row, og_row, *consts)


def _ssd_kernel(xbc_ref, z_ref, small_ref, cw_ref, cb_ref, alog_ref, dtb_ref, dskip_ref, og_ref,
                bd_ref, eye_ref, gm_ref, gmt_ref, seld_ref,
                out_ref,
                tail_ref, st_ref, xs_s, bm_s, cm_s, dt_s, ac_s, *, tb):
    nchunk = tb // C_CHUNK

    @pl.when(pl.program_id(1) == 0)
    def _():
        tail_ref[...] = jnp.zeros_like(tail_ref)
        st_ref[...] = jnp.zeros_like(st_ref)

    bd16 = bd_ref[...]
    eye_cat = eye_ref[...]
    gmask16 = gm_ref[...]
    gmask_t = gmt_ref[...]

    def prologue(r, carry):
        rows = pl.ds(pl.multiple_of(r * PRO_ROWS, PRO_ROWS), PRO_ROWS)
        halo = pl.ds(pl.multiple_of(jnp.maximum(r * PRO_ROWS - SUBLANES, 0), SUBLANES), SUBLANES)
        tail = jnp.where(r == 0, tail_ref[...], xbc_ref[0, halo, :])
        for part, dst in enumerate((xs_s, bm_s, cm_s)):
            cols = slice(part * BRANCH_WIDTH, (part + 1) * BRANCH_WIDTH)
            dst[rows, :] = _silu(_causal_conv(xbc_ref[0, rows, cols], tail[:, cols], cw_ref[:, cols])
                                 + cb_ref[:, cols])
        return carry

    lax.fori_loop(0, tb // PRO_ROWS, prologue, 0)
    tail_ref[...] = xbc_ref[0, tb - SUBLANES:, :]
    dt = _softplus(small_ref[0] + dtb_ref[...])
    a = -jnp.exp(alog_ref[...]) * dt
    dt_s[...] = _expand_heads_mxu(dt, seld_ref[...])
    ac_s[...] = _expand_heads(_seg_cumsum(a, C_CHUNK), SM_CDT)

    il = lax.broadcasted_iota(jnp.int32, (C_CHUNK, BRANCH_WIDTH), 0)
    jl = lax.broadcasted_iota(jnp.int32, (C_CHUNK, BRANCH_WIDTH), 1) & (HEAD_DIM - 1)
    causal = il >= jl
    og = og_ref[...]
    dskip = dskip_ref[...]
    half = BRANCH_WIDTH // C_GROUPS

    def chunk_group(i, carry):
        ks = range(C_UNROLL)
        rows = [pl.ds(pl.multiple_of((i * C_UNROLL + k) * C_CHUNK, C_CHUNK), C_CHUNK) for k in ks]
        xc = [xs_s[r, :] for r in rows]
        ac = [ac_s[r, :] for r in rows]
        cc16 = [_bf(cm_s[r, :]) for r in rows]
        bc16 = [_bf(bm_s[r, :]) for r in rows]
        xdt = [xc[k] * dt_s[rows[k], :] for k in ks]
        seg = [jnp.exp(jnp.where(causal, a - jnp.sum(a * eye_cat, axis=0, keepdims=True), -jnp.inf)) for a in ac]
        scores = [_dot_nt(cc16[k], jnp.concatenate([bc16[k]] * N_HEADS, axis=0) * gmask16) * seg[k] for k in ks]
        y_diag = [_dot(_bf(scores[k]), _stack(xdt[k], bd16)) for k in ks]
        alast = [a[C_CHUNK - 1:C_CHUNK, :] for a in ac]
        delta = [gmask_t * _dot_tn(bc16[k], _bf(xdt[k] * jnp.exp(alast[k] - ac[k]))) for k in ks]
        st = st_ref[...]
        sts = []
        for k in ks:
            sts.append(_bf(st))
            st = jnp.exp(alast[k]) * st + delta[k]
        st_ref[...] = st
        y_off = [_dot(cc16[k], sts[k]) * jnp.exp(ac[k]) for k in ks]
        for k in ks:
            yv = (y_diag[k] + y_off[k] + dskip * xc[k]) * _silu(z_ref[0, rows[k], :])
            for gi in range(C_GROUPS):
                yg = yv[:, gi * half:(gi + 1) * half]
                ms = jnp.mean(yg * yg, axis=-1, keepdims=True)
                out_ref[0, rows[k], gi * half:(gi + 1) * half] = _bf(
                    yg * lax.rsqrt(ms + EPS) * og[:, gi * half:(gi + 1) * half])
        return carry

    lax.fori_loop(0, nchunk // C_UNROLL, chunk_group, 0)


def _ssd(cxbc, z_all, small, conv_w, conv_b, alog_row, dtb_row, dskip_row, og_row, bsz, seq, tb):
    c = _NPC
    assert (tb // C_CHUNK) % C_UNROLL == 0
    seq_blk = lambda w, col: pl.BlockSpec((1, tb, w), lambda b, j: (b, j, col))
    consts = [jnp.asarray(c["bd"], BF16), jnp.asarray(c["eye_cat"]), jnp.asarray(c["gmask"], BF16),
              jnp.asarray(c["gmask"].T.copy()), _sel_matrix(SM_CDT)]
    act = pltpu.VMEM((tb, BRANCH_WIDTH), F32)
    return pl.pallas_call(
        functools.partial(_ssd_kernel, tb=tb),
        grid=(bsz, seq // tb),
        in_specs=[seq_blk(3 * BRANCH_WIDTH, 0), seq_blk(BRANCH_WIDTH, 2), seq_blk(SMALL_W, 0),
                  _full_spec(conv_w.shape), _full_spec((1, 3 * BRANCH_WIDTH)), _full_spec((1, SMALL_W)),
                  _full_spec((1, SMALL_W)), _full_spec((1, BRANCH_WIDTH)), _full_spec((1, BRANCH_WIDTH))]
                 + [_full_spec(a.shape) for a in consts],
        out_specs=seq_blk(BRANCH_WIDTH, 0),
        out_shape=jax.ShapeDtypeStruct((bsz, seq, BRANCH_WIDTH), BF16),
        scratch_shapes=[pltpu.VMEM((SUBLANES, 3 * BRANCH_WIDTH), F32),
                        pltpu.VMEM((BRANCH_WIDTH, BRANCH_WIDTH), F32)] + [act] * 5,
        compiler_params=pltpu.CompilerParams(dimension_semantics=("arbitrary", "arbitrary"),
                                             vmem_limit_bytes=VMEM_LIMIT),
        name="ssd",
    )(cxbc.reshape(bsz, seq, -1), z_all.reshape(bsz, seq, -1), small.reshape(bsz, seq, -1),
      conv_w, conv_b, alog_row, dtb_row, dskip_row, og_row, *consts)


def _gla_kernel(qkv_ref, z_ref, small_ref, w2_ref, bg_ref, og_ref, ones_ref, dm_ref, cm_ref,
                out_ref,
                st_ref, qg_s, kg_s, k_s, cum_s, *, tb):
    @pl.when(pl.program_id(1) == 0)
    def _():
        st_ref[...] = jnp.zeros_like(st_ref)

    qkv = qkv_ref[0]
    q = qkv[:, :D_KEY] * (D_DK ** -0.5)
    k = qkv[:, D_KEY:2 * D_KEY]
    sm = small_ref[0]
    s_hi = _bf(sm)
    s_lo = _bf(sm - s_hi.astype(F32))
    w2 = w2_ref[...]
    w_hi = _bf(w2)
    w_lo = _bf(w2 - w_hi.astype(F32))
    gl = _dot(s_hi, w_hi) + _dot(s_lo, w_hi) + _dot(s_hi, w_lo) + bg_ref[...]
    glog = jax.nn.log_sigmoid(gl) * (1.0 / D_GATE_NORM)
    cum = _seg_cumsum(glog, D_CHUNK)
    qg_s[...] = q * jnp.exp(cum)
    kg_s[...] = k * jnp.exp(-cum)
    k_s[...] = k
    cum_s[...] = cum

    cmask = cm_ref[...] > 0.0
    klane = lax.broadcasted_iota(jnp.int32, (1, D_KEY), 1) // D_DK
    vlane = lax.broadcasted_iota(jnp.int32, (1, BRANCH_WIDTH), 1) // D_DV
    dmask = dm_ref[...]
    ones_bd = ones_ref[...]
    og = og_ref[...]
    per_group = D_GROUP // D_CHUNK

    cmask4 = jnp.concatenate([cmask] * D_HEADS, axis=0)

    def groups(it, carry):
        gs = range(D_UNROLL)
        rows = [pl.ds(pl.multiple_of((it * D_UNROLL + g) * D_GROUP, D_GROUP), D_GROUP) for g in gs]
        qgg = [qg_s[r, :] for r in rows]
        cc = [cum_s[r, :] for r in rows]
        vg = [_bf(qkv_ref[0, r, 2 * D_KEY:]) for r in rows]
        a = []
        for g in gs:
            zero = jnp.zeros_like(qgg[g])
            qs = _bf(jnp.concatenate([jnp.where(klane == h, qgg[g], zero) for h in range(D_HEADS)], axis=0))
            a.append(_dot_nt(qs, _bf(kg_s[rows[g], :])))
        av = [_dot(_bf(jnp.where(cmask4, a[g], 0.0)), vg[g]) for g in gs]
        chunks = [(g, c) for g in gs for c in range(per_group)]
        clast, delta = {}, {}
        for g, c in chunks:
            sl = slice(c * D_CHUNK, (c + 1) * D_CHUNK)
            ccc = cc[g][sl]
            clast[g, c] = ccc[D_CHUNK - 1:D_CHUNK, :]
            kd = k_s[rows[g], :][sl] * jnp.exp(clast[g, c] - ccc)
            delta[g, c] = dmask * _dot_tn(_bf(kd), vg[g][sl])
        dcol = {}
        for g in gs:
            dec_rows = jnp.concatenate([jnp.broadcast_to(jnp.exp(clast[g, c]), (D_CHUNK, D_KEY))
                                        for c in range(per_group)], axis=0)
            dec_t = dec_rows.T
            for c in range(per_group):
                dcol[g, c] = dec_t[:, c * D_CHUNK:c * D_CHUNK + 1]
        st = st_ref[...]
        sts = {}
        for g, c in chunks:
            sts[g, c] = _bf(st)
            st = dcol[g, c] * st + delta[g, c]
        st_ref[...] = st
        for g in gs:
            o_inter = jnp.concatenate([_dot(_bf(qgg[g][c * D_CHUNK:(c + 1) * D_CHUNK]), sts[g, c])
                                       for c in range(per_group)], axis=0)
            o_intra = av[g][:D_GROUP]
            for h in range(1, D_HEADS):
                o_intra = jnp.where(vlane == h, av[g][h * D_GROUP:(h + 1) * D_GROUP], o_intra)
            o = o_inter + o_intra
            ms = _seg_sum(o * o, ones_bd) * (1.0 / D_DV)
            on = o * lax.rsqrt(ms + EPS) * og
            out_ref[0, rows[g], :] = _bf(on * _silu(z_ref[0, rows[g], :]))
        return carry

    lax.fori_loop(0, tb // (D_GROUP * D_UNROLL), groups, 0)


def _gla(dqkv, z_all, small, w2pad, bg_row, og_row, bsz, seq, tb):
    c = _NPC
    seq_blk = lambda w, col: pl.BlockSpec((1, tb, w), lambda b, j: (b, j, col))
    consts = [jnp.asarray(c["bd"], BF16), jnp.asarray(c["dmask"]), jnp.asarray(c["cmask"])]
    return pl.pallas_call(
        functools.partial(_gla_kernel, tb=tb),
        grid=(bsz, seq // tb),
        in_specs=[seq_blk(2 * BRANCH_WIDTH, 0), seq_blk(BRANCH_WIDTH, 3), seq_blk(SMALL_W, 0),
                  _full_spec((SMALL_W, D_KEY)), _full_spec((1, D_KEY)), _full_spec((1, BRANCH_WIDTH))]
                 + [_full_spec(a.shape) for a in consts],
        out_specs=seq_blk(BRANCH_WIDTH, 0),
        out_shape=jax.ShapeDtypeStruct((bsz, seq, BRANCH_WIDTH), BF16),
        scratch_shapes=[pltpu.VMEM((D_KEY, BRANCH_WIDTH), F32)] + [pltpu.VMEM((tb, D_KEY), F32)] * 4,
        compiler_params=pltpu.CompilerParams(dimension_semantics=("arbitrary", "arbitrary"),
                                             vmem_limit_bytes=VMEM_LIMIT),
        name="gla",
    )(dqkv.reshape(bsz, seq, -1), z_all.reshape(bsz, seq, -1), small.reshape(bsz, seq, -1),
      w2pad, bg_row, og_row, *consts)


def _merge_kernel(x_ref, ng_ref, o0, o1, o2, l0, l1, l2, za_ref, br1, br2, br3,
                  wg_ref, bg_ref, wb_ref, wo_ref, out_ref):
    x = x_ref[...]
    ms = jnp.mean(x * x, axis=-1, keepdims=True)
    hb = _bf(x * lax.rsqrt(ms + EPS) * ng_ref[...])
    br_a = _attn_merge_groups((o0, o1, o2), (l0, l1, l2), za_ref[...])
    merged = jnp.zeros(x.shape, F32)
    for i, br in enumerate((br_a, br1[...], br2[...], br3[...])):
        gate = jax.nn.sigmoid(_dot(hb, wg_ref[i]) + bg_ref[i])
        merged = merged + gate * _dot(br, wb_ref[i])
    out_ref[...] = x + _dot(_bf(merged), wo_ref[...])


def _merge(xf, norm_g, attn_os, attn_lses, z_all, brs, w_gate, b_gate, w_branch, w_out, seq, tm):
    t = xf.shape[0]
    tps = seq // tm
    row = lambda w: pl.BlockSpec((tm, w), lambda i: (i, 0))
    slab = pl.BlockSpec((1, SLABS, tm, LANES), lambda i: (i // tps, 0, i % tps, 0))
    return pl.pallas_call(
        _merge_kernel,
        grid=(t // tm,),
        in_specs=[row(D_MODEL), _full_spec((1, D_MODEL))] + [slab] * (2 * A_N_GROUPS) + [row(BRANCH_WIDTH)]
                 + [row(BRANCH_WIDTH)] * (N_BRANCH - 1)
                 + [_const_spec(w_gate.shape), _full_spec(b_gate.shape), _const_spec(w_branch.shape),
                    _const_spec(w_out.shape)],
        out_specs=row(D_MODEL),
        out_shape=jax.ShapeDtypeStruct((t, D_MODEL), F32),
        compiler_params=pltpu.CompilerParams(dimension_semantics=("arbitrary",),
                                             vmem_limit_bytes=VMEM_LIMIT),
        name="merge",
    )(xf, norm_g, *attn_os, *attn_lses, z_all, *brs, w_gate, b_gate, w_branch, w_out)


def _lane_row(vals, offset, width):
    return jnp.zeros((1, width), F32).at[0, offset:offset + vals.shape[0]].set(vals.astype(F32))


def _rope_rows():
    lane = np.arange(LANES) % HEAD_DIM
    inv_freq = ROPE_THETA ** (-jnp.arange(ROPE_HALF, dtype=F32) / ROPE_HALF)
    invf = jnp.where(lane < ROPE_DIM, inv_freq[lane % ROPE_HALF], 0.0).astype(F32)[None, :]
    sgn = np.where(lane < ROPE_HALF, -1.0, np.where(lane < ROPE_DIM, 1.0, 0.0)).astype(np.float32)[None, :]
    return invf, jnp.asarray(sgn)


def _w_in_layout():
    sizes = (A_QKV, A_QKV, A_QKV, BRANCH_WIDTH, 3 * BRANCH_WIDTH, BRANCH_WIDTH, N_HEADS, N_HEADS,
             3 * BRANCH_WIDTH, BRANCH_WIDTH, N_HEADS, D_KEY, D_KEY, BRANCH_WIDTH, BRANCH_WIDTH, D_GATE_RANK)
    names = ("a_q", "a_k", "a_v", "a_z", "b_qkv", "b_z", "b_beta", "b_a", "c_xbc", "c_z", "c_dt",
             "d_q", "d_k", "d_v", "d_z", "d_g")
    src = dict(zip(names, np.concatenate([[0], np.cumsum(sizes)[:-1]]).tolist()))
    total = int(sum(sizes))
    moves = [(W_AQ, src["a_q"], 3 * A_QKV + BRANCH_WIDTH),
             (W_Z + BRANCH_WIDTH, src["b_z"], BRANCH_WIDTH),
             (W_Z + 2 * BRANCH_WIDTH, src["c_z"], BRANCH_WIDTH),
             (W_Z + 3 * BRANCH_WIDTH, src["d_z"], BRANCH_WIDTH),
             (W_BQKV, src["b_qkv"], 3 * BRANCH_WIDTH),
             (W_CXBC, src["c_xbc"], 3 * BRANCH_WIDTH),
             (W_DQKV, src["d_q"], 2 * BRANCH_WIDTH)]
    small = []
    for lane0, name, width in ((SM_BETA, "b_beta", 2 * N_HEADS), (SM_CDT, "c_dt", N_HEADS),
                               (SM_DG, "d_g", D_GATE_RANK)):
        win = min(src[name] - lane0, total - SMALL_W)
        small.append((lane0 + width, win, (lane0 - (src[name] - win)) % SMALL_W))
    return total, moves, small


def _pack_kernel(w_ref, out_ref):
    _, moves, small = _w_in_layout()
    for dst, src, width in moves:
        out_ref[:, dst:dst + width] = _bf(w_ref[:, src:src + width])
    lane = lax.broadcasted_iota(jnp.int32, (1, SMALL_W), 1)
    blk = jnp.zeros((w_ref.shape[0], SMALL_W), F32)
    for lane_end, win, shift in reversed(small):
        piece = w_ref[:, win:win + SMALL_W]
        if shift:
            piece = pltpu.roll(piece, shift, axis=1)
        blk = jnp.where(lane < lane_end, piece, blk)
    out_ref[:, W_SMALL:W_SMALL + SMALL_W] = _bf(blk)


def _pack_w_in(w_in, layer):
    total, _, _ = _w_in_layout()
    assert w_in.shape[1:] == (D_MODEL, total)
    rows = D_MODEL // 4
    return pl.pallas_call(
        _pack_kernel,
        grid=(D_MODEL // rows,),
        in_specs=[pl.BlockSpec((None, rows, total), lambda i: (layer, i, 0))],
        out_specs=pl.BlockSpec((rows, W_TOTAL), lambda i: (i, 0)),
        out_shape=jax.ShapeDtypeStruct((D_MODEL, W_TOTAL), BF16),
        compiler_params=pltpu.CompilerParams(dimension_semantics=("arbitrary",)),
        name="pack_w_in",
    )(w_in)


def _layer(xf, pos, bsz, seq, p, tm, tb):
    tb_light = _pick_tile(seq, LIGHT_MIXER_TILE)
    invf, sgn = _rope_rows()
    ones_bd = jnp.asarray(_NPC["bd"], BF16)
    w_all = _pack_w_in(p["w_in_stacked"], p["layer"])
    ng = p["norm_g"][None, :]
    qn = jnp.tile(p["a_q_norm"], A_QKV // HEAD_DIM)[None, :]
    kn = jnp.tile(p["a_k_norm"], A_QKV // HEAD_DIM)[None, :]
    outs = _inproj(xf, pos, ng, w_all, qn, kn, invf, sgn, ones_bd, bsz, seq, tm)
    qs, ks, vs = outs[0:3], outs[3:6], outs[6:9]
    z_all, bqkv, cxbc, dqkv, small = outs[9:]

    os_, lses = [], []
    for gi in range(A_N_GROUPS):
        o, lse = _attn_group(qs[gi], ks[gi], vs[gi], gi)
        os_.append(o)
        lses.append(lse)

    br_b = _gdn(bqkv, z_all, small, p["b_conv_w"],
                _lane_row(p["b_a_log"], SM_BDEC, SMALL_W), _lane_row(p["b_dt_bias"], SM_BDEC, SMALL_W),
                jnp.tile(p["b_out_norm"], N_HEADS)[None, :], bsz, seq, tb).reshape(bsz * seq, BRANCH_WIDTH)

    br_c = _ssd(cxbc, z_all, small, p["c_conv_w"], p["c_conv_b"][None, :],
                _lane_row(p["c_a_log"], SM_CDT, SMALL_W), _lane_row(p["c_dt_bias"], SM_CDT, SMALL_W),
                jnp.repeat(p["c_d_skip"], HEAD_DIM)[None, :], p["c_out_norm"][None, :],
                bsz, seq, tb_light).reshape(bsz * seq, BRANCH_WIDTH)

    w2pad = jnp.zeros((SMALL_W, D_KEY), F32).at[SM_DG:SM_DG + D_GATE_RANK, :].set(p["d_gate_w2"])
    br_d = _gla(dqkv, z_all, small, w2pad, p["d_gate_b"][None, :],
                jnp.tile(p["d_out_norm"], D_HEADS)[None, :], bsz, seq, tb_light).reshape(bsz * seq, BRANCH_WIDTH)

    return _merge(xf, ng, os_, lses, z_all, (br_b, br_c, br_d), _bf(p["w_gate"]), p["b_gate"][:, None, :],
                  _bf(p["w_branch"]), _bf(p["w_out"]), seq, tm)


def _pick_tile(n, target):
    t = min(n, target)
    while n % t:
        t //= 2
    return t


def kernel(x, positions, norm_g, w_in, a_q_norm, a_k_norm, b_conv_w, b_a_log, b_dt_bias, b_out_norm,
           c_conv_w, c_conv_b, c_a_log, c_dt_bias, c_d_skip, c_out_norm, d_gate_w2, d_gate_b, d_out_norm,
           w_gate, b_gate, w_branch, w_out):
    bsz, seq, d_model = x.shape
    assert d_model == D_MODEL and seq % (2 * A_BLOCK * A_GROUPS[-1][1]) == 0
    xf = x.reshape(bsz * seq, d_model)
    tm = _pick_tile(seq, DENSE_TILE)
    assert tm % LANES == 0
    pos = positions.reshape(bsz * seq // tm, tm // LANES, LANES).astype(jnp.int32)
    tb = _pick_tile(seq, MIXER_TILE)
    names = ("norm_g", "w_in", "a_q_norm", "a_k_norm", "b_conv_w", "b_a_log", "b_dt_bias", "b_out_norm",
             "c_conv_w", "c_conv_b", "c_a_log", "c_dt_bias", "c_d_skip", "c_out_norm", "d_gate_w2",
             "d_gate_b", "d_out_norm", "w_gate", "b_gate", "w_branch", "w_out")
    stacked = (norm_g, w_in, a_q_norm, a_k_norm, b_conv_w, b_a_log, b_dt_bias, b_out_norm,
               c_conv_w, c_conv_b, c_a_log, c_dt_bias, c_d_skip, c_out_norm, d_gate_w2, d_gate_b,
               d_out_norm, w_gate, b_gate, w_branch, w_out)
    for layer in range(norm_g.shape[0]):
        p = {n: a[layer] for n, a in zip(names, stacked) if n != "w_in"}
        p["w_in_stacked"], p["layer"] = w_in, layer
        xf = _layer(xf, pos, bsz, seq, p, tm, tb)
    return xf.reshape(bsz, seq, d_model)
```
